```python
import math
import jax
import jax.numpy as jnp
from jax import lax
import numpy as np

D_MODEL = 2048
BATCH = 2
SEQ = 4096
DEPTH = 4
DEC_BATCH = 8
DEC_SEQ = 1
PAST_LEN = 16384
PAGE_SIZE = 128

N_MIXERS = 3
LAYER_KIND = tuple(i % N_MIXERS for i in range(DEPTH))
LAYER_SLOT = tuple(LAYER_KIND[:i].count(LAYER_KIND[i]) for i in range(DEPTH))
N_A = LAYER_KIND.count(0)
N_B = LAYER_KIND.count(1)
N_C = LAYER_KIND.count(2)

A_HEADS = 8
A_DK = D_MODEL // (2 * A_HEADS)
A_DV = D_MODEL // A_HEADS
A_CHUNK = 128
A_IN = 2 * A_HEADS * A_DK + 2 * A_HEADS * A_DV + 2 * A_HEADS
B_HEADS = 8
B_DK = D_MODEL // (2 * B_HEADS)
B_DV = 2 * B_DK
B_QK = B_HEADS * 2 * B_DK
B_IN = 2 * B_QK + B_HEADS * B_DV
C_HEADS = 16
C_DH = D_MODEL // C_HEADS
C_W = C_HEADS * C_DH
IDX_HEADS = 16
IDX_DIM = 64
TOPK_MAX = 256
C_IN = 3 * C_W + IDX_HEADS * IDX_DIM + IDX_DIM + IDX_HEADS
D_FF = 5504
N_ADA = 9
ROPE_THETA = 10000.0
EPS = 1e-6
Q_BLOCK = 128
DSA_Q_BLOCK = 64

kernel_name = 'hybrid_mlstm_diffattn_dsa_macaron_step'


def rmsnorm(x, g):
    xf = x.astype(jnp.float32)
    y = xf * lax.rsqrt(jnp.mean(xf * xf, -1, keepdims=True) + EPS)
    return (y * g.astype(jnp.float32)).astype(x.dtype)


def rope(x, pos):
    half = x.shape[-1] // 2
    inv = ROPE_THETA ** (-jnp.arange(half, dtype=jnp.float32) / half)
    ang = pos.astype(jnp.float32)[:, None] * inv[None, :]
    shp = (1, x.shape[1]) + (1,) * (x.ndim - 3) + (half,)
    cos = jnp.cos(ang).reshape(shp)
    sin = jnp.sin(ang).reshape(shp)
    xf = x.astype(jnp.float32)
    x1, x2 = xf[..., :half], xf[..., half:]
    return jnp.concatenate([x1 * cos - x2 * sin, x2 * cos + x1 * sin], -1).astype(x.dtype)


def ada_mod(c, w, b):
    return (jax.nn.silu(c) @ w + b).reshape(c.shape[0], N_ADA, D_MODEL)


def modulate(x, g, shift, scale):
    return rmsnorm(x, g) * (1 + scale[:, None, :]) + shift[:, None, :]


def ffn_sublayer(x, mod, sub, g, wi, wo):
    h = modulate(x, g, mod[:, 3 * sub], mod[:, 3 * sub + 1])
    gt, up = jnp.split(h @ wi, 2, -1)
    return x + 0.5 * mod[:, 3 * sub + 2][:, None, :] * ((jax.nn.silu(gt) * up) @ wo)


def mlstm_chunkwise(q, k, v, ig, fg, C0, n0, m0, chunk):
    bn, hn, t, dk = q.shape
    nc = t // chunk
    to_chunks = lambda a: jnp.moveaxis(a.reshape(bn, hn, nc, chunk, *a.shape[3:]), 2, 0)
    logf = jax.nn.log_sigmoid(fg)
    xs = (to_chunks(q * dk ** -0.5), to_chunks(k), to_chunks(v), to_chunks(ig), to_chunks(logf))
    causal = jnp.tril(jnp.ones((chunk, chunk), bool))

    def step(carry, inp):
        C, n, m = carry
        qc, kc, vc, ic, fc = inp
        b = jnp.cumsum(fc, -1)
        dmat = jnp.where(causal, b[..., :, None] - b[..., None, :] + ic[..., None, :], -jnp.inf)
        inter = b + m[..., None]
        m_t = jnp.maximum(inter, jnp.max(dmat, -1))
        w_intra = jnp.exp(dmat - m_t[..., None])
        w_inter = jnp.exp(inter - m_t)
        s = jnp.einsum('bhtd,bhsd->bhts', qc, kc) * w_intra
        num = jnp.einsum('bhts,bhsv->bhtv', s, vc) + w_inter[..., None] * jnp.einsum('bhvd,bhtd->bhtv', C, qc)
        den = jnp.sum(s, -1) + w_inter * jnp.einsum('bhd,bhtd->bht', n, qc)
        h = num / jnp.maximum(jnp.abs(den), jnp.exp(-m_t))[..., None]
        m_new = m_t[..., -1]
        w_s = jnp.exp(b[..., -1:] - b + ic - m_new[..., None])
        decay = jnp.exp(b[..., -1] + m - m_new)
        C_new = decay[..., None, None] * C + jnp.einsum('bhs,bhsv,bhsd->bhvd', w_s, vc, kc)
        n_new = decay[..., None] * n + jnp.einsum('bhs,bhsd->bhd', w_s, kc)
        return (C_new, n_new, m_new), h

    (C, n, m), hs = lax.scan(step, (C0, n0, m0), xs)
    return jnp.moveaxis(hs, 0, 2).reshape(bn, hn, t, -1), C, n, m


def mlstm_mixer(h, w_in, b_if, norm_g, w_out, C0, n0, m0, chunk):
    bn, t, _ = h.shape
    qk = A_HEADS * A_DK
    vw = A_HEADS * A_DV
    q, k, v, o, g = jnp.split(h @ w_in, [qk, 2 * qk, 2 * qk + vw, 2 * qk + 2 * vw], -1)
    heads = lambda a, d: a.reshape(bn, t, A_HEADS, d).transpose(0, 2, 1, 3).astype(jnp.float32)
    g = (g + b_if).astype(jnp.float32).reshape(bn, t, 2, A_HEADS).transpose(2, 0, 3, 1)
    hh, C, n, m = mlstm_chunkwise(heads(q, A_DK), heads(k, A_DK), heads(v, A_DV), g[0], g[1],
                                  C0.astype(jnp.float32), n0.astype(jnp.float32), m0.astype(jnp.float32), chunk)
    hh = rmsnorm(hh.transpose(0, 2, 1, 3), norm_g) * jax.nn.sigmoid(o.astype(jnp.float32).reshape(bn, t, A_HEADS, A_DV))
    return hh.reshape(bn, t, vw).astype(h.dtype) @ w_out, C, n, m


def diff_lambda(lam, layer_idx):
    lam_init = 0.8 - 0.6 * math.exp(-0.3 * layer_idx)
    lf = lam.astype(jnp.float32)
    return jnp.exp(jnp.sum(lf[0] * lf[1])) - jnp.exp(jnp.sum(lf[2] * lf[3])) + lam_init, lam_init


def diff_project(h, w_in, pos):
    bn, t, _ = h.shape
    q, k, v = jnp.split(h @ w_in, [B_QK, 2 * B_QK], -1)
    q = rope(q.reshape(bn, t, B_HEADS, 2, B_DK), pos)
    k = rope(k.reshape(bn, t, B_HEADS, 2, B_DK), pos)
    return q, k, v.reshape(bn, t, B_HEADS, B_DV)


def diff_attend(q, k, v, mask, lam):
    s = jnp.einsum('bqhcd,bshcd->bhcqs', q, k).astype(jnp.float32) * B_DK ** -0.5
    p = jax.nn.softmax(jnp.where(mask, s, -jnp.inf), -1)
    pw = p[:, :, 0] - lam * p[:, :, 1]
    return jnp.einsum('bhqs,bshv->bqhv', pw.astype(v.dtype), v)


def diff_prompt(q, k, v, lam):
    bn, t = q.shape[:2]
    nb = t // Q_BLOCK
    qb = jnp.moveaxis(q.reshape(bn, nb, Q_BLOCK, *q.shape[2:]), 1, 0)
    kpos = jnp.arange(t)

    def blk(args):
        qi, start = args
        qpos = start + jnp.arange(Q_BLOCK)
        return diff_attend(qi, k, v, kpos[None, :] <= qpos[:, None], lam)

    out = lax.map(blk, (qb, jnp.arange(nb) * Q_BLOCK))
    return jnp.moveaxis(out, 0, 1).reshape(bn, t, B_HEADS, B_DV)


def diff_sample(q, k, v, cache_k, cache_v, page_table, lam):
    bn, t = q.shape[:2]
    past = page_table.shape[1] * PAGE_SIZE
    k_all = jnp.concatenate([cache_k[page_table].reshape(bn, past, B_HEADS, 2, B_DK).astype(k.dtype), k], 1)
    v_all = jnp.concatenate([cache_v[page_table].reshape(bn, past, B_HEADS, B_DV).astype(v.dtype), v], 1)
    qpos = past + jnp.arange(t)
    return diff_attend(q, k_all, v_all, jnp.arange(past + t)[None, :] <= qpos[:, None], lam)


def diff_output(o, norm_g, lam_init, w_out):
    bn, t = o.shape[:2]
    o = rmsnorm(o, norm_g) * (1 - lam_init)
    return o.reshape(bn, t, B_HEADS * B_DV) @ w_out


def dsa_project(h, w_in, pos):
    bn, t, _ = h.shape
    splits = [C_W, 2 * C_W, 3 * C_W, 3 * C_W + IDX_HEADS * IDX_DIM, 3 * C_W + IDX_HEADS * IDX_DIM + IDX_DIM]
    q, k, v, qi, ki, wi = jnp.split(h @ w_in, splits, -1)
    q = rope(q.reshape(bn, t, C_HEADS, C_DH), pos)
    k = rope(k.reshape(bn, t, C_HEADS, C_DH), pos)
    qi = rope(qi.reshape(bn, t, IDX_HEADS, IDX_DIM), pos)
    ki = rope(ki, pos)
    return q, k, v.reshape(bn, t, C_HEADS, C_DH), qi, ki, wi


def index_scores(qi, wi, ki):
    s = jax.nn.relu(jnp.einsum('bqhd,bsd->bqhs', qi, ki).astype(jnp.float32) * IDX_DIM ** -0.5)
    return jnp.einsum('bqh,bqhs->bqs', wi.astype(jnp.float32) * IDX_HEADS ** -0.5, s)


def sparse_attend(q, k_sel, v_sel, valid):
    s = jnp.einsum('bqhd,bqkhd->bqhk', q, k_sel).astype(jnp.float32) * C_DH ** -0.5
    p = jax.nn.softmax(jnp.where(valid[:, :, None, :], s, -jnp.inf), -1)
    return jnp.einsum('bqhk,bqkhd->bqhd', p.astype(v_sel.dtype), v_sel)


def dsa_prompt(q, k, v, qi, wi, ki):
    bn, t = q.shape[:2]
    n_top = min(TOPK_MAX, t // 4)
    nb = t // DSA_Q_BLOCK
    blocks = lambda a: jnp.moveaxis(a.reshape(bn, nb, DSA_Q_BLOCK, *a.shape[2:]), 1, 0)
    kpos = jnp.arange(t)
    bidx = jnp.arange(bn)[:, None, None]

    def blk(args):
        qb, qib, wib, start = args
        qpos = start + jnp.arange(DSA_Q_BLOCK)
        sc = jnp.where((kpos[None, :] <= qpos[:, None])[None], index_scores(qib, wib, ki), -jnp.inf)
        _, idx = lax.top_k(sc, n_top)
        valid = idx <= qpos[None, :, None]
        return sparse_attend(qb, k[bidx, idx], v[bidx, idx], valid)

    out = lax.map(blk, (blocks(q), blocks(qi), blocks(wi), jnp.arange(nb) * DSA_Q_BLOCK))
    return jnp.moveaxis(out, 0, 1).reshape(bn, t, C_HEADS, C_DH)


def dsa_sample(q, k, v, qi, wi, ki, cache_k, cache_v, cache_ki, page_table):
    bn, t = q.shape[:2]
    past = page_table.shape[1] * PAGE_SIZE
    n_top = min(TOPK_MAX, (past + t) // 4)
    ki_all = jnp.concatenate([cache_ki[page_table].reshape(bn, past, IDX_DIM).astype(ki.dtype), ki], 1)
    qpos = past + jnp.arange(t)
    causal = jnp.arange(past + t)[None, :] <= qpos[:, None]
    sc = jnp.where(causal[None], index_scores(qi, wi, ki_all), -jnp.inf)
    _, idx = lax.top_k(sc, n_top)
    valid = idx <= qpos[None, :, None]
    from_past = idx < past
    pidx = jnp.minimum(idx, past - 1)
    bidx = jnp.arange(bn)[:, None, None]
    phys = page_table[bidx, pidx // PAGE_SIZE]
    slot = pidx % PAGE_SIZE
    nidx = jnp.clip(idx - past, 0, t - 1)
    sel = lambda pool, new: jnp.where(from_past[..., None, None], pool[phys, slot].astype(new.dtype), new[bidx, nidx])
    return sparse_attend(q, sel(cache_k, k), sel(cache_v, v), valid)


def setup_inputs(seed: int = 0) -> dict:
    key = jax.random.key(seed)
    ks = iter(jax.random.split(key, 40))
    nrm = lambda shape, scale=1.0: jax.random.normal(next(ks), shape, jnp.float32) * scale
    n_pages = PAST_LEN // PAGE_SIZE
    n_pool = (5 * DEC_BATCH * n_pages + 3) // 4
    perm = jax.random.permutation(next(ks), n_pool)
    page_table = perm[:DEC_BATCH * n_pages].reshape(DEC_BATCH, n_pages).astype(jnp.int32)
    f_bias = jnp.linspace(3.0, 6.0, A_HEADS, dtype=jnp.float32)
    a_b_if = jnp.concatenate([nrm((N_A, A_HEADS), 0.1), f_bias + nrm((N_A, A_HEADS), 0.1)], -1)
    return {
        'x_prompt': nrm((BATCH, SEQ, D_MODEL)),
        'x_sample': nrm((DEC_BATCH, DEC_SEQ, D_MODEL)),
        'c_prompt': nrm((BATCH, D_MODEL)),
        'c_sample': nrm((DEC_BATCH, D_MODEL)),
        'state_a_C': nrm((N_A, DEC_BATCH, A_HEADS, A_DV, A_DK)),
        'state_a_n': nrm((N_A, DEC_BATCH, A_HEADS, A_DK)),
        'state_a_m': nrm((N_A, DEC_BATCH, A_HEADS)),
        'cache_b_k': nrm((N_B, n_pool, PAGE_SIZE, B_HEADS, 2, B_DK)),
        'cache_b_v': nrm((N_B, n_pool, PAGE_SIZE, B_HEADS, B_DV)),
        'cache_c_k': nrm((N_C, n_pool, PAGE_SIZE, C_HEADS, C_DH)),
        'cache_c_v': nrm((N_C, n_pool, PAGE_SIZE, C_HEADS, C_DH)),
        'cache_c_kidx': nrm((N_C, n_pool, PAGE_SIZE, IDX_DIM)),
        'page_table': page_table,
        'ada_w': nrm((DEPTH, D_MODEL, N_ADA * D_MODEL), D_MODEL ** -0.5),
        'ada_b': nrm((DEPTH, N_ADA * D_MODEL), 0.02),
        'norm_g': 1.0 + nrm((DEPTH, 3, D_MODEL), 0.01),
        'ffn_wi': nrm((DEPTH, 2, D_MODEL, 2 * D_FF), D_MODEL ** -0.5),
        'ffn_wo': nrm((DEPTH, 2, D_FF, D_MODEL), D_FF ** -0.5),
        'a_w_in': nrm((N_A, D_MODEL, A_IN), D_MODEL ** -0.5),
        'a_b_if': a_b_if,
        'a_norm_g': 1.0 + nrm((N_A, A_HEADS, A_DV), 0.01),
        'a_w_out': nrm((N_A, A_HEADS * A_DV, D_MODEL), (A_HEADS * A_DV) ** -0.5),
        'b_w_in': nrm((N_B, D_MODEL, B_IN), D_MODEL ** -0.5),
        'b_lambda': nrm((N_B, 4, B_DK), 0.1),
        'b_norm_g': 1.0 + nrm((N_B, B_HEADS, B_DV), 0.01),
        'b_w_out': nrm((N_B, B_HEADS * B_DV, D_MODEL), (B_HEADS * B_DV) ** -0.5),
        'c_w_in': nrm((N_C, D_MODEL, C_IN), D_MODEL ** -0.5),
        'c_w_out': nrm((N_C, C_W, D_MODEL), C_W ** -0.5),
        'final_g': 1.0 + nrm((D_MODEL,), 0.01),
    }


def reference(x_prompt, x_sample, c_prompt, c_sample,
              state_a_C, state_a_n, state_a_m,
              cache_b_k, cache_b_v, cache_c_k, cache_c_v, cache_c_kidx,
              page_table,
              ada_w, ada_b, norm_g, ffn_wi, ffn_wo,
              a_w_in, a_b_if, a_norm_g, a_w_out,
              b_w_in, b_lambda, b_norm_g, b_w_out,
              c_w_in, c_w_out, final_g):
    f32 = jnp.float32
    bp, t_p = x_prompt.shape[:2]
    t_s = x_sample.shape[1]
    past = page_table.shape[1] * PAGE_SIZE
    pos_p = jnp.arange(t_p, dtype=jnp.int32)
    pos_s = past + jnp.arange(t_s, dtype=jnp.int32)
    xp, xs = x_prompt, x_sample
    aCp, anp, amp, aCs, ans, ams = [], [], [], [], [], []
    bkp, bvp, bks, bvs = [], [], [], []
    ckp, cvp, cip, cks, cvs, cis = [], [], [], [], [], []
    for l in range(DEPTH):
        kind, j = LAYER_KIND[l], LAYER_SLOT[l]
        mod_p = ada_mod(c_prompt, ada_w[l], ada_b[l])
        mod_s = ada_mod(c_sample, ada_w[l], ada_b[l])
        xp = ffn_sublayer(xp, mod_p, 0, norm_g[l, 0], ffn_wi[l, 0], ffn_wo[l, 0])
        xs = ffn_sublayer(xs, mod_s, 0, norm_g[l, 0], ffn_wi[l, 0], ffn_wo[l, 0])
        hp = modulate(xp, norm_g[l, 1], mod_p[:, 3], mod_p[:, 4])
        hs = modulate(xs, norm_g[l, 1], mod_s[:, 3], mod_s[:, 4])
        if kind == 0:
            z_C = jnp.zeros((bp, A_HEADS, A_DV, A_DK), f32)
            z_n = jnp.zeros((bp, A_HEADS, A_DK), f32)
            z_m = jnp.zeros((bp, A_HEADS), f32)
            op, Cp, n_p, m_p = mlstm_mixer(hp, a_w_in[j], a_b_if[j], a_norm_g[j], a_w_out[j], z_C, z_n, z_m, A_CHUNK)
            os_, Cs, n_s, m_s = mlstm_mixer(hs, a_w_in[j], a_b_if[j], a_norm_g[j], a_w_out[j],
                                            state_a_C[j], state_a_n[j], state_a_m[j], t_s)
            aCp.append(Cp); anp.append(n_p); amp.append(m_p)
            aCs.append(Cs); ans.append(n_s); ams.append(m_s)
        elif kind == 1:
            lam, lam_init = diff_lambda(b_lambda[j], l)
            qp, kp, vp = diff_project(hp, b_w_in[j], pos_p)
            op = diff_output(diff_prompt(qp, kp, vp, lam), b_norm_g[j], lam_init, b_w_out[j])
            qs, k_s, vs = diff_project(hs, b_w_in[j], pos_s)
            os_ = diff_output(diff_sample(qs, k_s, vs, cache_b_k[j], cache_b_v[j], page_table, lam),
                              b_norm_g[j], lam_init, b_w_out[j])
            bkp.append(kp); bvp.append(vp); bks.append(k_s); bvs.append(vs)
        else:
            qp, kp, vp, qip, kip, wip = dsa_project(hp, c_w_in[j], pos_p)
            op = dsa_prompt(qp, kp, vp, qip, wip, kip).reshape(bp, t_p, C_W) @ c_w_out[j]
            qs, k_s, vs, qis, kis, wis = dsa_project(hs, c_w_in[j], pos_s)
            os_ = dsa_sample(qs, k_s, vs, qis, wis, kis, cache_c_k[j], cache_c_v[j], cache_c_kidx[j],
                             page_table).reshape(xs.shape[0], t_s, C_W) @ c_w_out[j]
            ckp.append(kp); cvp.append(vp); cip.append(kip)
            cks.append(k_s); cvs.append(vs); cis.append(kis)
        xp = xp + mod_p[:, 5][:, None, :] * op.astype(xp.dtype)
        xs = xs + mod_s[:, 5][:, None, :] * os_.astype(xs.dtype)
        xp = ffn_sublayer(xp, mod_p, 2, norm_g[l, 2], ffn_wi[l, 1], ffn_wo[l, 1])
        xs = ffn_sublayer(xs, mod_s, 2, norm_g[l, 2], ffn_wi[l, 1], ffn_wo[l, 1])
    y_prompt = rmsnorm(xp, final_g)
    y_sample = rmsnorm(xs, final_g)
    return (y_prompt, y_sample,
            jnp.stack(aCp), jnp.stack(anp), jnp.stack(amp),
            jnp.stack(aCs), jnp.stack(ans), jnp.stack(ams),
            jnp.stack(bkp), jnp.stack(bvp), jnp.stack(bks), jnp.stack(bvs),
            jnp.stack(ckp), jnp.stack(cvp), jnp.stack(cip),
            jnp.stack(cks), jnp.stack(cvs), jnp.stack(cis))
```

```python
import functools
import math

import jax
import jax.numpy as jnp
from jax import lax
from jax.experimental import pallas as pl
from jax.experimental.pallas import tpu as pltpu

F32 = jnp.float32
BF16 = jnp.bfloat16
I32 = jnp.int32

EPS = 1e-6
ROPE_THETA = 10000.0
N_ADA = 9
TOPK_MAX = 256
PAGE_SIZE = 128
A_HEADS = 8
A_CHUNK = 128
B_HEADS = 8
C_HEADS = 16
IDX_HEADS = 16
IDX_DIM = 64

LANES = 128
SUBLANES_BF16 = 16
VMEM_LIMIT = 56 * 1024 * 1024
NEG = -1e30
INT_MIN = -2 ** 31


def _cparams(sem):
    return pltpu.CompilerParams(dimension_semantics=sem, vmem_limit_bytes=VMEM_LIMIT)


def _dot(a, b):
    return jnp.dot(a, b, preferred_element_type=F32)


def _dot_nt(a, b):
    return lax.dot_general(a, b, (((1,), (1,)), ((), ())), preferred_element_type=F32)


def _split(x):
    hi = x.astype(BF16)
    return hi, (x - hi.astype(F32)).astype(BF16)


def _dot3(a, b, dot=_dot):
    a_hi, a_lo = _split(a)
    b_hi, b_lo = _split(b)
    return dot(a_hi, b_hi) + (dot(a_hi, b_lo) + dot(a_lo, b_hi))


def _norm_mod(x, g, scale, shift):
    y = x * lax.rsqrt(jnp.mean(x * x, -1, keepdims=True) + EPS) * g
    return y * (1.0 + scale) + shift


def _log_sigmoid(x):
    return jnp.minimum(x, 0.0) - jnp.log1p(jnp.exp(-jnp.abs(x)))


def _ada_kernel(c_ref, w_ref, b_ref, o_ref):
    c = c_ref[...]
    a = (c * jax.nn.sigmoid(c)).astype(BF16)
    o_ref[0] = _dot(a, w_ref[0].astype(BF16)) + b_ref[0]


def ada_all(c_rows, ada_w, ada_b):
    depth, d, n = ada_w.shape
    r = c_rows.shape[0]
    tn = 1024
    return pl.pallas_call(
        _ada_kernel,
        grid=(depth, n // tn),
        in_specs=[pl.BlockSpec((r, d), lambda l, j: (0, 0)),
                  pl.BlockSpec((1, d, tn), lambda l, j: (l, 0, j)),
                  pl.BlockSpec((1, 1, tn), lambda l, j: (l, 0, j))],
        out_specs=pl.BlockSpec((1, r, tn), lambda l, j: (l, 0, j)),
        out_shape=jax.ShapeDtypeStruct((depth, r, n), F32),
        compiler_params=_cparams(("arbitrary", "arbitrary")),
        name="ada_mod",
    )(c_rows, ada_w, ada_b.reshape(depth, 1, n))


def _row_tile(want, rows_per_batch):
    return min(want, rows_per_batch)


def _mod_spec(mod, tm, tn, rows_per_batch, col_of):
    r = mod.shape[1]
    bpb = max(rows_per_batch // tm, 1)
    return pl.BlockSpec((1, r, tn), lambda i, j: (i // bpb, 0, col_of(j)))


def _ffn_kernel(*refs, nf, final):
    if final:
        (x_ref, sh_ref, sc_ref, gt_ref, g_ref, wg_ref, wu_ref, wo_ref, fg_ref,
         o_ref, h_scr, acc_scr) = refs
    else:
        (x_ref, sh_ref, sc_ref, gt_ref, g_ref, wg_ref, wu_ref, wo_ref,
         o_ref, h_scr, acc_scr) = refs
    f = pl.program_id(1)

    @pl.when(f == 0)
    def _():
        h = _norm_mod(x_ref[...], g_ref[...], sc_ref[0], sh_ref[0])
        h_scr[...] = h.astype(BF16)
        acc_scr[...] = jnp.zeros_like(acc_scr)

    h = h_scr[...]
    a = _dot(h, wg_ref[...])
    u = _dot(h, wu_ref[...])
    act = (a * jax.nn.sigmoid(a) * u).astype(BF16)
    acc_scr[...] += _dot(act, wo_ref[...])

    @pl.when(f == nf - 1)
    def _():
        y = x_ref[...] + 0.5 * gt_ref[0] * acc_scr[...]
        if final:
            y = y * lax.rsqrt(jnp.mean(y * y, -1, keepdims=True) + EPS) * fg_ref[...]
        o_ref[...] = y


FFN_TF = 512


def ffn_sublayer(x, shift, scale, gate, g, wi_p, wo_p, l, s, rows_per_batch, final_g=None):
    m, d = x.shape
    fp = wo_p.shape[2]
    tf = FFN_TF
    nf = fp // tf
    tm = _row_tile(512, rows_per_batch)
    final = final_g is not None
    in_specs = [
        pl.BlockSpec((tm, d), lambda i, f: (i, 0)),
        _mod_spec(shift, tm, d, rows_per_batch, lambda f: 0),
        _mod_spec(scale, tm, d, rows_per_batch, lambda f: 0),
        _mod_spec(gate, tm, d, rows_per_batch, lambda f: 0),
        pl.BlockSpec((1, d), lambda i, f: (0, 0)),
        pl.BlockSpec((None, None, d, tf), lambda i, f: (l, s, 0, f)),
        pl.BlockSpec((None, None, d, tf), lambda i, f: (l, s, 0, nf + f)),
        pl.BlockSpec((None, None, tf, d), lambda i, f: (l, s, f, 0)),
    ]
    args = [x, shift, scale, gate, g.reshape(1, d), wi_p, wi_p, wo_p]
    if final:
        in_specs.append(pl.BlockSpec((1, d), lambda i, f: (0, 0)))
        args.append(final_g.reshape(1, d))
    return pl.pallas_call(
        functools.partial(_ffn_kernel, nf=nf, final=final),
        grid=(m // tm, nf),
        in_specs=in_specs,
        out_specs=pl.BlockSpec((tm, d), lambda i, f: (i, 0)),
        out_shape=jax.ShapeDtypeStruct((m, d), F32),
        scratch_shapes=[pltpu.VMEM((tm, d), BF16), pltpu.VMEM((tm, d), F32)],
        compiler_params=_cparams(("arbitrary", "arbitrary")),
        name="ffn",
    )(*args)


def _rope_tile(y, cos, sin, rope):
    if rope == 128:
        r = pltpu.roll(y, 64, 1)
    else:
        lane = lax.broadcasted_iota(I32, y.shape, 1)
        r = jnp.where((lane % 64) < 32, pltpu.roll(y, 96, 1), pltpu.roll(y, 32, 1))
    return y * cos + r * sin


LO = "bf16 residual"


def _emit(y, kind):
    if kind == LO:
        return _split(y)[1]
    return y.astype(kind)


def _proj_kernel(*refs, rope, kinds, tn, precise):
    n_out = len(kinds)
    x_ref, sh_ref, sc_ref, g_ref, w_ref = refs[:5]
    pos = 5
    if rope:
        cos_ref, sin_ref = refs[5:7]
        pos = 7
    out_refs = refs[pos:pos + n_out]
    h_scr = refs[pos + n_out]
    j = pl.program_id(1)

    @pl.when(j == 0)
    def _():
        h_scr[...] = _norm_mod(x_ref[...], g_ref[...], sc_ref[0], sh_ref[0]).astype(h_scr.dtype)

    if precise:
        y = _dot3(h_scr[...], w_ref[...])
    else:
        y = _dot(h_scr[...], w_ref[...].astype(BF16))
    if rope:
        cos = cos_ref[...]
        sin = sin_ref[...]
        for c in range(tn // LANES):
            sl = slice(c * LANES, (c + 1) * LANES)
            yc = _rope_tile(y[:, sl], cos, sin, rope)
            for o_ref, kind in zip(out_refs, kinds):
                o_ref[:, sl] = _emit(yc, kind)
    else:
        for o_ref, kind in zip(out_refs, kinds):
            o_ref[...] = _emit(y, kind)


def proj(x, shift, scale, g, w, slot, col_off, n_cols, rows_per_batch, dtypes,
         rope=0, tables=None, precise=False):
    m, d = x.shape
    tn = min(512, n_cols)
    tm = _row_tile(1024, rows_per_batch)
    off = col_off // tn
    assert col_off % tn == 0 and n_cols % tn == 0
    in_specs = [
        pl.BlockSpec((tm, d), lambda i, j: (i, 0)),
        _mod_spec(shift, tm, d, rows_per_batch, lambda j: 0),
        _mod_spec(scale, tm, d, rows_per_batch, lambda j: 0),
        pl.BlockSpec((1, d), lambda i, j: (0, 0)),
        pl.BlockSpec((None, d, tn), lambda i, j: (slot, 0, off + j)),
    ]
    args = [x, shift, scale, g.reshape(1, d), w]
    if rope:
        cos, sin = tables
        nbt = max(cos.shape[0] // tm, 1)
        in_specs += [pl.BlockSpec((tm, LANES), lambda i, j: (i % nbt, 0))] * 2
        args += [cos, sin]
    out = pl.pallas_call(
        functools.partial(_proj_kernel, rope=rope, kinds=tuple(dtypes), tn=tn, precise=precise),
        grid=(m // tm, n_cols // tn),
        in_specs=in_specs,
        out_specs=[pl.BlockSpec((tm, tn), lambda i, j: (i, j)) for _ in dtypes],
        out_shape=[jax.ShapeDtypeStruct((m, n_cols), BF16 if dt == LO else dt) for dt in dtypes],
        scratch_shapes=[pltpu.VMEM((tm, d), F32 if precise else BF16)],
        compiler_params=_cparams(("arbitrary", "arbitrary")),
        name="proj",
    )(*args)
    return out


def _outproj_kernel(a_ref, x_ref, gt_ref, w_ref, o_ref):
    y = _dot(a_ref[...], w_ref[...].astype(BF16))
    o_ref[...] = x_ref[...] + gt_ref[0] * y


def outproj(a, x, gate, w, slot, rows_per_batch):
    m, d = x.shape
    k = a.shape[1]
    tn = 512
    tm = _row_tile(1024, rows_per_batch)
    return pl.pallas_call(
        _outproj_kernel,
        grid=(m // tm, d // tn),
        in_specs=[pl.BlockSpec((tm, k), lambda i, j: (i, 0)),
                  pl.BlockSpec((tm, tn), lambda i, j: (i, j)),
                  _mod_spec(gate, tm, tn, rows_per_batch, lambda j: j),
                  pl.BlockSpec((None, k, tn), lambda i, j: (slot, 0, j))],
        out_specs=pl.BlockSpec((tm, tn), lambda i, j: (i, j)),
        out_shape=jax.ShapeDtypeStruct((m, d), F32),
        compiler_params=_cparams(("arbitrary", "arbitrary")),
        name="outproj",
    )(a, x, gate, w)


def _mlstm_kernel(q_ref, k_ref, v_ref, o_ref, gates_ref, bias_ref, ng_ref,
                  hh_ref, c_ref, n_ref, m_ref, *, heads, dk, dv, chunk):
    c_idx = pl.program_id(1)

    @pl.when(c_idx == 0)
    def _():
        c_ref[...] = jnp.zeros_like(c_ref)
        n_ref[...] = jnp.zeros_like(n_ref)
        m_ref[...] = jnp.zeros_like(m_ref)

    L = chunk
    g = gates_ref[...] + bias_ref[...]
    gt = g.T
    lf = _log_sigmoid(g)
    lft = _log_sigmoid(gt)
    row = lax.broadcasted_iota(I32, (L, L), 0)
    col = lax.broadcasted_iota(I32, (L, L), 1)
    causal = col <= row
    qscale = dk ** -0.5
    for h in range(heads):
        icol = g[:, h:h + 1]
        fcol = lf[:, heads + h:heads + h + 1]
        irow = gt[h:h + 1, :]
        frow = lft[heads + h:heads + h + 1, :]
        b_col = jnp.sum(jnp.where(causal, frow, 0.0), axis=1, keepdims=True)
        b_row = jnp.sum(jnp.where(row <= col, fcol, 0.0), axis=0, keepdims=True)
        dmat = jnp.where(causal, b_col - b_row + irow, NEG)
        m_prev = m_ref[0, h:h + 1, 0:1]
        inter = b_col + m_prev
        m_t = jnp.maximum(inter, jnp.max(dmat, axis=1, keepdims=True))
        w_intra = jnp.exp(dmat - m_t)
        w_inter = jnp.exp(inter - m_t)
        qf = q_ref[:, h * dk:(h + 1) * dk] * qscale
        kf = k_ref[:, h * dk:(h + 1) * dk]
        vf = v_ref[:, h * dv:(h + 1) * dv]
        s = _dot3(qf, kf, _dot_nt) * w_intra
        c_prev = c_ref[0, h]
        n_prev = n_ref[0, h:h + 1, :]
        num = _dot3(s, vf) + w_inter * _dot3(qf, c_prev, _dot_nt)
        den = jnp.sum(s, axis=1, keepdims=True) + w_inter * jnp.sum(qf * n_prev, axis=1, keepdims=True)
        hh = num / jnp.maximum(jnp.abs(den), jnp.exp(-m_t))
        m_new = m_t[L - 1:L, :]
        b_last = b_col[L - 1:L, :]
        w_s = jnp.exp(b_last - b_col + icol - m_new)
        decay = jnp.exp(b_last + m_prev - m_new)
        vw_t = (vf * w_s).T
        c_ref[0, h] = decay * c_prev + _dot3(vw_t, kf)
        n_ref[0, h:h + 1, :] = decay * n_prev + jnp.sum(kf * w_s, axis=0, keepdims=True)
        m_ref[0, h:h + 1, :] = jnp.broadcast_to(m_new, (1, LANES))
        hn = hh * lax.rsqrt(jnp.mean(hh * hh, -1, keepdims=True) + EPS) * ng_ref[:, h * dv:(h + 1) * dv]
        hn = hn * jax.nn.sigmoid(o_ref[:, h * dv:(h + 1) * dv])
        hh_ref[:, h * dv:(h + 1) * dv] = hn.astype(BF16)


def mlstm_prompt(y, gates, b_if, norm_g, bn, t):
    heads = A_HEADS
    dv = norm_g.shape[-1]
    dk = dv // 2
    qk = heads * dk
    vw = heads * dv
    L = A_CHUNK
    nc = t // L
    bias = jnp.zeros((1, LANES), F32).at[0, :2 * heads].set(b_if)
    row = lambda b, c: b * nc + c
    outs = pl.pallas_call(
        functools.partial(_mlstm_kernel, heads=heads, dk=dk, dv=dv, chunk=L),
        grid=(bn, nc),
        in_specs=[pl.BlockSpec((L, qk), lambda b, c: (row(b, c), 0)),
                  pl.BlockSpec((L, qk), lambda b, c: (row(b, c), 1)),
                  pl.BlockSpec((L, vw), lambda b, c: (row(b, c), 1)),
                  pl.BlockSpec((L, vw), lambda b, c: (row(b, c), 2)),
                  pl.BlockSpec((L, LANES), lambda b, c: (row(b, c), 0)),
                  pl.BlockSpec((1, LANES), lambda b, c: (0, 0)),
                  pl.BlockSpec((1, vw), lambda b, c: (0, 0))],
        out_specs=[pl.BlockSpec((L, vw), lambda b, c: (row(b, c), 0)),
                   pl.BlockSpec((1, heads, dv, dk), lambda b, c: (b, 0, 0, 0)),
                   pl.BlockSpec((1, heads, dk), lambda b, c: (b, 0, 0)),
                   pl.BlockSpec((1, heads, LANES), lambda b, c: (b, 0, 0))],
        out_shape=[jax.ShapeDtypeStruct((bn * t, vw), BF16),
                   jax.ShapeDtypeStruct((bn, heads, dv, dk), F32),
                   jax.ShapeDtypeStruct((bn, heads, dk), F32),
                   jax.ShapeDtypeStruct((bn, heads, LANES), F32)],
        compiler_params=_cparams(("arbitrary", "arbitrary")),
        name="mlstm_scan",
    )(y, y, y, y, gates, bias, norm_g.reshape(1, vw))
    hh, c_fin, n_fin, m_fin = outs
    return hh, c_fin, n_fin, m_fin[:, :, 0]


def _mlstm_step_kernel(q_ref, k_ref, v_ref, o_ref, ig_ref, fg_ref, c0_ref, n0_ref, m0_ref, ng_ref,
                       h_ref, c_ref, n_ref, m_ref, *, dk):
    cst = c0_ref[0]
    n0 = n0_ref[0]
    m0 = m0_ref[0]
    q = q_ref[0] * dk ** -0.5
    k = k_ref[0]
    v = v_ref[0]
    ig = ig_ref[0]
    fc = _log_sigmoid(fg_ref[0])
    inter = fc + m0
    m_t = jnp.maximum(inter, ig)
    w_intra = jnp.exp(ig - m_t)
    w_inter = jnp.exp(inter - m_t)
    s = jnp.sum(q * k, axis=-1, keepdims=True) * w_intra
    cq = jnp.sum(cst * q, axis=-1, keepdims=True)
    num = s * v + w_inter * cq
    den = s + w_inter * jnp.sum(n0 * q, axis=-1, keepdims=True)
    hh = num / jnp.maximum(jnp.abs(den), jnp.exp(-m_t))
    w_s = w_intra
    decay = jnp.exp(inter - m_t)
    c_ref[0] = decay * cst + w_s * (v * k)
    n_ref[0] = decay * n0 + w_s * k
    m_ref[0] = m_t
    hn = hh * lax.rsqrt(jnp.mean(hh * hh, axis=1, keepdims=True) + EPS) * ng_ref[...]
    h_ref[0] = hn * jax.nn.sigmoid(o_ref[0])


def mlstm_sample(y, gates, b_if, norm_g, c0, n0, m0, bs):
    heads = A_HEADS
    dv = norm_g.shape[-1]
    dk = dv // 2
    qk = heads * dk
    vw = heads * dv
    y = y[:bs]
    q = y[:, :qk].reshape(bs, heads, 1, dk)
    k = y[:, qk:2 * qk].reshape(bs, heads, 1, dk)
    v = y[:, 2 * qk:2 * qk + vw].reshape(bs, heads, dv, 1)
    o = y[:, 2 * qk + vw:].reshape(bs, heads, dv, 1)
    gb = gates[:bs, :2 * heads] + b_if
    ig = gb[:, :heads].reshape(bs, heads, 1, 1)
    fg = gb[:, heads:].reshape(bs, heads, 1, 1)
    spec = lambda *shape: pl.BlockSpec((1,) + shape, lambda b: (b,) + (0,) * len(shape))
    outs = pl.pallas_call(
        functools.partial(_mlstm_step_kernel, dk=dk),
        grid=(bs,),
        in_specs=[spec(heads, 1, dk), spec(heads, 1, dk), spec(heads, dv, 1), spec(heads, dv, 1),
                  spec(heads, 1, 1), spec(heads, 1, 1),
                  spec(heads, dv, dk), spec(heads, 1, dk), spec(heads, 1, 1),
                  pl.BlockSpec((heads, dv, 1), lambda b: (0, 0, 0))],
        out_specs=[spec(heads, dv, 1), spec(heads, dv, dk), spec(heads, 1, dk), spec(heads, 1, 1)],
        out_shape=[jax.ShapeDtypeStruct((bs, heads, dv, 1), F32),
                   jax.ShapeDtypeStruct((bs, heads, dv, dk), F32),
                   jax.ShapeDtypeStruct((bs, heads, 1, dk), F32),
                   jax.ShapeDtypeStruct((bs, heads, 1, 1), F32)],
        compiler_params=_cparams(("arbitrary",)),
        name="mlstm_step",
    )(q, k, v, o, ig, fg, c0, n0.reshape(bs, heads, 1, dk), m0.reshape(bs, heads, 1, 1),
      norm_g.reshape(heads, dv, 1))
    hh, c_new, n_new, m_new = outs
    return hh.reshape(bs, vw), c_new, n_new.reshape(bs, heads, dk), m_new.reshape(bs, heads)


def _diff_lambda(lam_ref, lam_init):
    l = lam_ref[...]
    a = jnp.sum(l[0:1, :] * l[1:2, :], axis=1, keepdims=True)
    b = jnp.sum(l[2:3, :] * l[3:4, :], axis=1, keepdims=True)
    return jnp.exp(a) - jnp.exp(b) + lam_init


def _diff_prompt_kernel(q_ref, k_ref, v_ref, lam_ref, ng_ref, o_ref, m_scr, l_scr, acc_scr,
                        *, tq, dk, lam_init):
    i = pl.program_id(2)
    scale = dk ** -0.5
    m_scr[...] = jnp.full_like(m_scr, NEG)
    l_scr[...] = jnp.zeros_like(l_scr)
    acc_scr[...] = jnp.zeros_like(acc_scr)
    row = lax.broadcasted_iota(I32, (tq, tq), 0)
    col = lax.broadcasted_iota(I32, (tq, tq), 1)

    def chunk(j, masked):
        start = pl.multiple_of(j * tq, tq)
        ks = k_ref[pl.ds(start, tq), :]
        vs = v_ref[pl.ds(start, tq), :]
        for c in range(2):
            s = _dot_nt(q_ref[:, c * dk:(c + 1) * dk], ks[:, c * dk:(c + 1) * dk]) * scale
            if masked:
                s = jnp.where(col <= row, s, NEG)
            m_prev = m_scr[c]
            m_new = jnp.maximum(m_prev, jnp.max(s, axis=1, keepdims=True))
            alpha = jnp.exp(m_prev - m_new)
            p = jnp.exp(s - m_new)
            l_scr[c] = alpha * l_scr[c] + jnp.sum(p, axis=1, keepdims=True)
            acc_scr[c] = alpha * acc_scr[c] + _dot(p.astype(BF16), vs)
            m_scr[c] = m_new

    def body(j, carry):
        chunk(j, False)
        return carry

    lax.fori_loop(0, i, body, 0)
    chunk(i, True)
    lam = _diff_lambda(lam_ref, lam_init)
    o = acc_scr[0] / l_scr[0] - lam * (acc_scr[1] / l_scr[1])
    o = o * lax.rsqrt(jnp.mean(o * o, -1, keepdims=True) + EPS) * ng_ref[...] * (1.0 - lam_init)
    o_ref[...] = o.astype(o_ref.dtype)


def diff_prompt(qb, kb, vb, lam, norm_g, lam_init, bn, t):
    heads = B_HEADS
    dv = norm_g.shape[-1]
    dk = dv // 2
    tq = min(256, t)
    nq = t // tq
    return pl.pallas_call(
        functools.partial(_diff_prompt_kernel, tq=tq, dk=dk, lam_init=lam_init),
        grid=(bn, heads, nq),
        in_specs=[pl.BlockSpec((tq, 2 * dk), lambda b, h, i: (b * nq + i, h)),
                  pl.BlockSpec((t, 2 * dk), lambda b, h, i: (b, h)),
                  pl.BlockSpec((t, dv), lambda b, h, i: (b, h)),
                  pl.BlockSpec(lam.shape, lambda b, h, i: (0, 0)),
                  pl.BlockSpec((1, dv), lambda b, h, i: (0, h))],
        out_specs=pl.BlockSpec((tq, dv), lambda b, h, i: (b * nq + i, h)),
        out_shape=jax.ShapeDtypeStruct((bn * t, heads * dv), BF16),
        scratch_shapes=[pltpu.VMEM((2, tq, 1), F32), pltpu.VMEM((2, tq, 1), F32),
                        pltpu.VMEM((2, tq, dv), F32)],
        compiler_params=_cparams(("arbitrary", "arbitrary", "arbitrary")),
        name="diff_prompt",
    )(qb, kb, vb, lam, norm_g.reshape(1, heads * dv))


def _paged_kernel(pt_ref, q_ref, kn_ref, vn_ref, k_ref, v_ref, *rest, mode, rows, dk, dv,
                  n_pages, lam_init):
    if mode == "diff":
        lam_ref, ng_ref, o_ref, qbd_scr, m_scr, l_scr, acc_scr = rest
    else:
        bias_ref, biasn_ref, o_ref, qbd_scr, m_scr, l_scr, acc_scr = rest
    p = pl.program_id(1)
    scale = dk ** -0.5
    width = rows * dk

    @pl.when(p == 0)
    def _():
        r = lax.broadcasted_iota(I32, (rows, width), 0)
        c = lax.broadcasted_iota(I32, (rows, width), 1)
        qbd_scr[...] = jnp.where(c // dk == r, jnp.broadcast_to(q_ref[0], (rows, width)), 0.0)
        m_scr[...] = jnp.full_like(m_scr, NEG)
        l_scr[...] = jnp.zeros_like(l_scr)
        acc_scr[...] = jnp.zeros_like(acc_scr)

    qbd = qbd_scr[...]
    s = _dot_nt(qbd.astype(BF16), k_ref[...].astype(BF16)) * scale
    if mode == "dsa":
        s = s + bias_ref[0, 0]
    m_prev = m_scr[...]
    m_new = jnp.maximum(m_prev, jnp.max(s, axis=1, keepdims=True))
    alpha = jnp.exp(m_prev - m_new)
    pe = jnp.exp(s - m_new)
    l_scr[...] = alpha * l_scr[...] + jnp.sum(pe, axis=1, keepdims=True)
    acc_scr[...] = alpha * acc_scr[...] + _dot(pe.astype(BF16), v_ref[...].astype(BF16))
    m_scr[...] = m_new

    @pl.when(p == n_pages - 1)
    def _():
        s_n = jnp.sum(qbd * kn_ref[0], axis=1, keepdims=True) * scale
        if mode == "dsa":
            s_n = s_n + biasn_ref[0][:, 0:1]
        m_prev = m_scr[...]
        m_new = jnp.maximum(m_prev, s_n)
        alpha = jnp.exp(m_prev - m_new)
        pn = jnp.exp(s_n - m_new)
        l_fin = alpha * l_scr[...] + pn
        acc = (alpha * acc_scr[...] + pn * vn_ref[0]) / l_fin
        if mode == "diff":
            lam = _diff_lambda(lam_ref, lam_init)
            for h in range(rows // 2):
                sl = slice(h * dv, (h + 1) * dv)
                o = acc[2 * h:2 * h + 1, sl] - lam * acc[2 * h + 1:2 * h + 2, sl]
                o = o * lax.rsqrt(jnp.mean(o * o, -1, keepdims=True) + EPS) * ng_ref[:, sl]
                o_ref[0, :, sl] = o * (1.0 - lam_init)
        else:
            for h in range(rows):
                sl = slice(h * dv, (h + 1) * dv)
                o_ref[0, :, sl] = acc[h:h + 1, sl]


def paged_attention(mode, q, k_new, v_new, cache_k, cache_v, slot, page_table, extra, lam_init=0.0):
    bs, n_pages = page_table.shape
    w = q.shape[-1]
    rows = 16
    dk = w // rows
    dv = 2 * dk if mode == "diff" else dk
    row_spec = pl.BlockSpec((1, 1, w), lambda b, p, pt: (b, 0, 0))
    page_spec = pl.BlockSpec((None, None, PAGE_SIZE, w), lambda b, p, pt: (slot, pt[b, p], 0, 0))
    if mode == "diff":
        lam, norm_g = extra
        extra_specs = [pl.BlockSpec(lam.shape, lambda b, p, pt: (0, 0)),
                       pl.BlockSpec((1, w), lambda b, p, pt: (0, 0))]
        extra_args = [lam, norm_g.reshape(1, w)]
    else:
        bias, bias_new = extra
        extra_specs = [pl.BlockSpec((1, 1, 1, PAGE_SIZE), lambda b, p, pt: (b, p, 0, 0)),
                       pl.BlockSpec((1, 1, LANES), lambda b, p, pt: (b, 0, 0))]
        extra_args = [bias, bias_new]
    grid_spec = pltpu.PrefetchScalarGridSpec(
        num_scalar_prefetch=1,
        grid=(bs, n_pages),
        in_specs=[row_spec, row_spec, row_spec, page_spec, page_spec] + extra_specs,
        out_specs=pl.BlockSpec((1, 1, w), lambda b, p, pt: (b, 0, 0)),
        scratch_shapes=[pltpu.VMEM((rows, w), F32), pltpu.VMEM((rows, 1), F32),
                        pltpu.VMEM((rows, 1), F32), pltpu.VMEM((rows, w), F32)],
    )
    return pl.pallas_call(
        functools.partial(_paged_kernel, mode=mode, rows=rows, dk=dk, dv=dv, n_pages=n_pages,
                          lam_init=lam_init),
        grid_spec=grid_spec,
        out_shape=jax.ShapeDtypeStruct((bs, 1, w), F32),
        compiler_params=_cparams(("arbitrary", "arbitrary")),
        name="paged_" + mode,
    )(page_table, q, k_new, v_new, cache_k, cache_v, *extra_args)


def _sortable_key(s):
    bits = lax.bitcast_convert_type(s, I32)
    return bits ^ ((bits >> 31) & 0x7FFFFFFF)


def _bisect_threshold(count_ge, shape, n_top):
    def step(it, t_u):
        bit = 31 - it
        cand_u = t_u | jnp.left_shift(jnp.int32(1), bit)
        cnt = count_ge(cand_u ^ INT_MIN)
        return jnp.where(cnt >= n_top, cand_u, t_u)

    t_u = lax.fori_loop(0, 32, step, jnp.zeros(shape, I32))
    return jnp.maximum(t_u ^ INT_MIN, INT_MIN + 1)


def _dsa_prompt_kernel(qi_ref, qil_ref, wt_ref, ki_ref, kil_ref, q_ref, k_ref, v_ref, o_ref,
                       key_scr, bias_scr, m_scr, l_scr, acc_scr, *, tq, dh, n_top, wcol):
    i = pl.program_id(1)
    h = pl.program_id(2)
    scale = dh ** -0.5
    idx_scale = IDX_DIM ** -0.5 * IDX_HEADS ** -0.5

    @pl.when(h == 0)
    def _():
        w = wt_ref[...] * idx_scale
        row = lax.broadcasted_iota(I32, (tq, tq), 0)
        col = lax.broadcasted_iota(I32, (tq, tq), 1)

        def score_chunk(j, masked):
            start = pl.multiple_of(j * tq, tq)
            kc = ki_ref[pl.ds(start, tq), :][:, :IDX_DIM]
            kcl = kil_ref[pl.ds(start, tq), :][:, :IDX_DIM]
            acc = jnp.zeros((tq, tq), F32)
            for hh in range(IDX_HEADS):
                qh = qi_ref[:, hh * IDX_DIM:(hh + 1) * IDX_DIM]
                s = _dot_nt(qh, kc) + (_dot_nt(qh, kcl) + _dot_nt(qil_ref[:, hh * IDX_DIM:(hh + 1) * IDX_DIM], kc))
                acc = acc + jnp.maximum(s, 0.0) * w[:, wcol + hh:wcol + hh + 1]
            key = _sortable_key(acc)
            if masked:
                key = jnp.where(col <= row, key, INT_MIN)
            key_scr[:, pl.ds(start, tq)] = key

        def score_body(j, carry):
            score_chunk(j, False)
            return carry

        lax.fori_loop(0, i, score_body, 0)
        score_chunk(i, True)

        def count_ge(cand):
            def cbody(j, part):
                start = pl.multiple_of(j * tq, tq)
                kk = key_scr[:, pl.ds(start, tq)]
                hit = jnp.where(kk >= cand, 1.0, 0.0)
                for c in range(tq // LANES):
                    part = part + hit[:, c * LANES:(c + 1) * LANES]
                return part

            part = lax.fori_loop(0, i + 1, cbody, jnp.zeros((tq, LANES), F32))
            return jnp.sum(part, axis=1, keepdims=True)

        thr = _bisect_threshold(count_ge, (tq, 1), n_top)

        def bias_body(j, carry):
            start = pl.multiple_of(j * tq, tq)
            kk = key_scr[:, pl.ds(start, tq)]
            bias_scr[:, pl.ds(start, tq)] = jnp.where(kk >= thr, 0.0, NEG)
            return carry

        lax.fori_loop(0, i + 1, bias_body, 0)

    m_scr[...] = jnp.full_like(m_scr, NEG)
    l_scr[...] = jnp.zeros_like(l_scr)
    acc_scr[...] = jnp.zeros_like(acc_scr)
    qh = q_ref[...]

    def att_body(j, carry):
        start = pl.multiple_of(j * tq, tq)
        s = _dot_nt(qh, k_ref[pl.ds(start, tq), :]) * scale + bias_scr[:, pl.ds(start, tq)]
        m_prev = m_scr[...]
        m_new = jnp.maximum(m_prev, jnp.max(s, axis=1, keepdims=True))
        alpha = jnp.exp(m_prev - m_new)
        p = jnp.exp(s - m_new)
        l_scr[...] = alpha * l_scr[...] + jnp.sum(p, axis=1, keepdims=True)
        acc_scr[...] = alpha * acc_scr[...] + _dot(p.astype(BF16), v_ref[pl.ds(start, tq), :])
        m_scr[...] = m_new
        return carry

    lax.fori_loop(0, i + 1, att_body, 0)
    o_ref[...] = (acc_scr[...] / l_scr[...]).astype(o_ref.dtype)


def dsa_prompt(qib, qil, tail, tailb, taill, qb, kb, vb, bn, t, wcol):
    heads = C_HEADS
    dh = qb.shape[1] // heads
    tq = min(256, t)
    nq = t // tq
    n_top = min(TOPK_MAX, t // 4)
    return pl.pallas_call(
        functools.partial(_dsa_prompt_kernel, tq=tq, dh=dh, n_top=n_top, wcol=wcol),
        grid=(bn, nq, heads),
        in_specs=[pl.BlockSpec((tq, IDX_HEADS * IDX_DIM), lambda b, i, h: (b * nq + i, 0)),
                  pl.BlockSpec((tq, IDX_HEADS * IDX_DIM), lambda b, i, h: (b * nq + i, 0)),
                  pl.BlockSpec((tq, LANES), lambda b, i, h: (b * nq + i, 0)),
                  pl.BlockSpec((t, LANES), lambda b, i, h: (b, 0)),
                  pl.BlockSpec((t, LANES), lambda b, i, h: (b, 0)),
                  pl.BlockSpec((tq, dh), lambda b, i, h: (b * nq + i, h)),
                  pl.BlockSpec((t, dh), lambda b, i, h: (b, h)),
                  pl.BlockSpec((t, dh), lambda b, i, h: (b, h))],
        out_specs=pl.BlockSpec((tq, dh), lambda b, i, h: (b * nq + i, h)),
        out_shape=jax.ShapeDtypeStruct((bn * t, heads * dh), BF16),
        scratch_shapes=[pltpu.VMEM((tq, t), I32), pltpu.VMEM((tq, t), F32),
                        pltpu.VMEM((tq, 1), F32), pltpu.VMEM((tq, 1), F32),
                        pltpu.VMEM((tq, dh), F32)],
        compiler_params=_cparams(("arbitrary", "arbitrary", "arbitrary")),
        name="dsa_prompt",
    )(qib, qil, tail, tailb, taill, qb, kb, vb)


def _dsa_index_kernel(pt_ref, qi_ref, w_ref, kin_ref, kp_ref, bias_ref, biasn_ref, sc_scr,
                      *, n_pages, n_top):
    p = pl.program_id(1)
    idx_scale = IDX_DIM ** -0.5 * IDX_HEADS ** -0.5
    qi = qi_ref[0]
    w = w_ref[0] * idx_scale
    s = _dot3(qi, kp_ref[...], _dot_nt)
    sc_scr[pl.ds(p, 1), :] = jnp.sum(jnp.maximum(s, 0.0) * w, axis=0, keepdims=True)

    @pl.when(p == n_pages - 1)
    def _():
        s_n = jnp.sum(qi * kin_ref[0], axis=1, keepdims=True)
        sc_new = jnp.sum(jnp.maximum(s_n, 0.0) * w, axis=0, keepdims=True)
        key = _sortable_key(sc_scr[...])
        key_new = _sortable_key(sc_new)

        def count_ge(cand):
            hit = jnp.where(key >= cand, 1.0, 0.0)
            cnt = jnp.sum(jnp.sum(hit, axis=1, keepdims=True), axis=0, keepdims=True)
            return cnt + jnp.where(key_new >= cand, 1.0, 0.0)

        thr = _bisect_threshold(count_ge, (1, 1), n_top)
        bias_ref[0] = jnp.where(key >= thr, 0.0, NEG)
        biasn_ref[0] = jnp.broadcast_to(jnp.where(key_new >= thr, 0.0, NEG), (1, LANES))


def dsa_sample_index(qi, w, ki_new, cache_ki, slot, page_table):
    bs, n_pages = page_table.shape
    n_top = min(TOPK_MAX, (n_pages * PAGE_SIZE + 1) // 4)
    grid_spec = pltpu.PrefetchScalarGridSpec(
        num_scalar_prefetch=1,
        grid=(bs, n_pages),
        in_specs=[pl.BlockSpec((1, IDX_HEADS, IDX_DIM), lambda b, p, pt: (b, 0, 0)),
                  pl.BlockSpec((1, IDX_HEADS, 1), lambda b, p, pt: (b, 0, 0)),
                  pl.BlockSpec((1, 1, IDX_DIM), lambda b, p, pt: (b, 0, 0)),
                  pl.BlockSpec((None, None, PAGE_SIZE, IDX_DIM), lambda b, p, pt: (slot, pt[b, p], 0, 0))],
        out_specs=[pl.BlockSpec((1, n_pages, PAGE_SIZE), lambda b, p, pt: (b, 0, 0)),
                   pl.BlockSpec((1, 1, LANES), lambda b, p, pt: (b, 0, 0))],
        scratch_shapes=[pltpu.VMEM((n_pages, PAGE_SIZE), F32)],
    )
    bias, bias_new = pl.pallas_call(
        functools.partial(_dsa_index_kernel, n_pages=n_pages, n_top=n_top),
        grid_spec=grid_spec,
        out_shape=[jax.ShapeDtypeStruct((bs, n_pages, PAGE_SIZE), F32),
                   jax.ShapeDtypeStruct((bs, 1, LANES), F32)],
        compiler_params=_cparams(("arbitrary", "arbitrary")),
        name="dsa_index",
    )(page_table, qi, w, ki_new, cache_ki)
    return bias.reshape(bs, n_pages, 1, PAGE_SIZE), bias_new


def _rope_tables(pos, half):
    inv = ROPE_THETA ** (-jnp.arange(half, dtype=F32) / half)
    ang = pos.astype(F32)[:, None] * inv[None, :]
    return jnp.cos(ang), jnp.sin(ang)


def _tables_128(pos):
    c, s = _rope_tables(pos, 64)
    return jnp.concatenate([c, c], -1), jnp.concatenate([-s, s], -1)


def _tables_64(pos, tail):
    c, s = _rope_tables(pos, 32)
    if tail:
        one, zero = jnp.ones_like(c), jnp.zeros_like(s)
        return jnp.concatenate([c, c, one, one], -1), jnp.concatenate([-s, s, zero, zero], -1)
    return jnp.concatenate([c, c, c, c], -1), jnp.concatenate([-s, s, -s, s], -1)


def _pad_cols(w, n):
    return jnp.pad(w, ((0, 0), (0, 0), (0, n - w.shape[-1])))


def kernel(x_prompt, x_sample, c_prompt, c_sample, state_a_C, state_a_n, state_a_m, cache_b_k,
           cache_b_v, cache_c_k, cache_c_v, cache_c_kidx, page_table, ada_w, ada_b, norm_g, ffn_wi,
           ffn_wo, a_w_in, a_b_if, a_norm_g, a_w_out, b_w_in, b_lambda, b_norm_g, b_w_out, c_w_in,
           c_w_out, final_g):
    bp, t_p, d = x_prompt.shape
    bs = x_sample.shape[0]
    depth = ada_w.shape[0]
    n_pages = page_table.shape[1]
    past = n_pages * PAGE_SIZE
    n_pool = cache_b_k.shape[1]
    rs = SUBLANES_BF16 * ((bs + SUBLANES_BF16 - 1) // SUBLANES_BF16)
    mp = bp * t_p

    ff = ffn_wo.shape[2]
    fp = FFN_TF * ((ff + FFN_TF - 1) // FFN_TF)
    wi_p = jnp.pad(ffn_wi.reshape(depth, 2, d, 2, ff), ((0, 0),) * 4 + ((0, fp - ff),))
    wi_p = wi_p.reshape(depth, 2, d, 2 * fp).astype(BF16)
    wo_p = jnp.pad(ffn_wo, ((0, 0), (0, 0), (0, fp - ff), (0, 0))).astype(BF16)

    c_rows = jnp.zeros((rs, d), F32).at[:bp].set(c_prompt).at[bp:bp + bs].set(c_sample)
    mod = ada_all(c_rows, ada_w, ada_b).reshape(depth, rs, N_ADA, d)

    xp = x_prompt.reshape(mp, d)
    xs = jnp.zeros((rs, d), F32).at[:bs].set(x_sample.reshape(bs, d))

    pos_p = jnp.arange(t_p, dtype=jnp.int32)
    pos_s = jnp.full((rs,), past, jnp.int32)
    tab128_p, tab128_s = _tables_128(pos_p), _tables_128(pos_s)
    tab64_p, tab64_s = _tables_64(pos_p, False), _tables_64(pos_s, False)
    tabt_p, tabt_s = _tables_64(pos_p, True), _tables_64(pos_s, True)

    a_main = 2 * A_HEADS * (a_norm_g.shape[-1] // 2) + 2 * A_HEADS * a_norm_g.shape[-1]
    a_tail_w = _pad_cols(a_w_in[:, :, a_main:], LANES)
    c_main = 3 * d + IDX_HEADS * IDX_DIM
    c_tail_w = _pad_cols(c_w_in[:, :, c_main:], LANES)

    kind_of = [i % 3 for i in range(depth)]
    outs = {k: [] for k in ("aCp", "anp", "amp", "aCs", "ans", "ams", "bkp", "bvp", "bks", "bvs",
                            "ckp", "cvp", "cip", "cks", "cvs", "cis")}
    for l in range(depth):
        kind = kind_of[l]
        j = kind_of[:l].count(kind)
        mp_l = [mod[l, :bp, k][:, None, :] for k in range(N_ADA)]
        ms_l = [mod[l, bp:bp + bs, k] for k in range(N_ADA)]
        ms_l = [jnp.zeros((1, rs, d), F32).at[0, :bs].set(m) for m in ms_l]

        xp = ffn_sublayer(xp, mp_l[0], mp_l[1], mp_l[2], norm_g[l, 0], wi_p, wo_p, l, 0, t_p)
        xs = ffn_sublayer(xs, ms_l[0], ms_l[1], ms_l[2], norm_g[l, 0], wi_p, wo_p, l, 0, rs)
        pj_p = functools.partial(proj, xp, mp_l[3], mp_l[4], norm_g[l, 1], rows_per_batch=t_p)
        pj_s = functools.partial(proj, xs, ms_l[3], ms_l[4], norm_g[l, 1], rows_per_batch=rs)

        if kind == 0:
            (y_p,) = pj_p(a_w_in, j, 0, a_main, dtypes=(F32,))
            (g_p,) = pj_p(a_tail_w, j, 0, LANES, dtypes=(F32,), precise=True)
            hh_p, c_p, n_p, m_p = mlstm_prompt(y_p, g_p, a_b_if[j], a_norm_g[j], bp, t_p)
            (y_s,) = pj_s(a_w_in, j, 0, a_main, dtypes=(F32,))
            (g_s,) = pj_s(a_tail_w, j, 0, LANES, dtypes=(F32,), precise=True)
            hh_s, c_s, n_s, m_s = mlstm_sample(y_s, g_s, a_b_if[j], a_norm_g[j],
                                               state_a_C[j], state_a_n[j], state_a_m[j], bs)
            op_in = hh_p
            os_in = jnp.zeros((rs, hh_s.shape[1]), BF16).at[:bs].set(hh_s.astype(BF16))
            w_out = a_w_out
            outs["aCp"].append(c_p); outs["anp"].append(n_p); outs["amp"].append(m_p)
            outs["aCs"].append(c_s); outs["ans"].append(n_s); outs["ams"].append(m_s)
        elif kind == 1:
            lam_init = 0.8 - 0.6 * math.exp(-0.3 * l)
            bqk = b_w_in.shape[2] // 3
            (q_pb,) = pj_p(b_w_in, j, 0, bqk, dtypes=(BF16,), rope=128, tables=tab128_p)
            k_p, k_pb = pj_p(b_w_in, j, bqk, bqk, dtypes=(F32, BF16), rope=128, tables=tab128_p)
            v_p, v_pb = pj_p(b_w_in, j, 2 * bqk, bqk, dtypes=(F32, BF16))
            op_in = diff_prompt(q_pb, k_pb, v_pb, b_lambda[j], b_norm_g[j], lam_init, bp, t_p)
            (q_s,) = pj_s(b_w_in, j, 0, bqk, dtypes=(F32,), rope=128, tables=tab128_s)
            (k_s,) = pj_s(b_w_in, j, bqk, bqk, dtypes=(F32,), rope=128, tables=tab128_s)
            (v_s,) = pj_s(b_w_in, j, 2 * bqk, bqk, dtypes=(F32,))
            o_s = paged_attention(
                "diff", q_s[:bs, None], k_s[:bs, None], v_s[:bs, None],
                cache_b_k.reshape(cache_b_k.shape[0], n_pool, PAGE_SIZE, bqk),
                cache_b_v.reshape(cache_b_v.shape[0], n_pool, PAGE_SIZE, bqk),
                j, page_table, (b_lambda[j], b_norm_g[j]), lam_init)
            os_in = jnp.zeros((rs, bqk), BF16).at[:bs].set(o_s[:, 0].astype(BF16))
            w_out = b_w_out
            outs["bkp"].append(k_p.reshape(bp, t_p, B_HEADS, 2, -1))
            outs["bvp"].append(v_p.reshape(bp, t_p, B_HEADS, -1))
            outs["bks"].append(k_s[:bs].reshape(bs, 1, B_HEADS, 2, -1))
            outs["bvs"].append(v_s[:bs].reshape(bs, 1, B_HEADS, -1))
        else:
            cw = d
            ni = IDX_HEADS * IDX_DIM
            (q_pb,) = pj_p(c_w_in, j, 0, cw, dtypes=(BF16,), rope=128, tables=tab128_p)
            k_p, k_pb = pj_p(c_w_in, j, cw, cw, dtypes=(F32, BF16), rope=128, tables=tab128_p)
            v_p, v_pb = pj_p(c_w_in, j, 2 * cw, cw, dtypes=(F32, BF16))
            qi_pb, qi_pl = pj_p(c_w_in, j, 3 * cw, ni, dtypes=(BF16, LO), rope=64, tables=tab64_p)
            t_pf, t_pb, t_pl = pj_p(c_tail_w, j, 0, LANES, dtypes=(F32, BF16, LO), rope=64, tables=tabt_p)
            op_in = dsa_prompt(qi_pb, qi_pl, t_pf, t_pb, t_pl, q_pb, k_pb, v_pb, bp, t_p, IDX_DIM)
            (q_s,) = pj_s(c_w_in, j, 0, cw, dtypes=(F32,), rope=128, tables=tab128_s)
            (k_s,) = pj_s(c_w_in, j, cw, cw, dtypes=(F32,), rope=128, tables=tab128_s)
            (v_s,) = pj_s(c_w_in, j, 2 * cw, cw, dtypes=(F32,))
            (qi_s,) = pj_s(c_w_in, j, 3 * cw, ni, dtypes=(F32,), rope=64, tables=tab64_s)
            (t_s,) = pj_s(c_tail_w, j, 0, LANES, dtypes=(F32,), rope=64, tables=tabt_s)
            ki_s = t_s[:bs, :IDX_DIM]
            bias, bias_new = dsa_sample_index(
                qi_s[:bs].reshape(bs, IDX_HEADS, IDX_DIM),
                t_s[:bs, IDX_DIM:IDX_DIM + IDX_HEADS].reshape(bs, IDX_HEADS, 1),
                ki_s[:, None], cache_c_kidx, j, page_table)
            o_s = paged_attention(
                "dsa", q_s[:bs, None], k_s[:bs, None], v_s[:bs, None],
                cache_c_k.reshape(cache_c_k.shape[0], n_pool, PAGE_SIZE, cw),
                cache_c_v.reshape(cache_c_v.shape[0], n_pool, PAGE_SIZE, cw),
                j, page_table, (bias, bias_new))
            os_in = jnp.zeros((rs, cw), BF16).at[:bs].set(o_s[:, 0].astype(BF16))
            w_out = c_w_out
            outs["ckp"].append(k_p.reshape(bp, t_p, C_HEADS, -1))
            outs["cvp"].append(v_p.reshape(bp, t_p, C_HEADS, -1))
            outs["cip"].append(t_pf[:, :IDX_DIM].reshape(bp, t_p, IDX_DIM))
            outs["cks"].append(k_s[:bs].reshape(bs, 1, C_HEADS, -1))
            outs["cvs"].append(v_s[:bs].reshape(bs, 1, C_HEADS, -1))
            outs["cis"].append(ki_s.reshape(bs, 1, IDX_DIM))

        xp = outproj(op_in, xp, mp_l[5], w_out, j, t_p)
        xs = outproj(os_in, xs, ms_l[5], w_out, j, rs)
        fin = final_g if l == depth - 1 else None
        xp = ffn_sublayer(xp, mp_l[6], mp_l[7], mp_l[8], norm_g[l, 2], wi_p, wo_p, l, 1, t_p, fin)
        xs = ffn_sublayer(xs, ms_l[6], ms_l[7], ms_l[8], norm_g[l, 2], wi_p, wo_p, l, 1, rs, fin)

    st = lambda k: jnp.stack(outs[k])
    return (xp.reshape(bp, t_p, d), xs[:bs].reshape(bs, 1, d),
            st("aCp"), st("anp"), st("amp"), st("aCs"), st("ans"), st("ams"),
            st("bkp"), st("bvp"), st("bks"), st("bvs"),
            st("ckp"), st("cvp"), st("cip"), st("cks"), st("cvs"), st("cis"))
```

```python
import functools
import math

import jax
import jax.numpy as jnp
from jax import lax
from jax.experimental import pallas as pl
from jax.experimental.pallas import tpu as pltpu

F32 = jnp.float32
BF16 = jnp.bfloat16
I32 = jnp.int32

EPS = 1e-6
ROPE_THETA = 10000.0
N_ADA = 9
TOPK_MAX = 256
PAGE_SIZE = 128
A_HEADS = 8
A_CHUNK = 128
B_HEADS = 8
C_HEADS = 16
IDX_HEADS = 16
IDX_DIM = 64
IDX_PACK = 4 * IDX_DIM

LANES = 128
SUBLANES_BF16 = 16
VMEM_LIMIT = 56 * 1024 * 1024
BISECT_ROWS = 64
NEG = -1e30
INT_MIN = -2 ** 31


def _cparams(sem):
    return pltpu.CompilerParams(dimension_semantics=sem, vmem_limit_bytes=VMEM_LIMIT)


def _dot(a, b):
    return jnp.dot(a, b, preferred_element_type=F32)


def _dot_nt(a, b):
    return lax.dot_general(a, b, (((1,), (1,)), ((), ())), preferred_element_type=F32)


def _split(x):
    hi = x.astype(BF16)
    return hi, (x - hi.astype(F32)).astype(BF16)


def _dot3(a, b, dot=_dot):
    a_hi, a_lo = _split(a)
    b_hi, b_lo = _split(b)
    return dot(a_hi, b_hi) + (dot(a_hi, b_lo) + dot(a_lo, b_hi))


def _norm_mod(x, g, scale, shift):
    y = x * lax.rsqrt(jnp.mean(x * x, -1, keepdims=True) + EPS) * g
    return y * (1.0 + scale) + shift


def _log_sigmoid(x):
    return jnp.minimum(x, 0.0) - jnp.log1p(jnp.exp(-jnp.abs(x)))


def _ada_kernel(c_ref, w_ref, b_ref, o_ref):
    c = c_ref[...]
    a = (c * jax.nn.sigmoid(c)).astype(BF16)
    o_ref[0] = _dot(a, w_ref[0].astype(BF16)) + b_ref[0]


def ada_all(c_rows, ada_w, ada_b):
    depth, d, n = ada_w.shape
    r = c_rows.shape[0]
    tn = 1024
    return pl.pallas_call(
        _ada_kernel,
        grid=(depth, n // tn),
        in_specs=[pl.BlockSpec((r, d), lambda l, j: (0, 0)),
                  pl.BlockSpec((1, d, tn), lambda l, j: (l, 0, j)),
                  pl.BlockSpec((1, 1, tn), lambda l, j: (l, 0, j))],
        out_specs=pl.BlockSpec((1, r, tn), lambda l, j: (l, 0, j)),
        out_shape=jax.ShapeDtypeStruct((depth, r, n), F32),
        compiler_params=_cparams(("arbitrary", "arbitrary")),
        name="ada_mod",
    )(c_rows, ada_w, ada_b.reshape(depth, 1, n))


def _row_tile(want, rows_per_batch):
    return min(want, rows_per_batch)


def _mod_spec(mod, tm, tn, rows_per_batch, col_of):
    r = mod.shape[1]
    bpb = max(rows_per_batch // tm, 1)
    return pl.BlockSpec((1, r, tn), lambda i, j: (i // bpb, 0, col_of(j)))


def _ffn_kernel(*refs, nf, final):
    if final:
        (x_ref, sh_ref, sc_ref, gt_ref, g_ref, wg_ref, wu_ref, wo_ref, fg_ref,
         o_ref, h_scr, acc_scr) = refs
    else:
        (x_ref, sh_ref, sc_ref, gt_ref, g_ref, wg_ref, wu_ref, wo_ref,
         o_ref, h_scr, acc_scr) = refs
    f = pl.program_id(1)

    @pl.when(f == 0)
    def _():
        h = _norm_mod(x_ref[...], g_ref[...], sc_ref[0], sh_ref[0])
        h_scr[...] = h.astype(BF16)
        acc_scr[...] = jnp.zeros_like(acc_scr)

    h = h_scr[...]
    a = _dot(h, wg_ref[...])
    u = _dot(h, wu_ref[...])
    act = (a * jax.nn.sigmoid(a) * u).astype(BF16)
    acc_scr[...] += _dot(act, wo_ref[...])

    @pl.when(f == nf - 1)
    def _():
        y = x_ref[...] + 0.5 * gt_ref[0] * acc_scr[...]
        if final:
            y = y * lax.rsqrt(jnp.mean(y * y, -1, keepdims=True) + EPS) * fg_ref[...]
        o_ref[...] = y


FFN_TF = 512


def ffn_sublayer(x, shift, scale, gate, g, wi_p, wo_p, l, s, rows_per_batch, final_g=None):
    m, d = x.shape
    fp = wo_p.shape[2]
    tf = FFN_TF
    nf = fp // tf
    tm = _row_tile(512, rows_per_batch)
    final = final_g is not None
    in_specs = [
        pl.BlockSpec((tm, d), lambda i, f: (i, 0)),
        _mod_spec(shift, tm, d, rows_per_batch, lambda f: 0),
        _mod_spec(scale, tm, d, rows_per_batch, lambda f: 0),
        _mod_spec(gate, tm, d, rows_per_batch, lambda f: 0),
        pl.BlockSpec((1, d), lambda i, f: (0, 0)),
        pl.BlockSpec((None, None, d, tf), lambda i, f: (l, s, 0, f)),
        pl.BlockSpec((None, None, d, tf), lambda i, f: (l, s, 0, nf + f)),
        pl.BlockSpec((None, None, tf, d), lambda i, f: (l, s, f, 0)),
    ]
    args = [x, shift, scale, gate, g.reshape(1, d), wi_p, wi_p, wo_p]
    if final:
        in_specs.append(pl.BlockSpec((1, d), lambda i, f: (0, 0)))
        args.append(final_g.reshape(1, d))
    return pl.pallas_call(
        functools.partial(_ffn_kernel, nf=nf, final=final),
        grid=(m // tm, nf),
        in_specs=in_specs,
        out_specs=pl.BlockSpec((tm, d), lambda i, f: (i, 0)),
        out_shape=jax.ShapeDtypeStruct((m, d), F32),
        scratch_shapes=[pltpu.VMEM((tm, d), BF16), pltpu.VMEM((tm, d), F32)],
        compiler_params=_cparams(("arbitrary", "arbitrary")),
        name="ffn",
    )(*args)


def _rope_tile(y, cos, sin, rope):
    if rope == 128:
        r = pltpu.roll(y, 64, 1)
    else:
        lane = lax.broadcasted_iota(I32, y.shape, 1)
        r = jnp.where((lane % 64) < 32, pltpu.roll(y, 96, 1), pltpu.roll(y, 32, 1))
    return y * cos + r * sin


LO = "bf16 residual"
Q3 = "packed indexer query"
K3 = "packed indexer key"
_WIDTH = {Q3: 4, K3: 2}


def _emit(y, kind):
    if kind in (F32, BF16):
        return [y.astype(kind)]
    hi_b, lo_b = _split(y)
    if kind == LO:
        return [lo_b]
    hi, lo = hi_b.astype(F32), lo_b.astype(F32)
    first = lax.broadcasted_iota(I32, y.shape, 1) < 64
    if kind == K3:
        tiles = [jnp.where(first, hi, pltpu.roll(hi, 64, 1)), jnp.where(first, lo, 0.0)]
    else:
        hi_r = pltpu.roll(hi, 64, 1)
        tiles = [jnp.where(first, hi, pltpu.roll(lo, 64, 1)), jnp.where(first, hi, 0.0),
                 jnp.where(first, hi_r, lo), jnp.where(first, hi_r, 0.0)]
    return [t.astype(BF16) for t in tiles]


def _proj_kernel(*refs, rope, kinds, tn, precise):
    n_out = len(kinds)
    x_ref, sh_ref, sc_ref, g_ref, w_ref = refs[:5]
    pos = 5
    if rope:
        cos_ref, sin_ref = refs[5:7]
        pos = 7
    out_refs = refs[pos:pos + n_out]
    h_scr = refs[pos + n_out]
    j = pl.program_id(1)

    @pl.when(j == 0)
    def _():
        h_scr[...] = _norm_mod(x_ref[...], g_ref[...], sc_ref[0], sh_ref[0]).astype(h_scr.dtype)

    if precise:
        y = _dot3(h_scr[...], w_ref[...])
    else:
        y = _dot(h_scr[...], w_ref[...].astype(BF16))
    if rope:
        cos = cos_ref[...]
        sin = sin_ref[...]
    for c in range(tn // LANES):
        yc = y[:, c * LANES:(c + 1) * LANES]
        if rope:
            yc = _rope_tile(yc, cos, sin, rope)
        for o_ref, kind in zip(out_refs, kinds):
            tiles = _emit(yc, kind)
            for k, tile in enumerate(tiles):
                at = (c * len(tiles) + k) * LANES
                o_ref[:, at:at + LANES] = tile


def proj(x, shift, scale, g, w, slot, col_off, n_cols, rows_per_batch, dtypes,
         rope=0, tables=None, precise=False):
    m, d = x.shape
    tn = min(512, n_cols)
    tm = _row_tile(1024, rows_per_batch)
    off = col_off // tn
    assert col_off % tn == 0 and n_cols % tn == 0
    in_specs = [
        pl.BlockSpec((tm, d), lambda i, j: (i, 0)),
        _mod_spec(shift, tm, d, rows_per_batch, lambda j: 0),
        _mod_spec(scale, tm, d, rows_per_batch, lambda j: 0),
        pl.BlockSpec((1, d), lambda i, j: (0, 0)),
        pl.BlockSpec((None, d, tn), lambda i, j: (slot, 0, off + j)),
    ]
    args = [x, shift, scale, g.reshape(1, d), w]
    if rope:
        cos, sin = tables
        nbt = max(cos.shape[0] // tm, 1)
        in_specs += [pl.BlockSpec((tm, LANES), lambda i, j: (i % nbt, 0))] * 2
        args += [cos, sin]
    out = pl.pallas_call(
        functools.partial(_proj_kernel, rope=rope, kinds=tuple(dtypes), tn=tn, precise=precise),
        grid=(m // tm, n_cols // tn),
        in_specs=in_specs,
        out_specs=[pl.BlockSpec((tm, tn * _WIDTH.get(dt, 1)), lambda i, j: (i, j)) for dt in dtypes],
        out_shape=[jax.ShapeDtypeStruct((m, n_cols * _WIDTH.get(dt, 1)), dt if dt in (F32, BF16) else BF16)
                   for dt in dtypes],
        scratch_shapes=[pltpu.VMEM((tm, d), F32 if precise else BF16)],
        compiler_params=_cparams(("arbitrary", "arbitrary")),
        name="proj",
    )(*args)
    return out


def _outproj_kernel(a_ref, x_ref, gt_ref, w_ref, o_ref):
    y = _dot(a_ref[...], w_ref[...].astype(BF16))
    o_ref[...] = x_ref[...] + gt_ref[0] * y


def outproj(a, x, gate, w, slot, rows_per_batch):
    m, d = x.shape
    k = a.shape[1]
    tn = 512
    tm = _row_tile(1024, rows_per_batch)
    return pl.pallas_call(
        _outproj_kernel,
        grid=(m // tm, d // tn),
        in_specs=[pl.BlockSpec((tm, k), lambda i, j: (i, 0)),
                  pl.BlockSpec((tm, tn), lambda i, j: (i, j)),
                  _mod_spec(gate, tm, tn, rows_per_batch, lambda j: j),
                  pl.BlockSpec((None, k, tn), lambda i, j: (slot, 0, j))],
        out_specs=pl.BlockSpec((tm, tn), lambda i, j: (i, j)),
        out_shape=jax.ShapeDtypeStruct((m, d), F32),
        compiler_params=_cparams(("arbitrary", "arbitrary")),
        name="outproj",
    )(a, x, gate, w)


def _mlstm_kernel(q_ref, k_ref, v_ref, o_ref, gates_ref, bias_ref, ng_ref,
                  hh_ref, c_ref, n_ref, m_ref, *, heads, dk, dv, chunk):
    c_idx = pl.program_id(1)

    @pl.when(c_idx == 0)
    def _():
        c_ref[...] = jnp.zeros_like(c_ref)
        n_ref[...] = jnp.zeros_like(n_ref)
        m_ref[...] = jnp.zeros_like(m_ref)

    L = chunk
    g = gates_ref[...] + bias_ref[...]
    gt = g.T
    lf = _log_sigmoid(g)
    lft = _log_sigmoid(gt)
    row = lax.broadcasted_iota(I32, (L, L), 0)
    col = lax.broadcasted_iota(I32, (L, L), 1)
    causal = col <= row
    qscale = dk ** -0.5
    for h in range(heads):
        icol = g[:, h:h + 1]
        fcol = lf[:, heads + h:heads + h + 1]
        irow = gt[h:h + 1, :]
        frow = lft[heads + h:heads + h + 1, :]
        b_col = jnp.sum(jnp.where(causal, frow, 0.0), axis=1, keepdims=True)
        b_row = jnp.sum(jnp.where(row <= col, fcol, 0.0), axis=0, keepdims=True)
        dmat = jnp.where(causal, b_col - b_row + irow, NEG)
        m_prev = m_ref[0, h:h + 1, 0:1]
        inter = b_col + m_prev
        m_t = jnp.maximum(inter, jnp.max(dmat, axis=1, keepdims=True))
        w_intra = jnp.exp(dmat - m_t)
        w_inter = jnp.exp(inter - m_t)
        qf = q_ref[:, h * dk:(h + 1) * dk] * qscale
        kf = k_ref[:, h * dk:(h + 1) * dk]
        vf = v_ref[:, h * dv:(h + 1) * dv]
        s = _dot3(qf, kf, _dot_nt) * w_intra
        c_prev = c_ref[0, h]
        n_prev = n_ref[0, h:h + 1, :]
        num = _dot3(s, vf) + w_inter * _dot3(qf, c_prev, _dot_nt)
        den = jnp.sum(s, axis=1, keepdims=True) + w_inter * jnp.sum(qf * n_prev, axis=1, keepdims=True)
        hh = num / jnp.maximum(jnp.abs(den), jnp.exp(-m_t))
        m_new = m_t[L - 1:L, :]
        b_last = b_col[L - 1:L, :]
        w_s = jnp.exp(b_last - b_col + icol - m_new)
        decay = jnp.exp(b_last + m_prev - m_new)
        vw_t = (vf * w_s).T
        c_ref[0, h] = decay * c_prev + _dot3(vw_t, kf)
        n_ref[0, h:h + 1, :] = decay * n_prev + jnp.sum(kf * w_s, axis=0, keepdims=True)
        m_ref[0, h:h + 1, :] = jnp.broadcast_to(m_new, (1, LANES))
        hn = hh * lax.rsqrt(jnp.mean(hh * hh, -1, keepdims=True) + EPS) * ng_ref[:, h * dv:(h + 1) * dv]
        hn = hn * jax.nn.sigmoid(o_ref[:, h * dv:(h + 1) * dv])
        hh_ref[:, h * dv:(h + 1) * dv] = hn.astype(BF16)


def mlstm_prompt(y, gates, b_if, norm_g, bn, t):
    heads = A_HEADS
    dv = norm_g.shape[-1]
    dk = dv // 2
    qk = heads * dk
    vw = heads * dv
    L = A_CHUNK
    nc = t // L
    bias = jnp.zeros((1, LANES), F32).at[0, :2 * heads].set(b_if)
    row = lambda b, c: b * nc + c
    outs = pl.pallas_call(
        functools.partial(_mlstm_kernel, heads=heads, dk=dk, dv=dv, chunk=L),
        grid=(bn, nc),
        in_specs=[pl.BlockSpec((L, qk), lambda b, c: (row(b, c), 0)),
                  pl.BlockSpec((L, qk), lambda b, c: (row(b, c), 1)),
                  pl.BlockSpec((L, vw), lambda b, c: (row(b, c), 1)),
                  pl.BlockSpec((L, vw), lambda b, c: (row(b, c), 2)),
                  pl.BlockSpec((L, LANES), lambda b, c: (row(b, c), 0)),
                  pl.BlockSpec((1, LANES), lambda b, c: (0, 0)),
                  pl.BlockSpec((1, vw), lambda b, c: (0, 0))],
        out_specs=[pl.BlockSpec((L, vw), lambda b, c: (row(b, c), 0)),
                   pl.BlockSpec((1, heads, dv, dk), lambda b, c: (b, 0, 0, 0)),
                   pl.BlockSpec((1, heads, dk), lambda b, c: (b, 0, 0)),
                   pl.BlockSpec((1, heads, LANES), lambda b, c: (b, 0, 0))],
        out_shape=[jax.ShapeDtypeStruct((bn * t, vw), BF16),
                   jax.ShapeDtypeStruct((bn, heads, dv, dk), F32),
                   jax.ShapeDtypeStruct((bn, heads, dk), F32),
                   jax.ShapeDtypeStruct((bn, heads, LANES), F32)],
        compiler_params=_cparams(("arbitrary", "arbitrary")),
        name="mlstm_scan",
    )(y, y, y, y, gates, bias, norm_g.reshape(1, vw))
    hh, c_fin, n_fin, m_fin = outs
    return hh, c_fin, n_fin, m_fin[:, :, 0]


def _mlstm_step_kernel(q_ref, k_ref, v_ref, o_ref, ig_ref, fg_ref, c0_ref, n0_ref, m0_ref, ng_ref,
                       h_ref, c_ref, n_ref, m_ref, *, dk):
    cst = c0_ref[0]
    n0 = n0_ref[0]
    m0 = m0_ref[0]
    q = q_ref[0] * dk ** -0.5
    k = k_ref[0]
    v = v_ref[0]
    ig = ig_ref[0]
    fc = _log_sigmoid(fg_ref[0])
    inter = fc + m0
    m_t = jnp.maximum(inter, ig)
    w_intra = jnp.exp(ig - m_t)
    w_inter = jnp.exp(inter - m_t)
    s = jnp.sum(q * k, axis=-1, keepdims=True) * w_intra
    cq = jnp.sum(cst * q, axis=-1, keepdims=True)
    num = s * v + w_inter * cq
    den = s + w_inter * jnp.sum(n0 * q, axis=-1, keepdims=True)
    hh = num / jnp.maximum(jnp.abs(den), jnp.exp(-m_t))
    w_s = w_intra
    decay = jnp.exp(inter - m_t)
    c_ref[0] = decay * cst + w_s * (v * k)
    n_ref[0] = decay * n0 + w_s * k
    m_ref[0] = m_t
    hn = hh * lax.rsqrt(jnp.mean(hh * hh, axis=1, keepdims=True) + EPS) * ng_ref[...]
    h_ref[0] = hn * jax.nn.sigmoid(o_ref[0])


def mlstm_sample(y, gates, b_if, norm_g, c0, n0, m0, bs):
    heads = A_HEADS
    dv = norm_g.shape[-1]
    dk = dv // 2
    qk = heads * dk
    vw = heads * dv
    y = y[:bs]
    q = y[:, :qk].reshape(bs, heads, 1, dk)
    k = y[:, qk:2 * qk].reshape(bs, heads, 1, dk)
    v = y[:, 2 * qk:2 * qk + vw].reshape(bs, heads, dv, 1)
    o = y[:, 2 * qk + vw:].reshape(bs, heads, dv, 1)
    gb = gates[:bs, :2 * heads] + b_if
    ig = gb[:, :heads].reshape(bs, heads, 1, 1)
    fg = gb[:, heads:].reshape(bs, heads, 1, 1)
    spec = lambda *shape: pl.BlockSpec((1,) + shape, lambda b: (b,) + (0,) * len(shape))
    outs = pl.pallas_call(
        functools.partial(_mlstm_step_kernel, dk=dk),
        grid=(bs,),
        in_specs=[spec(heads, 1, dk), spec(heads, 1, dk), spec(heads, dv, 1), spec(heads, dv, 1),
                  spec(heads, 1, 1), spec(heads, 1, 1),
                  spec(heads, dv, dk), spec(heads, 1, dk), spec(heads, 1, 1),
                  pl.BlockSpec((heads, dv, 1), lambda b: (0, 0, 0))],
        out_specs=[spec(heads, dv, 1), spec(heads, dv, dk), spec(heads, 1, dk), spec(heads, 1, 1)],
        out_shape=[jax.ShapeDtypeStruct((bs, heads, dv, 1), F32),
                   jax.ShapeDtypeStruct((bs, heads, dv, dk), F32),
                   jax.ShapeDtypeStruct((bs, heads, 1, dk), F32),
                   jax.ShapeDtypeStruct((bs, heads, 1, 1), F32)],
        compiler_params=_cparams(("arbitrary",)),
        name="mlstm_step",
    )(q, k, v, o, ig, fg, c0, n0.reshape(bs, heads, 1, dk), m0.reshape(bs, heads, 1, 1),
      norm_g.reshape(heads, dv, 1))
    hh, c_new, n_new, m_new = outs
    return hh.reshape(bs, vw), c_new, n_new.reshape(bs, heads, dk), m_new.reshape(bs, heads)


def _diff_lambda(lam_ref, lam_init):
    l = lam_ref[...]
    a = jnp.sum(l[0:1, :] * l[1:2, :], axis=1, keepdims=True)
    b = jnp.sum(l[2:3, :] * l[3:4, :], axis=1, keepdims=True)
    return jnp.exp(a) - jnp.exp(b) + lam_init


def _flash_step(s, vs, m_scr, l_scr, acc_scr, c):
    m_prev = m_scr[c]
    m_new = jnp.maximum(m_prev, jnp.max(s, axis=1, keepdims=True))
    alpha = jnp.exp(m_prev - m_new)
    p = jnp.exp(s - m_new)
    l_scr[c] = alpha * l_scr[c] + jnp.sum(p, axis=1, keepdims=True)
    acc_scr[c] = alpha * acc_scr[c] + _dot(p.astype(BF16), vs)
    m_scr[c] = m_new


def _diff_prompt_kernel(q_ref, k_ref, v_ref, lam_ref, ng_ref, o_ref, m_scr, l_scr, acc_scr,
                        *, tq, tk, dk, lam_init):
    i = pl.program_id(2)
    scale = dk ** -0.5
    m_scr[...] = jnp.full_like(m_scr, NEG)
    l_scr[...] = jnp.zeros_like(l_scr)
    acc_scr[...] = jnp.zeros_like(acc_scr)

    def chunk(j, masked):
        start = pl.multiple_of(j * tk, tk)
        vs = v_ref[pl.ds(start, tk), :]
        for c in range(2):
            s = _dot_nt(q_ref[:, c * dk:(c + 1) * dk], k_ref[pl.ds(start, tk), c * dk:(c + 1) * dk]) * scale
            if masked:
                row = i * tq + lax.broadcasted_iota(I32, (tq, tk), 0)
                col = start + lax.broadcasted_iota(I32, (tq, tk), 1)
                s = jnp.where(col <= row, s, NEG)
            _flash_step(s, vs, m_scr, l_scr, acc_scr, c)

    def body(j, carry):
        chunk(j, False)
        return carry

    n_full = (i * tq) // tk
    lax.fori_loop(0, n_full, body, 0)
    chunk(n_full, True)
    lam = _diff_lambda(lam_ref, lam_init)
    o = acc_scr[0] / l_scr[0] - lam * (acc_scr[1] / l_scr[1])
    o = o * lax.rsqrt(jnp.mean(o * o, -1, keepdims=True) + EPS) * ng_ref[...] * (1.0 - lam_init)
    o_ref[...] = o.astype(o_ref.dtype)


def diff_prompt(qb, kb, vb, lam, norm_g, lam_init, bn, t):
    heads = B_HEADS
    dv = norm_g.shape[-1]
    dk = dv // 2
    tq = min(256, t)
    tk = min(1024, t)
    nq = t // tq
    return pl.pallas_call(
        functools.partial(_diff_prompt_kernel, tq=tq, tk=tk, dk=dk, lam_init=lam_init),
        grid=(bn, heads, nq),
        in_specs=[pl.BlockSpec((tq, 2 * dk), lambda b, h, i: (b * nq + i, h)),
                  pl.BlockSpec((t, 2 * dk), lambda b, h, i: (b, h)),
                  pl.BlockSpec((t, dv), lambda b, h, i: (b, h)),
                  pl.BlockSpec(lam.shape, lambda b, h, i: (0, 0)),
                  pl.BlockSpec((1, dv), lambda b, h, i: (0, h))],
        out_specs=pl.BlockSpec((tq, dv), lambda b, h, i: (b * nq + i, h)),
        out_shape=jax.ShapeDtypeStruct((bn * t, heads * dv), BF16),
        scratch_shapes=[pltpu.VMEM((2, tq, 1), F32), pltpu.VMEM((2, tq, 1), F32),
                        pltpu.VMEM((2, tq, dv), F32)],
        compiler_params=_cparams(("arbitrary", "arbitrary", "arbitrary")),
        name="diff_prompt",
    )(qb, kb, vb, lam, norm_g.reshape(1, heads * dv))


def _paged_kernel(pt_ref, q_ref, kn_ref, vn_ref, k_ref, v_ref, *rest, mode, group, n_pages, lam_init):
    if mode == "diff":
        lam_ref, ng_ref, o_ref, m_scr, l_scr, acc_scr = rest
    else:
        bias_ref, biasn_ref, o_ref, exp_scr, m_scr, l_scr, acc_scr = rest
    p = pl.program_id(1)
    rows, dk = q_ref.shape[1:]
    ncol = PAGE_SIZE * group
    scale = dk ** -0.5

    @pl.when(p == 0)
    def _():
        m_scr[...] = jnp.full_like(m_scr, NEG)
        l_scr[...] = jnp.zeros_like(l_scr)
        acc_scr[...] = jnp.zeros_like(acc_scr)
        if mode == "dsa":
            t = lax.broadcasted_iota(I32, (PAGE_SIZE, ncol), 0)
            c = lax.broadcasted_iota(I32, (PAGE_SIZE, ncol), 1)
            exp_scr[...] = jnp.where(c // group == t, 1.0, 0.0).astype(BF16)

    q = q_ref[0]
    qb = q.astype(BF16)
    if mode == "diff":
        half = rows // 2
        s = jnp.concatenate(
            [_dot_nt(qb[c * half:(c + 1) * half], k_ref[pl.ds(c, ncol, stride=2), :].astype(BF16))
             for c in range(2)], axis=0)
    else:
        s = _dot_nt(qb, k_ref[...].astype(BF16))
    r_i = lax.broadcasted_iota(I32, (rows, ncol), 0)
    c_i = lax.broadcasted_iota(I32, (rows, ncol), 1)
    keep = (c_i % group) == (r_i % group)
    if mode == "dsa":
        sel = jnp.where(bias_ref[0, 0] == 0.0, 1.0, 0.0).astype(BF16)
        sel = _dot(jnp.broadcast_to(sel, (8, PAGE_SIZE)), exp_scr[...])[0:1]
        keep = keep & (sel > 0.5)
    s = jnp.where(keep, s * scale, NEG)
    m_prev = m_scr[...]
    m_new = jnp.maximum(m_prev, jnp.max(s, axis=1, keepdims=True))
    alpha = jnp.exp(m_prev - m_new)
    pe = jnp.exp(s - m_new)
    l_scr[...] = alpha * l_scr[...] + jnp.sum(pe, axis=1, keepdims=True)
    acc_scr[...] = alpha * acc_scr[...] + _dot(pe.astype(BF16), v_ref[...].astype(BF16))
    m_scr[...] = m_new

    @pl.when(p == n_pages - 1)
    def _():
        s_n = jnp.sum(q * kn_ref[0], axis=1, keepdims=True) * scale
        v_n = vn_ref[0]
        if mode == "dsa":
            s_n = s_n + biasn_ref[0][:, 0:1]
        else:
            v_n = jnp.concatenate([v_n, v_n], axis=0)
        m_prev = m_scr[...]
        m_new = jnp.maximum(m_prev, s_n)
        alpha = jnp.exp(m_prev - m_new)
        pn = jnp.exp(s_n - m_new)
        l_fin = alpha * l_scr[...] + pn
        acc = (alpha * acc_scr[...] + pn * v_n) / l_fin
        if mode == "diff":
            half = rows // 2
            o = acc[:half] - _diff_lambda(lam_ref, lam_init) * acc[half:]
            o = o * lax.rsqrt(jnp.mean(o * o, -1, keepdims=True) + EPS) * ng_ref[...]
            o_ref[0] = o * (1.0 - lam_init)
        else:
            o_ref[0] = acc


def paged_attention(mode, q, k_new, v_new, cache_k, cache_v, slot, page_table, extra, lam_init=0.0):
    bs, n_pages = page_table.shape
    rows, dk = q.shape[1:]
    hv, dv = v_new.shape[1:]
    group = hv
    ncol = PAGE_SIZE * group
    seq_spec = lambda a: pl.BlockSpec((1,) + a.shape[1:], lambda b, p, pt: (b, 0, 0))
    page_spec = lambda a: pl.BlockSpec((None, None) + a.shape[2:], lambda b, p, pt: (slot, pt[b, p], 0, 0))
    scratch = [pltpu.VMEM((rows, 1), F32), pltpu.VMEM((rows, 1), F32), pltpu.VMEM((rows, dv), F32)]
    if mode == "diff":
        lam, norm_g = extra
        extra_specs = [pl.BlockSpec(lam.shape, lambda b, p, pt: (0, 0)),
                       pl.BlockSpec(norm_g.shape, lambda b, p, pt: (0, 0))]
        extra_args = [lam, norm_g]
    else:
        bias, bias_new = extra
        extra_specs = [pl.BlockSpec((1, 1, 1, PAGE_SIZE), lambda b, p, pt: (b, p, 0, 0)),
                       pl.BlockSpec((1, 1, LANES), lambda b, p, pt: (b, 0, 0))]
        extra_args = [bias, bias_new]
        scratch = [pltpu.VMEM((PAGE_SIZE, ncol), BF16)] + scratch
    grid_spec = pltpu.PrefetchScalarGridSpec(
        num_scalar_prefetch=1,
        grid=(bs, n_pages),
        in_specs=[seq_spec(q), seq_spec(k_new), seq_spec(v_new), page_spec(cache_k),
                  page_spec(cache_v)] + extra_specs,
        out_specs=pl.BlockSpec((1, hv, dv), lambda b, p, pt: (b, 0, 0)),
        scratch_shapes=scratch,
    )
    return pl.pallas_call(
        functools.partial(_paged_kernel, mode=mode, group=group, n_pages=n_pages, lam_init=lam_init),
        grid_spec=grid_spec,
        out_shape=jax.ShapeDtypeStruct((bs, hv, dv), F32),
        compiler_params=_cparams(("arbitrary", "arbitrary")),
        name="paged_" + mode,
    )(page_table, q, k_new, v_new, cache_k, cache_v, *extra_args)


def _sortable_key(s):
    bits = lax.bitcast_convert_type(s, I32)
    return bits ^ ((bits >> 31) & 0x7FFFFFFF)


def _bisect_threshold(count_ge, shape, n_top):
    def step(it, t_u):
        bit = 31 - it
        cand_u = t_u | jnp.left_shift(jnp.int32(1), bit)
        cnt = count_ge(cand_u ^ INT_MIN)
        return jnp.where(cnt >= n_top, cand_u, t_u)

    t_u = lax.fori_loop(0, 32, step, jnp.zeros(shape, I32))
    return jnp.maximum(t_u ^ INT_MIN, INT_MIN + 1)


def _dsa_prompt_kernel(qi_ref, wt_ref, ki_ref, q_ref, k_ref, v_ref, o_ref,
                       key_scr, bias_scr, wb_scr, m_scr, l_scr, acc_scr, *, tq, tk, dh, n_top, wcol):
    i = pl.program_id(1)
    h = pl.program_id(2)
    scale = dh ** -0.5
    idx_scale = IDX_DIM ** -0.5 * IDX_HEADS ** -0.5
    groups = tq // LANES
    n_att = (i * tq) // tk + 1

    @pl.when(h == 0)
    def _():
        w = wt_ref[...] * idx_scale
        for hh in range(IDX_HEADS):
            wb_scr[hh] = jnp.broadcast_to(w[:, wcol + hh:wcol + hh + 1], (tq, LANES))

        def score_chunk(j, masked):
            start = pl.multiple_of(j * tq, tq)
            kc = ki_ref[pl.ds(start, tq), :]
            acc = [jnp.zeros((tq, LANES), F32) for _ in range(groups)]
            for hh in range(IDX_HEADS):
                s = _dot_nt(qi_ref[:, hh * IDX_PACK:(hh + 1) * IDX_PACK], kc)
                wb = wb_scr[hh]
                for c in range(groups):
                    acc[c] = acc[c] + jnp.maximum(s[:, c * LANES:(c + 1) * LANES], 0.0) * wb
            for c in range(groups):
                key = _sortable_key(acc[c])
                if masked:
                    row = lax.broadcasted_iota(I32, (tq, LANES), 0)
                    col = lax.broadcasted_iota(I32, (tq, LANES), 1) + c * LANES
                    key = jnp.where(col <= row, key, INT_MIN)
                key_scr[:, pl.ds(pl.multiple_of(start + c * LANES, LANES), LANES)] = key

        def score_body(j, carry):
            score_chunk(j, False)
            return carry

        lax.fori_loop(0, i, score_body, 0)
        score_chunk(i, True)

        for rb in range(tq // BISECT_ROWS):
            rows = pl.ds(rb * BISECT_ROWS, BISECT_ROWS)

            def count_ge(cand, rows=rows):
                cand_b = jnp.broadcast_to(cand, (BISECT_ROWS, LANES))

                def cbody(j, part):
                    start = pl.multiple_of(j * tq, tq)
                    for c in range(groups):
                        kk = key_scr[rows, pl.ds(start + c * LANES, LANES)]
                        part = part + jnp.where(kk >= cand_b, 1.0, 0.0)
                    return part

                part = lax.fori_loop(0, i + 1, cbody, jnp.zeros((BISECT_ROWS, LANES), F32))
                return jnp.sum(part, axis=1, keepdims=True)

            thr = _bisect_threshold(count_ge, (BISECT_ROWS, 1), n_top)
            thr = jnp.broadcast_to(thr, (BISECT_ROWS, LANES))

            def bias_body(j, carry, rows=rows, thr=thr):
                start = pl.multiple_of(j * LANES, LANES)
                kk = key_scr[rows, pl.ds(start, LANES)]
                bias_scr[rows, pl.ds(start, LANES)] = jnp.where(kk >= thr, 0.0, NEG)
                return carry

            lax.fori_loop(0, (i + 1) * groups, bias_body, 0)

        def fill_body(j, carry):
            start = pl.multiple_of(j * LANES, LANES)
            bias_scr[:, pl.ds(start, LANES)] = jnp.full((tq, LANES), NEG, F32)
            return carry

        lax.fori_loop((i + 1) * groups, n_att * (tk // LANES), fill_body, 0)

    m_scr[...] = jnp.full_like(m_scr, NEG)
    l_scr[...] = jnp.zeros_like(l_scr)
    acc_scr[...] = jnp.zeros_like(acc_scr)
    qh = q_ref[...]

    def att_body(j, carry):
        start = pl.multiple_of(j * tk, tk)
        s = _dot_nt(qh, k_ref[pl.ds(start, tk), :]) * scale + bias_scr[:, pl.ds(start, tk)]
        _flash_step(s, v_ref[pl.ds(start, tk), :], m_scr, l_scr, acc_scr, 0)
        return carry

    lax.fori_loop(0, n_att, att_body, 0)
    o_ref[...] = (acc_scr[0] / l_scr[0]).astype(o_ref.dtype)


def dsa_prompt(qi3, tail, ki3, qb, kb, vb, bn, t, wcol):
    heads = C_HEADS
    dh = qb.shape[1] // heads
    tq = min(256, t)
    tk = min(1024, t)
    nq = t // tq
    n_top = min(TOPK_MAX, t // 4)
    return pl.pallas_call(
        functools.partial(_dsa_prompt_kernel, tq=tq, tk=tk, dh=dh, n_top=n_top, wcol=wcol),
        grid=(bn, nq, heads),
        in_specs=[pl.BlockSpec((tq, IDX_HEADS * IDX_PACK), lambda b, i, h: (b * nq + i, 0)),
                  pl.BlockSpec((tq, LANES), lambda b, i, h: (b * nq + i, 0)),
                  pl.BlockSpec((t, IDX_PACK), lambda b, i, h: (b, 0)),
                  pl.BlockSpec((tq, dh), lambda b, i, h: (b * nq + i, h)),
                  pl.BlockSpec((t, dh), lambda b, i, h: (b, h)),
                  pl.BlockSpec((t, dh), lambda b, i, h: (b, h))],
        out_specs=pl.BlockSpec((tq, dh), lambda b, i, h: (b * nq + i, h)),
        out_shape=jax.ShapeDtypeStruct((bn * t, heads * dh), BF16),
        scratch_shapes=[pltpu.VMEM((tq, t), I32), pltpu.VMEM((tq, t), F32),
                        pltpu.VMEM((IDX_HEADS, tq, LANES), F32),
                        pltpu.VMEM((1, tq, 1), F32), pltpu.VMEM((1, tq, 1), F32),
                        pltpu.VMEM((1, tq, dh), F32)],
        compiler_params=_cparams(("arbitrary", "arbitrary", "arbitrary")),
        name="dsa_prompt",
    )(qi3, tail, ki3, qb, kb, vb)


def _dsa_index_kernel(pt_ref, qi_ref, w_ref, kin_ref, kp_ref, bias_ref, biasn_ref, sc_scr,
                      *, n_pages, n_top):
    p = pl.program_id(1)
    idx_scale = IDX_DIM ** -0.5 * IDX_HEADS ** -0.5
    qi = qi_ref[0]
    w = w_ref[0] * idx_scale
    s = _dot3(qi, kp_ref[...], _dot_nt)
    sc_scr[pl.ds(p, 1), :] = jnp.sum(jnp.maximum(s, 0.0) * w, axis=0, keepdims=True)

    @pl.when(p == n_pages - 1)
    def _():
        s_n = jnp.sum(qi * kin_ref[0], axis=1, keepdims=True)
        sc_new = jnp.sum(jnp.maximum(s_n, 0.0) * w, axis=0, keepdims=True)
        key = _sortable_key(sc_scr[...])
        key_new = _sortable_key(sc_new)

        def count_ge(cand):
            hit = jnp.where(key >= cand, 1.0, 0.0)
            cnt = jnp.sum(jnp.sum(hit, axis=1, keepdims=True), axis=0, keepdims=True)
            return cnt + jnp.where(key_new >= cand, 1.0, 0.0)

        thr = _bisect_threshold(count_ge, (1, 1), n_top)
        bias_ref[0] = jnp.where(key >= thr, 0.0, NEG)
        biasn_ref[0] = jnp.broadcast_to(jnp.where(key_new >= thr, 0.0, NEG), (1, LANES))


def dsa_sample_index(qi, w, ki_new, cache_ki, slot, page_table):
    bs, n_pages = page_table.shape
    n_top = min(TOPK_MAX, (n_pages * PAGE_SIZE + 1) // 4)
    grid_spec = pltpu.PrefetchScalarGridSpec(
        num_scalar_prefetch=1,
        grid=(bs, n_pages),
        in_specs=[pl.BlockSpec((1, IDX_HEADS, IDX_DIM), lambda b, p, pt: (b, 0, 0)),
                  pl.BlockSpec((1, IDX_HEADS, 1), lambda b, p, pt: (b, 0, 0)),
                  pl.BlockSpec((1, 1, IDX_DIM), lambda b, p, pt: (b, 0, 0)),
                  pl.BlockSpec((None, None, PAGE_SIZE, IDX_DIM), lambda b, p, pt: (slot, pt[b, p], 0, 0))],
        out_specs=[pl.BlockSpec((1, n_pages, PAGE_SIZE), lambda b, p, pt: (b, 0, 0)),
                   pl.BlockSpec((1, 1, LANES), lambda b, p, pt: (b, 0, 0))],
        scratch_shapes=[pltpu.VMEM((n_pages, PAGE_SIZE), F32)],
    )
    bias, bias_new = pl.pallas_call(
        functools.partial(_dsa_index_kernel, n_pages=n_pages, n_top=n_top),
        grid_spec=grid_spec,
        out_shape=[jax.ShapeDtypeStruct((bs, n_pages, PAGE_SIZE), F32),
                   jax.ShapeDtypeStruct((bs, 1, LANES), F32)],
        compiler_params=_cparams(("arbitrary", "arbitrary")),
        name="dsa_index",
    )(page_table, qi, w, ki_new, cache_ki)
    return bias.reshape(bs, n_pages, 1, PAGE_SIZE), bias_new


def _rope_tables(pos, half):
    inv = ROPE_THETA ** (-jnp.arange(half, dtype=F32) / half)
    ang = pos.astype(F32)[:, None] * inv[None, :]
    return jnp.cos(ang), jnp.sin(ang)


def _tables_128(pos):
    c, s = _rope_tables(pos, 64)
    return jnp.concatenate([c, c], -1), jnp.concatenate([-s, s], -1)


def _tables_64(pos, tail):
    c, s = _rope_tables(pos, 32)
    if tail:
        one, zero = jnp.ones_like(c), jnp.zeros_like(s)
        return jnp.concatenate([c, c, one, one], -1), jnp.concatenate([-s, s, zero, zero], -1)
    return jnp.concatenate([c, c, c, c], -1), jnp.concatenate([-s, s, -s, s], -1)


def _pad_cols(w, n):
    return jnp.pad(w, ((0, 0), (0, 0), (0, n - w.shape[-1])))


def kernel(x_prompt, x_sample, c_prompt, c_sample, state_a_C, state_a_n, state_a_m, cache_b_k,
           cache_b_v, cache_c_k, cache_c_v, cache_c_kidx, page_table, ada_w, ada_b, norm_g, ffn_wi,
           ffn_wo, a_w_in, a_b_if, a_norm_g, a_w_out, b_w_in, b_lambda, b_norm_g, b_w_out, c_w_in,
           c_w_out, final_g):
    bp, t_p, d = x_prompt.shape
    bs = x_sample.shape[0]
    depth = ada_w.shape[0]
    n_pages = page_table.shape[1]
    past = n_pages * PAGE_SIZE
    n_pool = cache_b_k.shape[1]
    rs = SUBLANES_BF16 * ((bs + SUBLANES_BF16 - 1) // SUBLANES_BF16)
    mp = bp * t_p

    ff = ffn_wo.shape[2]
    fp = FFN_TF * ((ff + FFN_TF - 1) // FFN_TF)
    pad_cols = ((0, 0),) * 3 + ((0, fp - ff),)
    wi_p = jnp.concatenate([jnp.pad(ffn_wi[..., :ff].astype(BF16), pad_cols),
                            jnp.pad(ffn_wi[..., ff:].astype(BF16), pad_cols)], -1)
    wo_p = jnp.pad(ffn_wo, ((0, 0), (0, 0), (0, fp - ff), (0, 0))).astype(BF16)

    c_rows = jnp.zeros((rs, d), F32).at[:bp].set(c_prompt).at[bp:bp + bs].set(c_sample)
    mod = ada_all(c_rows, ada_w, ada_b).reshape(depth, rs, N_ADA, d)

    xp = x_prompt.reshape(mp, d)
    xs = jnp.zeros((rs, d), F32).at[:bs].set(x_sample.reshape(bs, d))

    pos_p = jnp.arange(t_p, dtype=jnp.int32)
    pos_s = jnp.full((rs,), past, jnp.int32)
    tab128_p, tab128_s = _tables_128(pos_p), _tables_128(pos_s)
    tab64_p, tab64_s = _tables_64(pos_p, False), _tables_64(pos_s, False)
    tabt_p, tabt_s = _tables_64(pos_p, True), _tables_64(pos_s, True)

    a_main = 2 * A_HEADS * (a_norm_g.shape[-1] // 2) + 2 * A_HEADS * a_norm_g.shape[-1]
    a_tail_w = _pad_cols(a_w_in[:, :, a_main:], LANES)
    c_main = 3 * d + IDX_HEADS * IDX_DIM
    c_tail_w = _pad_cols(c_w_in[:, :, c_main:], LANES)

    kind_of = [i % 3 for i in range(depth)]
    outs = {k: [] for k in ("aCp", "anp", "amp", "aCs", "ans", "ams", "bkp", "bvp", "bks", "bvs",
                            "ckp", "cvp", "cip", "cks", "cvs", "cis")}
    for l in range(depth):
        kind = kind_of[l]
        j = kind_of[:l].count(kind)
        mp_l = [mod[l, :bp, k][:, None, :] for k in range(N_ADA)]
        ms_l = [mod[l, bp:bp + bs, k] for k in range(N_ADA)]
        ms_l = [jnp.zeros((1, rs, d), F32).at[0, :bs].set(m) for m in ms_l]

        xp = ffn_sublayer(xp, mp_l[0], mp_l[1], mp_l[2], norm_g[l, 0], wi_p, wo_p, l, 0, t_p)
        xs = ffn_sublayer(xs, ms_l[0], ms_l[1], ms_l[2], norm_g[l, 0], wi_p, wo_p, l, 0, rs)
        pj_p = functools.partial(proj, xp, mp_l[3], mp_l[4], norm_g[l, 1], rows_per_batch=t_p)
        pj_s = functools.partial(proj, xs, ms_l[3], ms_l[4], norm_g[l, 1], rows_per_batch=rs)

        if kind == 0:
            (y_p,) = pj_p(a_w_in, j, 0, a_main, dtypes=(F32,))
            (g_p,) = pj_p(a_tail_w, j, 0, LANES, dtypes=(F32,), precise=True)
            hh_p, c_p, n_p, m_p = mlstm_prompt(y_p, g_p, a_b_if[j], a_norm_g[j], bp, t_p)
            (y_s,) = pj_s(a_w_in, j, 0, a_main, dtypes=(F32,))
            (g_s,) = pj_s(a_tail_w, j, 0, LANES, dtypes=(F32,), precise=True)
            hh_s, c_s, n_s, m_s = mlstm_sample(y_s, g_s, a_b_if[j], a_norm_g[j],
                                               state_a_C[j], state_a_n[j], state_a_m[j], bs)
            op_in = hh_p
            os_in = jnp.zeros((rs, hh_s.shape[1]), BF16).at[:bs].set(hh_s.astype(BF16))
            w_out = a_w_out
            outs["aCp"].append(c_p); outs["anp"].append(n_p); outs["amp"].append(m_p)
            outs["aCs"].append(c_s); outs["ans"].append(n_s); outs["ams"].append(m_s)
        elif kind == 1:
            lam_init = 0.8 - 0.6 * math.exp(-0.3 * l)
            bqk = b_w_in.shape[2] // 3
            (q_pb,) = pj_p(b_w_in, j, 0, bqk, dtypes=(BF16,), rope=128, tables=tab128_p)
            k_p, k_pb = pj_p(b_w_in, j, bqk, bqk, dtypes=(F32, BF16), rope=128, tables=tab128_p)
            v_p, v_pb = pj_p(b_w_in, j, 2 * bqk, bqk, dtypes=(F32, BF16))
            op_in = diff_prompt(q_pb, k_pb, v_pb, b_lambda[j], b_norm_g[j], lam_init, bp, t_p)
            (q_s,) = pj_s(b_w_in, j, 0, bqk, dtypes=(F32,), rope=128, tables=tab128_s)
            (k_s,) = pj_s(b_w_in, j, bqk, bqk, dtypes=(F32,), rope=128, tables=tab128_s)
            (v_s,) = pj_s(b_w_in, j, 2 * bqk, bqk, dtypes=(F32,))
            comp_major = lambda a: a[:bs].reshape(bs, B_HEADS, 2, -1).transpose(0, 2, 1, 3).reshape(
                bs, 2 * B_HEADS, -1)
            o_s = paged_attention(
                "diff", comp_major(q_s), comp_major(k_s), v_s[:bs].reshape(bs, B_HEADS, -1),
                cache_b_k.reshape(cache_b_k.shape[0], n_pool, PAGE_SIZE * B_HEADS * 2, -1),
                cache_b_v.reshape(cache_b_v.shape[0], n_pool, PAGE_SIZE * B_HEADS, -1),
                j, page_table, (b_lambda[j], b_norm_g[j]), lam_init)
            os_in = jnp.zeros((rs, bqk), BF16).at[:bs].set(o_s.reshape(bs, bqk).astype(BF16))
            w_out = b_w_out
            outs["bkp"].append(k_p.reshape(bp, t_p, B_HEADS, 2, -1))
            outs["bvp"].append(v_p.reshape(bp, t_p, B_HEADS, -1))
            outs["bks"].append(k_s[:bs].reshape(bs, 1, B_HEADS, 2, -1))
            outs["bvs"].append(v_s[:bs].reshape(bs, 1, B_HEADS, -1))
        else:
            cw = d
            ni = IDX_HEADS * IDX_DIM
            (q_pb,) = pj_p(c_w_in, j, 0, cw, dtypes=(BF16,), rope=128, tables=tab128_p)
            k_p, k_pb = pj_p(c_w_in, j, cw, cw, dtypes=(F32, BF16), rope=128, tables=tab128_p)
            v_p, v_pb = pj_p(c_w_in, j, 2 * cw, cw, dtypes=(F32, BF16))
            (qi_p3,) = pj_p(c_w_in, j, 3 * cw, ni, dtypes=(Q3,), rope=64, tables=tab64_p)
            t_pf, ki_p3 = pj_p(c_tail_w, j, 0, LANES, dtypes=(F32, K3), rope=64, tables=tabt_p)
            op_in = dsa_prompt(qi_p3, t_pf, ki_p3, q_pb, k_pb, v_pb, bp, t_p, IDX_DIM)
            (q_s,) = pj_s(c_w_in, j, 0, cw, dtypes=(F32,), rope=128, tables=tab128_s)
            (k_s,) = pj_s(c_w_in, j, cw, cw, dtypes=(F32,), rope=128, tables=tab128_s)
            (v_s,) = pj_s(c_w_in, j, 2 * cw, cw, dtypes=(F32,))
            (qi_s,) = pj_s(c_w_in, j, 3 * cw, ni, dtypes=(F32,), rope=64, tables=tab64_s)
            (t_s,) = pj_s(c_tail_w, j, 0, LANES, dtypes=(F32,), rope=64, tables=tabt_s)
            ki_s = t_s[:bs, :IDX_DIM]
            bias, bias_new = dsa_sample_index(
                qi_s[:bs].reshape(bs, IDX_HEADS, IDX_DIM),
                t_s[:bs, IDX_DIM:IDX_DIM + IDX_HEADS].reshape(bs, IDX_HEADS, 1),
                ki_s[:, None], cache_c_kidx, j, page_table)
            o_s = paged_attention(
                "dsa", q_s[:bs].reshape(bs, C_HEADS, -1), k_s[:bs].reshape(bs, C_HEADS, -1),
                v_s[:bs].reshape(bs, C_HEADS, -1),
                cache_c_k.reshape(cache_c_k.shape[0], n_pool, PAGE_SIZE * C_HEADS, -1),
                cache_c_v.reshape(cache_c_v.shape[0], n_pool, PAGE_SIZE * C_HEADS, -1),
                j, page_table, (bias, bias_new))
            os_in = jnp.zeros((rs, cw), BF16).at[:bs].set(o_s.reshape(bs, cw).astype(BF16))
            w_out = c_w_out
            outs["ckp"].append(k_p.reshape(bp, t_p, C_HEADS, -1))
            outs["cvp"].append(v_p.reshape(bp, t_p, C_HEADS, -1))
            outs["cip"].append(t_pf[:, :IDX_DIM].reshape(bp, t_p, IDX_DIM))
            outs["cks"].append(k_s[:bs].reshape(bs, 1, C_HEADS, -1))
            outs["cvs"].append(v_s[:bs].reshape(bs, 1, C_HEADS, -1))
            outs["cis"].append(ki_s.reshape(bs, 1, IDX_DIM))

        xp = outproj(op_in, xp, mp_l[5], w_out, j, t_p)
        xs = outproj(os_in, xs, ms_l[5], w_out, j, rs)
        fin = final_g if l == depth - 1 else None
        xp = ffn_sublayer(xp, mp_l[6], mp_l[7], mp_l[8], norm_g[l, 2], wi_p, wo_p, l, 1, t_p, fin)
        xs = ffn_sublayer(xs, ms_l[6], ms_l[7], ms_l[8], norm_g[l, 2], wi_p, wo_p, l, 1, rs, fin)

    st = lambda k: jnp.stack(outs[k])
    return (xp.reshape(bp, t_p, d), xs[:bs].reshape(bs, 1, d),
            st("aCp"), st("anp"), st("amp"), st("aCs"), st("ans"), st("ams"),
            st("bkp"), st("bvp"), st("bks"), st("bvs"),
            st("ckp"), st("cvp"), st("cip"), st("cks"), st("cvs"), st("cis"))
```

```python
import functools
import math

import jax
import jax.numpy as jnp
from jax import lax
from jax.experimental import pallas as pl
from jax.experimental.pallas import tpu as pltpu

F32 = jnp.float32
BF16 = jnp.bfloat16
I32 = jnp.int32

EPS = 1e-6
ROPE_THETA = 10000.0
N_ADA = 9
TOPK_MAX = 256
PAGE_SIZE = 128
A_HEADS = 8
A_CHUNK = 128
B_HEADS = 8
C_HEADS = 16
IDX_HEADS = 16
IDX_DIM = 64
IDX_PACK = 4 * IDX_DIM

LANES = 128
SUBLANES_BF16 = 16
VMEM_LIMIT = 56 * 1024 * 1024
BISECT_ROWS = 64
NEG = -1e30
INT_MIN = -2 ** 31


def _cparams(sem):
    return pltpu.CompilerParams(dimension_semantics=sem, vmem_limit_bytes=VMEM_LIMIT)


def _dot(a, b):
    return jnp.dot(a, b, preferred_element_type=F32)


def _dot_nt(a, b):
    return lax.dot_general(a, b, (((1,), (1,)), ((), ())), preferred_element_type=F32)


def _split(x):
    hi = x.astype(BF16)
    return hi, (x - hi.astype(F32)).astype(BF16)


def _dot3(a, b, dot=_dot):
    a_hi, a_lo = _split(a)
    b_hi, b_lo = _split(b)
    return dot(a_hi, b_hi) + (dot(a_hi, b_lo) + dot(a_lo, b_hi))


def _norm_mod(x, g, scale, shift):
    y = x * lax.rsqrt(jnp.mean(x * x, -1, keepdims=True) + EPS) * g
    return y * (1.0 + scale) + shift


def _log_sigmoid(x):
    return jnp.minimum(x, 0.0) - jnp.log1p(jnp.exp(-jnp.abs(x)))


def _ada_kernel(c_ref, w_ref, b_ref, o_ref):
    c = c_ref[...]
    a = (c * jax.nn.sigmoid(c)).astype(BF16)
    o_ref[0] = _dot(a, w_ref[0].astype(BF16)) + b_ref[0]


def ada_all(c_rows, ada_w, ada_b):
    depth, d, n = ada_w.shape
    r = c_rows.shape[0]
    tn = 1024
    return pl.pallas_call(
        _ada_kernel,
        grid=(depth, n // tn),
        in_specs=[pl.BlockSpec((r, d), lambda l, j: (0, 0)),
                  pl.BlockSpec((1, d, tn), lambda l, j: (l, 0, j)),
                  pl.BlockSpec((1, 1, tn), lambda l, j: (l, 0, j))],
        out_specs=pl.BlockSpec((1, r, tn), lambda l, j: (l, 0, j)),
        out_shape=jax.ShapeDtypeStruct((depth, r, n), F32),
        compiler_params=_cparams(("arbitrary", "arbitrary")),
        name="ada_mod",
    )(c_rows, ada_w, ada_b.reshape(depth, 1, n))


def _row_tile(want, rows_per_batch):
    return min(want, rows_per_batch)


def _mod_spec(mod, tm, tn, rows_per_batch, col_of):
    r = mod.shape[1]
    bpb = max(rows_per_batch // tm, 1)
    return pl.BlockSpec((1, r, tn), lambda i, j: (i // bpb, 0, col_of(j)))


def _stage_kernel(w_ref, o_ref, *, nv, nb):
    j = pl.program_id(1)

    @pl.when(j % nb < nv)
    def _():
        o_ref[...] = w_ref[...].astype(BF16)

    @pl.when(j % nb >= nv)
    def _():
        o_ref[...] = jnp.zeros_like(o_ref)


def stage_ffn_weights(ffn_wi, ffn_wo, fp):
    depth, _, d, ff2 = ffn_wi.shape
    ff = ff2 // 2
    assert ff % LANES == 0 and fp % LANES == 0
    nv, nb = ff // LANES, fp // LANES
    ls = depth * 2
    src = lambda j: jnp.minimum(j % nb, nv - 1) + (j // nb) * nv
    kern = functools.partial(_stage_kernel, nv=nv, nb=nb)
    wi_p = pl.pallas_call(
        kern,
        grid=(ls, 2 * nb),
        in_specs=[pl.BlockSpec((None, d, LANES), lambda a, j: (a, 0, src(j)))],
        out_specs=pl.BlockSpec((None, d, LANES), lambda a, j: (a, 0, j)),
        out_shape=jax.ShapeDtypeStruct((ls, d, 2 * fp), BF16),
        compiler_params=_cparams(("arbitrary", "arbitrary")),
        name="stage_wi",
    )(ffn_wi.reshape(ls, d, ff2))
    wo_p = pl.pallas_call(
        kern,
        grid=(ls, nb),
        in_specs=[pl.BlockSpec((None, LANES, d), lambda a, j: (a, src(j), 0))],
        out_specs=pl.BlockSpec((None, LANES, d), lambda a, j: (a, j, 0)),
        out_shape=jax.ShapeDtypeStruct((ls, fp, d), BF16),
        compiler_params=_cparams(("arbitrary", "arbitrary")),
        name="stage_wo",
    )(ffn_wo.reshape(ls, ff, d))
    return wi_p.reshape(depth, 2, d, 2 * fp), wo_p.reshape(depth, 2, fp, d)


def _ffn_kernel(*refs, nf, final):
    if final:
        (x_ref, sh_ref, sc_ref, gt_ref, g_ref, wg_ref, wu_ref, wo_ref, fg_ref,
         o_ref, h_scr, acc_scr) = refs
    else:
        (x_ref, sh_ref, sc_ref, gt_ref, g_ref, wg_ref, wu_ref, wo_ref,
         o_ref, h_scr, acc_scr) = refs
    f = pl.program_id(1)

    @pl.when(f == 0)
    def _():
        h_scr[...] = _norm_mod(x_ref[...], g_ref[...], sc_ref[0], sh_ref[0]).astype(BF16)
        acc_scr[...] = jnp.zeros_like(acc_scr)

    h = h_scr[...]
    a = _dot(h, wg_ref[...])
    u = _dot(h, wu_ref[...])
    act = (a * jax.nn.sigmoid(a) * u).astype(BF16)
    acc_scr[...] += _dot(act, wo_ref[...])

    @pl.when(f == nf - 1)
    def _():
        y = x_ref[...] + 0.5 * gt_ref[0] * acc_scr[...]
        if final:
            y = y * lax.rsqrt(jnp.mean(y * y, -1, keepdims=True) + EPS) * fg_ref[...]
        o_ref[...] = y


FFN_TF = 512


def ffn_sublayer(x, shift, scale, gate, g, wi_p, wo_p, l, s, rows_per_batch, final_g=None):
    m, d = x.shape
    fp = wo_p.shape[2]
    tf = FFN_TF
    nf = fp // tf
    tm = _row_tile(512, rows_per_batch)
    final = final_g is not None
    in_specs = [
        pl.BlockSpec((tm, d), lambda i, f: (i, 0)),
        _mod_spec(shift, tm, d, rows_per_batch, lambda f: 0),
        _mod_spec(scale, tm, d, rows_per_batch, lambda f: 0),
        _mod_spec(gate, tm, d, rows_per_batch, lambda f: 0),
        pl.BlockSpec((1, d), lambda i, f: (0, 0)),
        pl.BlockSpec((None, None, d, tf), lambda i, f: (l, s, 0, f)),
        pl.BlockSpec((None, None, d, tf), lambda i, f: (l, s, 0, nf + f)),
        pl.BlockSpec((None, None, tf, d), lambda i, f: (l, s, f, 0)),
    ]
    args = [x, shift, scale, gate, g.reshape(1, d), wi_p, wi_p, wo_p]
    if final:
        in_specs.append(pl.BlockSpec((1, d), lambda i, f: (0, 0)))
        args.append(final_g.reshape(1, d))
    return pl.pallas_call(
        functools.partial(_ffn_kernel, nf=nf, final=final),
        grid=(m // tm, nf),
        in_specs=in_specs,
        out_specs=pl.BlockSpec((tm, d), lambda i, f: (i, 0)),
        out_shape=jax.ShapeDtypeStruct((m, d), F32),
        scratch_shapes=[pltpu.VMEM((tm, d), BF16), pltpu.VMEM((tm, d), F32)],
        compiler_params=_cparams(("arbitrary", "arbitrary")),
        name="ffn",
    )(*args)


def _rope_tile(y, cos, sin, rope):
    if rope == 128:
        r = pltpu.roll(y, 64, 1)
    else:
        lane = lax.broadcasted_iota(I32, y.shape, 1)
        r = jnp.where((lane % 64) < 32, pltpu.roll(y, 96, 1), pltpu.roll(y, 32, 1))
    return y * cos + r * sin


LO = "bf16 residual"
Q3 = "packed indexer query"
K3 = "packed indexer key"
_WIDTH = {Q3: 4, K3: 2}


def _emit(y, kind):
    if kind in (F32, BF16):
        return [y.astype(kind)]
    hi_b, lo_b = _split(y)
    if kind == LO:
        return [lo_b]
    hi, lo = hi_b.astype(F32), lo_b.astype(F32)
    first = lax.broadcasted_iota(I32, y.shape, 1) < 64
    if kind == K3:
        tiles = [jnp.where(first, hi, pltpu.roll(hi, 64, 1)), jnp.where(first, lo, 0.0)]
    else:
        hi_r = pltpu.roll(hi, 64, 1)
        tiles = [jnp.where(first, hi, pltpu.roll(lo, 64, 1)), jnp.where(first, hi, 0.0),
                 jnp.where(first, hi_r, lo), jnp.where(first, hi_r, 0.0)]
    return [t.astype(BF16) for t in tiles]


def _proj_kernel(*refs, rope, kinds, tn, precise):
    n_out = len(kinds)
    x_ref, sh_ref, sc_ref, g_ref, w_ref = refs[:5]
    pos = 5
    if rope:
        cos_ref, sin_ref = refs[5:7]
        pos = 7
    out_refs = refs[pos:pos + n_out]
    h_scr = refs[pos + n_out]
    j = pl.program_id(1)

    @pl.when(j == 0)
    def _():
        h_scr[...] = _norm_mod(x_ref[...], g_ref[...], sc_ref[0], sh_ref[0]).astype(h_scr.dtype)

    if precise:
        y = _dot3(h_scr[...], w_ref[...])
    else:
        y = _dot(h_scr[...], w_ref[...].astype(BF16))
    if rope:
        cos = cos_ref[...]
        sin = sin_ref[...]
    for c in range(tn // LANES):
        yc = y[:, c * LANES:(c + 1) * LANES]
        if rope:
            yc = _rope_tile(yc, cos, sin, rope)
        for o_ref, kind in zip(out_refs, kinds):
            tiles = _emit(yc, kind)
            for k, tile in enumerate(tiles):
                at = (c * len(tiles) + k) * LANES
                o_ref[:, at:at + LANES] = tile


def proj(x, shift, scale, g, w, slot, col_off, n_cols, rows_per_batch, dtypes,
         rope=0, tables=None, precise=False):
    m, d = x.shape
    tn = min(512, n_cols)
    tm = _row_tile(1024, rows_per_batch)
    off = col_off // tn
    assert col_off % tn == 0 and n_cols % tn == 0
    in_specs = [
        pl.BlockSpec((tm, d), lambda i, j: (i, 0)),
        _mod_spec(shift, tm, d, rows_per_batch, lambda j: 0),
        _mod_spec(scale, tm, d, rows_per_batch, lambda j: 0),
        pl.BlockSpec((1, d), lambda i, j: (0, 0)),
        pl.BlockSpec((None, d, tn), lambda i, j: (slot, 0, off + j)),
    ]
    args = [x, shift, scale, g.reshape(1, d), w]
    if rope:
        cos, sin = tables
        nbt = max(cos.shape[0] // tm, 1)
        in_specs += [pl.BlockSpec((tm, LANES), lambda i, j: (i % nbt, 0))] * 2
        args += [cos, sin]
    out = pl.pallas_call(
        functools.partial(_proj_kernel, rope=rope, kinds=tuple(dtypes), tn=tn, precise=precise),
        grid=(m // tm, n_cols // tn),
        in_specs=in_specs,
        out_specs=[pl.BlockSpec((tm, tn * _WIDTH.get(dt, 1)), lambda i, j: (i, j)) for dt in dtypes],
        out_shape=[jax.ShapeDtypeStruct((m, n_cols * _WIDTH.get(dt, 1)), dt if dt in (F32, BF16) else BF16)
                   for dt in dtypes],
        scratch_shapes=[pltpu.VMEM((tm, d), F32 if precise else BF16)],
        compiler_params=_cparams(("arbitrary", "arbitrary")),
        name="proj",
    )(*args)
    return out


def _outproj_kernel(a_ref, x_ref, gt_ref, w_ref, o_ref):
    y = _dot(a_ref[...], w_ref[...].astype(BF16))
    o_ref[...] = x_ref[...] + gt_ref[0] * y


def outproj(a, x, gate, w, slot, rows_per_batch):
    m, d = x.shape
    k = a.shape[1]
    tn = 512
    tm = _row_tile(1024, rows_per_batch)
    return pl.pallas_call(
        _outproj_kernel,
        grid=(m // tm, d // tn),
        in_specs=[pl.BlockSpec((tm, k), lambda i, j: (i, 0)),
                  pl.BlockSpec((tm, tn), lambda i, j: (i, j)),
                  _mod_spec(gate, tm, tn, rows_per_batch, lambda j: j),
                  pl.BlockSpec((None, k, tn), lambda i, j: (slot, 0, j))],
        out_specs=pl.BlockSpec((tm, tn), lambda i, j: (i, j)),
        out_shape=jax.ShapeDtypeStruct((m, d), F32),
        compiler_params=_cparams(("arbitrary", "arbitrary")),
        name="outproj",
    )(a, x, gate, w)


def _mlstm_kernel(q_ref, k_ref, v_ref, o_ref, gates_ref, bias_ref, ng_ref,
                  hh_ref, c_ref, n_ref, m_ref, *, heads, dk, dv, chunk):
    c_idx = pl.program_id(1)

    @pl.when(c_idx == 0)
    def _():
        c_ref[...] = jnp.zeros_like(c_ref)
        n_ref[...] = jnp.zeros_like(n_ref)
        m_ref[...] = jnp.zeros_like(m_ref)

    L = chunk
    g = gates_ref[...] + bias_ref[...]
    gt = g.T
    lf = _log_sigmoid(g)
    lft = _log_sigmoid(gt)
    row = lax.broadcasted_iota(I32, (L, L), 0)
    col = lax.broadcasted_iota(I32, (L, L), 1)
    causal = col <= row
    qscale = dk ** -0.5
    for h in range(heads):
        icol = g[:, h:h + 1]
        fcol = lf[:, heads + h:heads + h + 1]
        irow = gt[h:h + 1, :]
        frow = lft[heads + h:heads + h + 1, :]
        b_col = jnp.sum(jnp.where(causal, frow, 0.0), axis=1, keepdims=True)
        b_row = jnp.sum(jnp.where(row <= col, fcol, 0.0), axis=0, keepdims=True)
        dmat = jnp.where(causal, b_col - b_row + irow, NEG)
        m_prev = m_ref[0, h:h + 1, 0:1]
        inter = b_col + m_prev
        m_t = jnp.maximum(inter, jnp.max(dmat, axis=1, keepdims=True))
        w_intra = jnp.exp(dmat - m_t)
        w_inter = jnp.exp(inter - m_t)
        qf = q_ref[:, h * dk:(h + 1) * dk] * qscale
        kf = k_ref[:, h * dk:(h + 1) * dk]
        vf = v_ref[:, h * dv:(h + 1) * dv]
        s = _dot3(qf, kf, _dot_nt) * w_intra
        c_prev = c_ref[0, h]
        n_prev = n_ref[0, h:h + 1, :]
        num = _dot3(s, vf) + w_inter * _dot3(qf, c_prev, _dot_nt)
        den = jnp.sum(s, axis=1, keepdims=True) + w_inter * jnp.sum(qf * n_prev, axis=1, keepdims=True)
        hh = num / jnp.maximum(jnp.abs(den), jnp.exp(-m_t))
        m_new = m_t[L - 1:L, :]
        b_last = b_col[L - 1:L, :]
        w_s = jnp.exp(b_last - b_col + icol - m_new)
        decay = jnp.exp(b_last + m_prev - m_new)
        vw_t = (vf * w_s).T
        c_ref[0, h] = decay * c_prev + _dot3(vw_t, kf)
        n_ref[0, h:h + 1, :] = decay * n_prev + jnp.sum(kf * w_s, axis=0, keepdims=True)
        m_ref[0, h:h + 1, :] = jnp.broadcast_to(m_new, (1, LANES))
        hn = hh * lax.rsqrt(jnp.mean(hh * hh, -1, keepdims=True) + EPS) * ng_ref[:, h * dv:(h + 1) * dv]
        hn = hn * jax.nn.sigmoid(o_ref[:, h * dv:(h + 1) * dv])
        hh_ref[:, h * dv:(h + 1) * dv] = hn.astype(BF16)


def mlstm_prompt(y, gates, b_if, norm_g, bn, t):
    heads = A_HEADS
    dv = norm_g.shape[-1]
    dk = dv // 2
    qk = heads * dk
    vw = heads * dv
    L = A_CHUNK
    nc = t // L
    bias = jnp.zeros((1, LANES), F32).at[0, :2 * heads].set(b_if)
    row = lambda b, c: b * nc + c
    outs = pl.pallas_call(
        functools.partial(_mlstm_kernel, heads=heads, dk=dk, dv=dv, chunk=L),
        grid=(bn, nc),
        in_specs=[pl.BlockSpec((L, qk), lambda b, c: (row(b, c), 0)),
                  pl.BlockSpec((L, qk), lambda b, c: (row(b, c), 1)),
                  pl.BlockSpec((L, vw), lambda b, c: (row(b, c), 1)),
                  pl.BlockSpec((L, vw), lambda b, c: (row(b, c), 2)),
                  pl.BlockSpec((L, LANES), lambda b, c: (row(b, c), 0)),
                  pl.BlockSpec((1, LANES), lambda b, c: (0, 0)),
                  pl.BlockSpec((1, vw), lambda b, c: (0, 0))],
        out_specs=[pl.BlockSpec((L, vw), lambda b, c: (row(b, c), 0)),
                   pl.BlockSpec((1, heads, dv, dk), lambda b, c: (b, 0, 0, 0)),
                   pl.BlockSpec((1, heads, dk), lambda b, c: (b, 0, 0)),
                   pl.BlockSpec((1, heads, LANES), lambda b, c: (b, 0, 0))],
        out_shape=[jax.ShapeDtypeStruct((bn * t, vw), BF16),
                   jax.ShapeDtypeStruct((bn, heads, dv, dk), F32),
                   jax.ShapeDtypeStruct((bn, heads, dk), F32),
                   jax.ShapeDtypeStruct((bn, heads, LANES), F32)],
        compiler_params=_cparams(("arbitrary", "arbitrary")),
        name="mlstm_scan",
    )(y, y, y, y, gates, bias, norm_g.reshape(1, vw))
    hh, c_fin, n_fin, m_fin = outs
    return hh, c_fin, n_fin, m_fin[:, :, 0]


def _mlstm_step_kernel(q_ref, k_ref, v_ref, o_ref, ig_ref, fg_ref, c0_ref, n0_ref, m0_ref, ng_ref,
                       h_ref, c_ref, n_ref, m_ref, *, dk):
    cst = c0_ref[0]
    n0 = n0_ref[0]
    m0 = m0_ref[0]
    q = q_ref[0] * dk ** -0.5
    k = k_ref[0]
    v = v_ref[0]
    ig = ig_ref[0]
    fc = _log_sigmoid(fg_ref[0])
    inter = fc + m0
    m_t = jnp.maximum(inter, ig)
    w_intra = jnp.exp(ig - m_t)
    w_inter = jnp.exp(inter - m_t)
    s = jnp.sum(q * k, axis=-1, keepdims=True) * w_intra
    cq = jnp.sum(cst * q, axis=-1, keepdims=True)
    num = s * v + w_inter * cq
    den = s + w_inter * jnp.sum(n0 * q, axis=-1, keepdims=True)
    hh = num / jnp.maximum(jnp.abs(den), jnp.exp(-m_t))
    w_s = w_intra
    decay = jnp.exp(inter - m_t)
    c_ref[0] = decay * cst + w_s * (v * k)
    n_ref[0] = decay * n0 + w_s * k
    m_ref[0] = m_t
    hn = hh * lax.rsqrt(jnp.mean(hh * hh, axis=1, keepdims=True) + EPS) * ng_ref[...]
    h_ref[0] = hn * jax.nn.sigmoid(o_ref[0])


def mlstm_sample(y, gates, b_if, norm_g, c0, n0, m0, bs):
    heads = A_HEADS
    dv = norm_g.shape[-1]
    dk = dv // 2
    qk = heads * dk
    vw = heads * dv
    y = y[:bs]
    q = y[:, :qk].reshape(bs, heads, 1, dk)
    k = y[:, qk:2 * qk].reshape(bs, heads, 1, dk)
    v = y[:, 2 * qk:2 * qk + vw].reshape(bs, heads, dv, 1)
    o = y[:, 2 * qk + vw:].reshape(bs, heads, dv, 1)
    gb = gates[:bs, :2 * heads] + b_if
    ig = gb[:, :heads].reshape(bs, heads, 1, 1)
    fg = gb[:, heads:].reshape(bs, heads, 1, 1)
    spec = lambda *shape: pl.BlockSpec((1,) + shape, lambda b: (b,) + (0,) * len(shape))
    outs = pl.pallas_call(
        functools.partial(_mlstm_step_kernel, dk=dk),
        grid=(bs,),
        in_specs=[spec(heads, 1, dk), spec(heads, 1, dk), spec(heads, dv, 1), spec(heads, dv, 1),
                  spec(heads, 1, 1), spec(heads, 1, 1),
                  spec(heads, dv, dk), spec(heads, 1, dk), spec(heads, 1, 1),
                  pl.BlockSpec((heads, dv, 1), lambda b: (0, 0, 0))],
        out_specs=[spec(heads, dv, 1), spec(heads, dv, dk), spec(heads, 1, dk), spec(heads, 1, 1)],
        out_shape=[jax.ShapeDtypeStruct((bs, heads, dv, 1), F32),
                   jax.ShapeDtypeStruct((bs, heads, dv, dk), F32),
                   jax.ShapeDtypeStruct((bs, heads, 1, dk), F32),
                   jax.ShapeDtypeStruct((bs, heads, 1, 1), F32)],
        compiler_params=_cparams(("arbitrary",)),
        name="mlstm_step",
    )(q, k, v, o, ig, fg, c0, n0.reshape(bs, heads, 1, dk), m0.reshape(bs, heads, 1, 1),
      norm_g.reshape(heads, dv, 1))
    hh, c_new, n_new, m_new = outs
    return hh.reshape(bs, vw), c_new, n_new.reshape(bs, heads, dk), m_new.reshape(bs, heads)


def _diff_lambda(lam_ref, lam_init):
    l = lam_ref[...]
    a = jnp.sum(l[0:1, :] * l[1:2, :], axis=1, keepdims=True)
    b = jnp.sum(l[2:3, :] * l[3:4, :], axis=1, keepdims=True)
    return jnp.exp(a) - jnp.exp(b) + lam_init


def _flash_step(s, vs, m_scr, l_scr, acc_scr, c):
    m_prev = m_scr[c]
    m_new = jnp.maximum(m_prev, jnp.max(s, axis=1, keepdims=True))
    alpha = jnp.exp(m_prev - m_new)
    p = jnp.exp(s - m_new)
    l_scr[c] = alpha * l_scr[c] + jnp.sum(p, axis=1, keepdims=True)
    acc_scr[c] = alpha * acc_scr[c] + _dot(p.astype(BF16), vs)
    m_scr[c] = m_new


def _diff_prompt_kernel(q_ref, k_ref, v_ref, lam_ref, ng_ref, o_ref, m_scr, l_scr, acc_scr,
                        *, tq, tk, dk, lam_init):
    i = pl.program_id(2)
    scale = dk ** -0.5
    m_scr[...] = jnp.full_like(m_scr, NEG)
    l_scr[...] = jnp.zeros_like(l_scr)
    acc_scr[...] = jnp.zeros_like(acc_scr)

    def chunk(j, masked):
        start = pl.multiple_of(j * tk, tk)
        vs = v_ref[pl.ds(start, tk), :]
        for c in range(2):
            s = _dot_nt(q_ref[:, c * dk:(c + 1) * dk], k_ref[pl.ds(start, tk), c * dk:(c + 1) * dk]) * scale
            if masked:
                row = i * tq + lax.broadcasted_iota(I32, (tq, tk), 0)
                col = start + lax.broadcasted_iota(I32, (tq, tk), 1)
                s = jnp.where(col <= row, s, NEG)
            _flash_step(s, vs, m_scr, l_scr, acc_scr, c)

    def body(j, carry):
        chunk(j, False)
        return carry

    n_full = (i * tq) // tk
    lax.fori_loop(0, n_full, body, 0)
    chunk(n_full, True)
    lam = _diff_lambda(lam_ref, lam_init)
    o = acc_scr[0] / l_scr[0] - lam * (acc_scr[1] / l_scr[1])
    o = o * lax.rsqrt(jnp.mean(o * o, -1, keepdims=True) + EPS) * ng_ref[...] * (1.0 - lam_init)
    o_ref[...] = o.astype(o_ref.dtype)


def diff_prompt(qb, kb, vb, lam, norm_g, lam_init, bn, t):
    heads = B_HEADS
    dv = norm_g.shape[-1]
    dk = dv // 2
    tq = min(256, t)
    tk = min(1024, t)
    nq = t // tq
    return pl.pallas_call(
        functools.partial(_diff_prompt_kernel, tq=tq, tk=tk, dk=dk, lam_init=lam_init),
        grid=(bn, heads, nq),
        in_specs=[pl.BlockSpec((tq, 2 * dk), lambda b, h, i: (b * nq + i, h)),
                  pl.BlockSpec((t, 2 * dk), lambda b, h, i: (b, h)),
                  pl.BlockSpec((t, dv), lambda b, h, i: (b, h)),
                  pl.BlockSpec(lam.shape, lambda b, h, i: (0, 0)),
                  pl.BlockSpec((1, dv), lambda b, h, i: (0, h))],
        out_specs=pl.BlockSpec((tq, dv), lambda b, h, i: (b * nq + i, h)),
        out_shape=jax.ShapeDtypeStruct((bn * t, heads * dv), BF16),
        scratch_shapes=[pltpu.VMEM((2, tq, 1), F32), pltpu.VMEM((2, tq, 1), F32),
                        pltpu.VMEM((2, tq, dv), F32)],
        compiler_params=_cparams(("arbitrary", "arbitrary", "arbitrary")),
        name="diff_prompt",
    )(qb, kb, vb, lam, norm_g.reshape(1, heads * dv))


def _paged_kernel(pt_ref, q_ref, kn_ref, vn_ref, *rest, mode, group, n_steps, pps, lam_init):
    k_refs, v_refs, rest = rest[:pps], rest[pps:2 * pps], rest[2 * pps:]
    if mode == "diff":
        lam_ref, ng_ref, o_ref, m_scr, l_scr, acc_scr = rest
    else:
        bias_ref, biasn_ref, o_ref, exp_scr, m_scr, l_scr, acc_scr = rest
    p = pl.program_id(1)
    rows, dk = q_ref.shape[1:]
    ncol = PAGE_SIZE * group
    scale = dk ** -0.5

    @pl.when(p == 0)
    def _():
        m_scr[...] = jnp.full_like(m_scr, NEG)
        l_scr[...] = jnp.zeros_like(l_scr)
        acc_scr[...] = jnp.zeros_like(acc_scr)
        if mode == "dsa":
            t = lax.broadcasted_iota(I32, (PAGE_SIZE, ncol), 0)
            c = lax.broadcasted_iota(I32, (PAGE_SIZE, ncol), 1)
            exp_scr[...] = jnp.where(c // group == t, 1.0, 0.0).astype(BF16)

    q = q_ref[0]
    qb = q.astype(BF16)

    def page_scores(k_ref):
        if mode == "diff":
            half = rows // 2
            return jnp.concatenate(
                [_dot_nt(qb[c * half:(c + 1) * half], k_ref[pl.ds(c, ncol, stride=2), :].astype(BF16))
                 for c in range(2)], axis=0)
        return _dot_nt(qb, k_ref[...].astype(BF16))

    s = jnp.concatenate([page_scores(k_ref) for k_ref in k_refs], axis=1)
    r_i = lax.broadcasted_iota(I32, s.shape, 0)
    c_i = lax.broadcasted_iota(I32, s.shape, 1)
    keep = (c_i % group) == (r_i % group)
    if mode == "dsa":
        sel = jnp.where(bias_ref[0, 0] == 0.0, 1.0, 0.0).astype(BF16)
        sel = jnp.concatenate(
            [_dot(jnp.broadcast_to(sel[pp:pp + 1], (8, PAGE_SIZE)), exp_scr[...])[0:1] for pp in range(pps)],
            axis=1)
        keep = keep & (sel > 0.5)
    s = jnp.where(keep, s * scale, NEG)
    m_prev = m_scr[...]
    m_new = jnp.maximum(m_prev, jnp.max(s, axis=1, keepdims=True))
    alpha = jnp.exp(m_prev - m_new)
    pe = jnp.exp(s - m_new)
    l_scr[...] = alpha * l_scr[...] + jnp.sum(pe, axis=1, keepdims=True)
    pv = _dot(pe[:, :ncol].astype(BF16), v_refs[0][...].astype(BF16))
    for pp in range(1, pps):
        pv = pv + _dot(pe[:, pp * ncol:(pp + 1) * ncol].astype(BF16), v_refs[pp][...].astype(BF16))
    acc_scr[...] = alpha * acc_scr[...] + pv
    m_scr[...] = m_new

    @pl.when(p == n_steps - 1)
    def _():
        s_n = jnp.sum(q * kn_ref[0], axis=1, keepdims=True) * scale
        v_n = vn_ref[0]
        if mode == "dsa":
            s_n = s_n + biasn_ref[0][:, 0:1]
        else:
            v_n = jnp.concatenate([v_n, v_n], axis=0)
        m_prev = m_scr[...]
        m_new = jnp.maximum(m_prev, s_n)
        alpha = jnp.exp(m_prev - m_new)
        pn = jnp.exp(s_n - m_new)
        l_fin = alpha * l_scr[...] + pn
        acc = (alpha * acc_scr[...] + pn * v_n) / l_fin
        if mode == "diff":
            half = rows // 2
            o = acc[:half] - _diff_lambda(lam_ref, lam_init) * acc[half:]
            o = o * lax.rsqrt(jnp.mean(o * o, -1, keepdims=True) + EPS) * ng_ref[...]
            o_ref[0] = o * (1.0 - lam_init)
        else:
            o_ref[0] = acc


def paged_attention(mode, q, k_new, v_new, cache_k, cache_v, slot, page_table, extra, lam_init=0.0):
    bs, n_pages = page_table.shape
    rows, dk = q.shape[1:]
    hv, dv = v_new.shape[1:]
    group = hv
    ncol = PAGE_SIZE * group
    pps = max(c for c in (4, 2, 1) if n_pages % c == 0)
    n_steps = n_pages // pps
    seq_spec = lambda a: pl.BlockSpec((1,) + a.shape[1:], lambda b, p, pt: (b, 0, 0))

    def page_specs(a):
        return [pl.BlockSpec((None, None) + a.shape[2:],
                             lambda b, p, pt, pp=pp: (slot, pt[b, p * pps + pp], 0, 0)) for pp in range(pps)]

    scratch = [pltpu.VMEM((rows, 1), F32), pltpu.VMEM((rows, 1), F32), pltpu.VMEM((rows, dv), F32)]
    if mode == "diff":
        lam, norm_g = extra
        extra_specs = [pl.BlockSpec(lam.shape, lambda b, p, pt: (0, 0)),
                       pl.BlockSpec(norm_g.shape, lambda b, p, pt: (0, 0))]
        extra_args = [lam, norm_g]
    else:
        bias, bias_new = extra
        bias = bias.reshape(bs, n_steps, pps, PAGE_SIZE)
        extra_specs = [pl.BlockSpec((1, 1, pps, PAGE_SIZE), lambda b, p, pt: (b, p, 0, 0)),
                       pl.BlockSpec((1, 1, LANES), lambda b, p, pt: (b, 0, 0))]
        extra_args = [bias, bias_new]
        scratch = [pltpu.VMEM((PAGE_SIZE, ncol), BF16)] + scratch
    grid_spec = pltpu.PrefetchScalarGridSpec(
        num_scalar_prefetch=1,
        grid=(bs, n_steps),
        in_specs=[seq_spec(q), seq_spec(k_new), seq_spec(v_new)] + page_specs(cache_k)
        + page_specs(cache_v) + extra_specs,
        out_specs=pl.BlockSpec((1, hv, dv), lambda b, p, pt: (b, 0, 0)),
        scratch_shapes=scratch,
    )
    return pl.pallas_call(
        functools.partial(_paged_kernel, mode=mode, group=group, n_steps=n_steps, pps=pps,
                          lam_init=lam_init),
        grid_spec=grid_spec,
        out_shape=jax.ShapeDtypeStruct((bs, hv, dv), F32),
        compiler_params=_cparams(("arbitrary", "arbitrary")),
        name="paged_" + mode,
    )(page_table, q, k_new, v_new, *([cache_k] * pps), *([cache_v] * pps), *extra_args)


def _sortable_key(s):
    bits = lax.bitcast_convert_type(s, I32)
    return bits ^ ((bits >> 31) & 0x7FFFFFFF)


def _bisect_threshold(count_ge, shape, n_top):
    def step(it, t_u):
        bit = 31 - it
        cand_u = t_u | jnp.left_shift(jnp.int32(1), bit)
        cnt = count_ge(cand_u ^ INT_MIN)
        return jnp.where(cnt >= n_top, cand_u, t_u)

    t_u = lax.fori_loop(0, 32, step, jnp.zeros(shape, I32))
    return jnp.maximum(t_u ^ INT_MIN, INT_MIN + 1)


def _dsa_prompt_kernel(qi_ref, wt_ref, ki_ref, q_ref, k_ref, v_ref, o_ref,
                       key_scr, bias_scr, wb_scr, m_scr, l_scr, acc_scr, *, tq, tk, dh, n_top, wcol):
    i = pl.program_id(1)
    h = pl.program_id(2)
    scale = dh ** -0.5
    idx_scale = IDX_DIM ** -0.5 * IDX_HEADS ** -0.5
    groups = tq // LANES
    n_att = (i * tq) // tk + 1

    @pl.when(h == 0)
    def _():
        w = wt_ref[...] * idx_scale
        for hh in range(IDX_HEADS):
            wb_scr[hh] = jnp.broadcast_to(w[:, wcol + hh:wcol + hh + 1], (tq, LANES))

        def score_chunk(j, masked):
            start = pl.multiple_of(j * tq, tq)
            kc = ki_ref[pl.ds(start, tq), :]
            acc = [jnp.zeros((tq, LANES), F32) for _ in range(groups)]
            for hh in range(IDX_HEADS):
                s = _dot_nt(qi_ref[:, hh * IDX_PACK:(hh + 1) * IDX_PACK], kc)
                wb = wb_scr[hh]
                for c in range(groups):
                    acc[c] = acc[c] + jnp.maximum(s[:, c * LANES:(c + 1) * LANES], 0.0) * wb
            for c in range(groups):
                key = _sortable_key(acc[c])
                if masked:
                    row = lax.broadcasted_iota(I32, (tq, LANES), 0)
                    col = lax.broadcasted_iota(I32, (tq, LANES), 1) + c * LANES
                    key = jnp.where(col <= row, key, INT_MIN)
                key_scr[:, pl.ds(pl.multiple_of(start + c * LANES, LANES), LANES)] = key

        def score_body(j, carry):
            score_chunk(j, False)
            return carry

        lax.fori_loop(0, i, score_body, 0)
        score_chunk(i, True)

        for rb in range(tq // BISECT_ROWS):
            rows = pl.ds(rb * BISECT_ROWS, BISECT_ROWS)

            def count_ge(cand, rows=rows):
                cand_b = jnp.broadcast_to(cand, (BISECT_ROWS, LANES))

                def cbody(j, part):
                    start = pl.multiple_of(j * tq, tq)
                    for c in range(groups):
                        kk = key_scr[rows, pl.ds(start + c * LANES, LANES)]
                        part = part + jnp.where(kk >= cand_b, 1.0, 0.0)
                    return part

                part = lax.fori_loop(0, i + 1, cbody, jnp.zeros((BISECT_ROWS, LANES), F32))
                return jnp.sum(part, axis=1, keepdims=True)

            thr = _bisect_threshold(count_ge, (BISECT_ROWS, 1), n_top)
            thr = jnp.broadcast_to(thr, (BISECT_ROWS, LANES))

            def bias_body(j, carry, rows=rows, thr=thr):
                start = pl.multiple_of(j * LANES, LANES)
                kk = key_scr[rows, pl.ds(start, LANES)]
                bias_scr[rows, pl.ds(start, LANES)] = jnp.where(kk >= thr, 0.0, NEG)
                return carry

            lax.fori_loop(0, (i + 1) * groups, bias_body, 0)

        def fill_body(j, carry):
            start = pl.multiple_of(j * LANES, LANES)
            bias_scr[:, pl.ds(start, LANES)] = jnp.full((tq, LANES), NEG, F32)
            return carry

        lax.fori_loop((i + 1) * groups, n_att * (tk // LANES), fill_body, 0)

    m_scr[...] = jnp.full_like(m_scr, NEG)
    l_scr[...] = jnp.zeros_like(l_scr)
    acc_scr[...] = jnp.zeros_like(acc_scr)
    qh = q_ref[...]

    def att_body(j, carry):
        start = pl.multiple_of(j * tk, tk)
        s = _dot_nt(qh, k_ref[pl.ds(start, tk), :]) * scale + bias_scr[:, pl.ds(start, tk)]
        _flash_step(s, v_ref[pl.ds(start, tk), :], m_scr, l_scr, acc_scr, 0)
        return carry

    lax.fori_loop(0, n_att, att_body, 0)
    o_ref[...] = (acc_scr[0] / l_scr[0]).astype(o_ref.dtype)


def dsa_prompt(qi3, tail, ki3, qb, kb, vb, bn, t, wcol):
    heads = C_HEADS
    dh = qb.shape[1] // heads
    tq = min(256, t)
    tk = min(1024, t)
    nq = t // tq
    n_top = min(TOPK_MAX, t // 4)
    return pl.pallas_call(
        functools.partial(_dsa_prompt_kernel, tq=tq, tk=tk, dh=dh, n_top=n_top, wcol=wcol),
        grid=(bn, nq, heads),
        in_specs=[pl.BlockSpec((tq, IDX_HEADS * IDX_PACK), lambda b, i, h: (b * nq + i, 0)),
                  pl.BlockSpec((tq, LANES), lambda b, i, h: (b * nq + i, 0)),
                  pl.BlockSpec((t, IDX_PACK), lambda b, i, h: (b, 0)),
                  pl.BlockSpec((tq, dh), lambda b, i, h: (b * nq + i, h)),
                  pl.BlockSpec((t, dh), lambda b, i, h: (b, h)),
                  pl.BlockSpec((t, dh), lambda b, i, h: (b, h))],
        out_specs=pl.BlockSpec((tq, dh), lambda b, i, h: (b * nq + i, h)),
        out_shape=jax.ShapeDtypeStruct((bn * t, heads * dh), BF16),
        scratch_shapes=[pltpu.VMEM((tq, t), I32), pltpu.VMEM((tq, t), F32),
                        pltpu.VMEM((IDX_HEADS, tq, LANES), F32),
                        pltpu.VMEM((1, tq, 1), F32), pltpu.VMEM((1, tq, 1), F32),
                        pltpu.VMEM((1, tq, dh), F32)],
        compiler_params=_cparams(("arbitrary", "arbitrary", "arbitrary")),
        name="dsa_prompt",
    )(qi3, tail, ki3, qb, kb, vb)


def _dsa_index_kernel(pt_ref, qi_ref, w_ref, kin_ref, *rest, n_steps, pps, n_top):
    kp_refs, (bias_ref, biasn_ref, sc_scr) = rest[:pps], rest[pps:]
    p = pl.program_id(1)
    idx_scale = IDX_DIM ** -0.5 * IDX_HEADS ** -0.5
    qi = qi_ref[0]
    w = w_ref[0] * idx_scale
    for pp, kp_ref in enumerate(kp_refs):
        s = _dot3(qi, kp_ref[...], _dot_nt)
        sc_scr[pl.ds(p * pps + pp, 1), :] = jnp.sum(jnp.maximum(s, 0.0) * w, axis=0, keepdims=True)

    @pl.when(p == n_steps - 1)
    def _():
        s_n = jnp.sum(qi * kin_ref[0], axis=1, keepdims=True)
        sc_new = jnp.sum(jnp.maximum(s_n, 0.0) * w, axis=0, keepdims=True)
        key = _sortable_key(sc_scr[...])
        key_new = _sortable_key(sc_new)

        def count_ge(cand):
            hit = jnp.where(key >= cand, 1.0, 0.0)
            cnt = jnp.sum(jnp.sum(hit, axis=1, keepdims=True), axis=0, keepdims=True)
            return cnt + jnp.where(key_new >= cand, 1.0, 0.0)

        thr = _bisect_threshold(count_ge, (1, 1), n_top)
        bias_ref[0] = jnp.where(key >= thr, 0.0, NEG)
        biasn_ref[0] = jnp.broadcast_to(jnp.where(key_new >= thr, 0.0, NEG), (1, LANES))


def dsa_sample_index(qi, w, ki_new, cache_ki, slot, page_table):
    bs, n_pages = page_table.shape
    n_top = min(TOPK_MAX, (n_pages * PAGE_SIZE + 1) // 4)
    pps = max(c for c in (16, 8, 4, 2, 1) if n_pages % c == 0)
    n_steps = n_pages // pps
    grid_spec = pltpu.PrefetchScalarGridSpec(
        num_scalar_prefetch=1,
        grid=(bs, n_steps),
        in_specs=[pl.BlockSpec((1, IDX_HEADS, IDX_DIM), lambda b, p, pt: (b, 0, 0)),
                  pl.BlockSpec((1, IDX_HEADS, 1), lambda b, p, pt: (b, 0, 0)),
                  pl.BlockSpec((1, 1, IDX_DIM), lambda b, p, pt: (b, 0, 0))]
        + [pl.BlockSpec((None, None, PAGE_SIZE, IDX_DIM),
                        lambda b, p, pt, pp=pp: (slot, pt[b, p * pps + pp], 0, 0)) for pp in range(pps)],
        out_specs=[pl.BlockSpec((1, n_pages, PAGE_SIZE), lambda b, p, pt: (b, 0, 0)),
                   pl.BlockSpec((1, 1, LANES), lambda b, p, pt: (b, 0, 0))],
        scratch_shapes=[pltpu.VMEM((n_pages, PAGE_SIZE), F32)],
    )
    bias, bias_new = pl.pallas_call(
        functools.partial(_dsa_index_kernel, n_steps=n_steps, pps=pps, n_top=n_top),
        grid_spec=grid_spec,
        out_shape=[jax.ShapeDtypeStruct((bs, n_pages, PAGE_SIZE), F32),
                   jax.ShapeDtypeStruct((bs, 1, LANES), F32)],
        compiler_params=_cparams(("arbitrary", "arbitrary")),
        name="dsa_index",
    )(page_table, qi, w, ki_new, *([cache_ki] * pps))
    return bias, bias_new


def _rope_tables(pos, half):
    inv = ROPE_THETA ** (-jnp.arange(half, dtype=F32) / half)
    ang = pos.astype(F32)[:, None] * inv[None, :]
    return jnp.cos(ang), jnp.sin(ang)


def _tables_128(pos):
    c, s = _rope_tables(pos, 64)
    return jnp.concatenate([c, c], -1), jnp.concatenate([-s, s], -1)


def _tables_64(pos, tail):
    c, s = _rope_tables(pos, 32)
    if tail:
        one, zero = jnp.ones_like(c), jnp.zeros_like(s)
        return jnp.concatenate([c, c, one, one], -1), jnp.concatenate([-s, s, zero, zero], -1)
    return jnp.concatenate([c, c, c, c], -1), jnp.concatenate([-s, s, -s, s], -1)


def _pad_cols(w, n):
    return jnp.pad(w, ((0, 0), (0, 0), (0, n - w.shape[-1])))


def kernel(x_prompt, x_sample, c_prompt, c_sample, state_a_C, state_a_n, state_a_m, cache_b_k,
           cache_b_v, cache_c_k, cache_c_v, cache_c_kidx, page_table, ada_w, ada_b, norm_g, ffn_wi,
           ffn_wo, a_w_in, a_b_if, a_norm_g, a_w_out, b_w_in, b_lambda, b_norm_g, b_w_out, c_w_in,
           c_w_out, final_g):
    bp, t_p, d = x_prompt.shape
    bs = x_sample.shape[0]
    depth = ada_w.shape[0]
    n_pages = page_table.shape[1]
    past = n_pages * PAGE_SIZE
    n_pool = cache_b_k.shape[1]
    rs = SUBLANES_BF16 * ((bs + SUBLANES_BF16 - 1) // SUBLANES_BF16)
    mp = bp * t_p

    ff = ffn_wo.shape[2]
    fp = FFN_TF * ((ff + FFN_TF - 1) // FFN_TF)
    wi_p, wo_p = stage_ffn_weights(ffn_wi, ffn_wo, fp)

    c_rows = jnp.zeros((rs, d), F32).at[:bp].set(c_prompt).at[bp:bp + bs].set(c_sample)
    mod = ada_all(c_rows, ada_w, ada_b).reshape(depth, rs, N_ADA, d)

    xp = x_prompt.reshape(mp, d)
    xs = jnp.zeros((rs, d), F32).at[:bs].set(x_sample.reshape(bs, d))

    pos_p = jnp.arange(t_p, dtype=jnp.int32)
    pos_s = jnp.full((rs,), past, jnp.int32)
    tab128_p, tab128_s = _tables_128(pos_p), _tables_128(pos_s)
    tab64_p, tab64_s = _tables_64(pos_p, False), _tables_64(pos_s, False)
    tabt_p, tabt_s = _tables_64(pos_p, True), _tables_64(pos_s, True)

    a_main = 2 * A_HEADS * (a_norm_g.shape[-1] // 2) + 2 * A_HEADS * a_norm_g.shape[-1]
    a_tail_w = _pad_cols(a_w_in[:, :, a_main:], LANES)
    c_main = 3 * d + IDX_HEADS * IDX_DIM
    c_tail_w = _pad_cols(c_w_in[:, :, c_main:], LANES)

    kind_of = [i % 3 for i in range(depth)]
    outs = {k: [] for k in ("aCp", "anp", "amp", "aCs", "ans", "ams", "bkp", "bvp", "bks", "bvs",
                            "ckp", "cvp", "cip", "cks", "cvs", "cis")}
    for l in range(depth):
        kind = kind_of[l]
        j = kind_of[:l].count(kind)
        mp_l = [mod[l, :bp, k][:, None, :] for k in range(N_ADA)]
        ms_l = [mod[l, bp:bp + bs, k] for k in range(N_ADA)]
        ms_l = [jnp.zeros((1, rs, d), F32).at[0, :bs].set(m) for m in ms_l]

        xp = ffn_sublayer(xp, mp_l[0], mp_l[1], mp_l[2], norm_g[l, 0], wi_p, wo_p, l, 0, t_p)
        xs = ffn_sublayer(xs, ms_l[0], ms_l[1], ms_l[2], norm_g[l, 0], wi_p, wo_p, l, 0, rs)
        pj_p = functools.partial(proj, xp, mp_l[3], mp_l[4], norm_g[l, 1], rows_per_batch=t_p)
        pj_s = functools.partial(proj, xs, ms_l[3], ms_l[4], norm_g[l, 1], rows_per_batch=rs)

        if kind == 0:
            (y_p,) = pj_p(a_w_in, j, 0, a_main, dtypes=(F32,))
            (g_p,) = pj_p(a_tail_w, j, 0, LANES, dtypes=(F32,), precise=True)
            hh_p, c_p, n_p, m_p = mlstm_prompt(y_p, g_p, a_b_if[j], a_norm_g[j], bp, t_p)
            (y_s,) = pj_s(a_w_in, j, 0, a_main, dtypes=(F32,))
            (g_s,) = pj_s(a_tail_w, j, 0, LANES, dtypes=(F32,), precise=True)
            hh_s, c_s, n_s, m_s = mlstm_sample(y_s, g_s, a_b_if[j], a_norm_g[j],
                                               state_a_C[j], state_a_n[j], state_a_m[j], bs)
            op_in = hh_p
            os_in = jnp.zeros((rs, hh_s.shape[1]), BF16).at[:bs].set(hh_s.astype(BF16))
            w_out = a_w_out
            outs["aCp"].append(c_p); outs["anp"].append(n_p); outs["amp"].append(m_p)
            outs["aCs"].append(c_s); outs["ans"].append(n_s); outs["ams"].append(m_s)
        elif kind == 1:
            lam_init = 0.8 - 0.6 * math.exp(-0.3 * l)
            bqk = b_w_in.shape[2] // 3
            (q_pb,) = pj_p(b_w_in, j, 0, bqk, dtypes=(BF16,), rope=128, tables=tab128_p)
            k_p, k_pb = pj_p(b_w_in, j, bqk, bqk, dtypes=(F32, BF16), rope=128, tables=tab128_p)
            v_p, v_pb = pj_p(b_w_in, j, 2 * bqk, bqk, dtypes=(F32, BF16))
            op_in = diff_prompt(q_pb, k_pb, v_pb, b_lambda[j], b_norm_g[j], lam_init, bp, t_p)
            (q_s,) = pj_s(b_w_in, j, 0, bqk, dtypes=(F32,), rope=128, tables=tab128_s)
            (k_s,) = pj_s(b_w_in, j, bqk, bqk, dtypes=(F32,), rope=128, tables=tab128_s)
            (v_s,) = pj_s(b_w_in, j, 2 * bqk, bqk, dtypes=(F32,))
            comp_major = lambda a: a[:bs].reshape(bs, B_HEADS, 2, -1).transpose(0, 2, 1, 3).reshape(
                bs, 2 * B_HEADS, -1)
            o_s = paged_attention(
                "diff", comp_major(q_s), comp_major(k_s), v_s[:bs].reshape(bs, B_HEADS, -1),
                cache_b_k.reshape(cache_b_k.shape[0], n_pool, PAGE_SIZE * B_HEADS * 2, -1),
                cache_b_v.reshape(cache_b_v.shape[0], n_pool, PAGE_SIZE * B_HEADS, -1),
                j, page_table, (b_lambda[j], b_norm_g[j]), lam_init)
            os_in = jnp.zeros((rs, bqk), BF16).at[:bs].set(o_s.reshape(bs, bqk).astype(BF16))
            w_out = b_w_out
            outs["bkp"].append(k_p.reshape(bp, t_p, B_HEADS, 2, -1))
            outs["bvp"].append(v_p.reshape(bp, t_p, B_HEADS, -1))
            outs["bks"].append(k_s[:bs].reshape(bs, 1, B_HEADS, 2, -1))
            outs["bvs"].append(v_s[:bs].reshape(bs, 1, B_HEADS, -1))
        else:
            cw = d
            ni = IDX_HEADS * IDX_DIM
            (q_pb,) = pj_p(c_w_in, j, 0, cw, dtypes=(BF16,), rope=128, tables=tab128_p)
            k_p, k_pb = pj_p(c_w_in, j, cw, cw, dtypes=(F32, BF16), rope=128, tables=tab128_p)
            v_p, v_pb = pj_p(c_w_in, j, 2 * cw, cw, dtypes=(F32, BF16))
            (qi_p3,) = pj_p(c_w_in, j, 3 * cw, ni, dtypes=(Q3,), rope=64, tables=tab64_p)
            t_pf, ki_p3 = pj_p(c_tail_w, j, 0, LANES, dtypes=(F32, K3), rope=64, tables=tabt_p)
            op_in = dsa_prompt(qi_p3, t_pf, ki_p3, q_pb, k_pb, v_pb, bp, t_p, IDX_DIM)
            (q_s,) = pj_s(c_w_in, j, 0, cw, dtypes=(F32,), rope=128, tables=tab128_s)
            (k_s,) = pj_s(c_w_in, j, cw, cw, dtypes=(F32,), rope=128, tables=tab128_s)
            (v_s,) = pj_s(c_w_in, j, 2 * cw, cw, dtypes=(F32,))
            (qi_s,) = pj_s(c_w_in, j, 3 * cw, ni, dtypes=(F32,), rope=64, tables=tab64_s)
            (t_s,) = pj_s(c_tail_w, j, 0, LANES, dtypes=(F32,), rope=64, tables=tabt_s)
            ki_s = t_s[:bs, :IDX_DIM]
            bias, bias_new = dsa_sample_index(
                qi_s[:bs].reshape(bs, IDX_HEADS, IDX_DIM),
                t_s[:bs, IDX_DIM:IDX_DIM + IDX_HEADS].reshape(bs, IDX_HEADS, 1),
                ki_s[:, None], cache_c_kidx, j, page_table)
            o_s = paged_attention(
                "dsa", q_s[:bs].reshape(bs, C_HEADS, -1), k_s[:bs].reshape(bs, C_HEADS, -1),
                v_s[:bs].reshape(bs, C_HEADS, -1),
                cache_c_k.reshape(cache_c_k.shape[0], n_pool, PAGE_SIZE * C_HEADS, -1),
                cache_c_v.reshape(cache_c_v.shape[0], n_pool, PAGE_SIZE * C_HEADS, -1),
                j, page_table, (bias, bias_new))
            os_in = jnp.zeros((rs, cw), BF16).at[:bs].set(o_s.reshape(bs, cw).astype(BF16))
            w_out = c_w_out
            outs["ckp"].append(k_p.reshape(bp, t_p, C_HEADS, -1))
            outs["cvp"].append(v_p.reshape(bp, t_p, C_HEADS, -1))
            outs["cip"].append(t_pf[:, :IDX_DIM].reshape(bp, t_p, IDX_DIM))
            outs["cks"].append(k_s[:bs].reshape(bs, 1, C_HEADS, -1))
            outs["cvs"].append(v_s[:bs].reshape(bs, 1, C_HEADS, -1))
            outs["cis"].append(ki_s.reshape(bs, 1, IDX_DIM))

        xp = outproj(op_in, xp, mp_l[5], w_out, j, t_p)
        xs = outproj(os_in, xs, ms_l[5], w_out, j, rs)
        fin = final_g if l == depth - 1 else None
        xp = ffn_sublayer(xp, mp_l[6], mp_l[7], mp_l[8], norm_g[l, 2], wi_p, wo_p, l, 1, t_p, fin)
        xs = ffn_sublayer(xs, ms_l[6], ms_l[7], ms_l[8], norm_g[l, 2], wi_p, wo_p, l, 1, rs, fin)

    st = lambda k: jnp.stack(outs[k])
    return (xp.reshape(bp, t_p, d), xs[:bs].reshape(bs, 1, d),
            st("aCp"), st("anp"), st("amp"), st("aCs"), st("ans"), st("ams"),
            st("bkp"), st("bvp"), st("bks"), st("bvs"),
            st("ckp"), st("cvp"), st("cip"), st("cks"), st("cvs"), st("cis"))
```

```python
import functools
import math

import jax
import jax.numpy as jnp
from jax import lax
from jax.experimental import pallas as pl
from jax.experimental.pallas import tpu as pltpu

F32 = jnp.float32
BF16 = jnp.bfloat16
I32 = jnp.int32

EPS = 1e-6
ROPE_THETA = 10000.0
N_ADA = 9
TOPK_MAX = 256
PAGE_SIZE = 128
A_HEADS = 8
A_CHUNK = 128
B_HEADS = 8
C_HEADS = 16
IDX_HEADS = 16
IDX_DIM = 64
IDX_PACK = 4 * IDX_DIM

LANES = 128
SUBLANES_BF16 = 16
VMEM_LIMIT = 56 * 1024 * 1024
BISECT_ROWS = 64
NEG = -1e30
INT_MIN = -2 ** 31


def _cparams(sem):
    return pltpu.CompilerParams(dimension_semantics=sem, vmem_limit_bytes=VMEM_LIMIT)


def _dot(a, b):
    return jnp.dot(a, b, preferred_element_type=F32)


def _dot_nt(a, b):
    return lax.dot_general(a, b, (((1,), (1,)), ((), ())), preferred_element_type=F32)


def _split(x):
    hi = x.astype(BF16)
    return hi, (x - hi.astype(F32)).astype(BF16)


def _dot3(a, b, dot=_dot):
    a_hi, a_lo = _split(a)
    b_hi, b_lo = _split(b)
    return dot(a_hi, b_hi) + (dot(a_hi, b_lo) + dot(a_lo, b_hi))


def _norm_mod(x, g, scale, shift):
    y = x * lax.rsqrt(jnp.mean(x * x, -1, keepdims=True) + EPS) * g
    return y * (1.0 + scale) + shift


def _log_sigmoid(x):
    return jnp.minimum(x, 0.0) - jnp.log1p(jnp.exp(-jnp.abs(x)))


def _ada_kernel(c_ref, w_ref, b_ref, o_ref):
    c = c_ref[...]
    a = (c * jax.nn.sigmoid(c)).astype(BF16)
    o_ref[0] = _dot(a, w_ref[0].astype(BF16)) + b_ref[0]


def ada_all(c_rows, ada_w, ada_b):
    depth, d, n = ada_w.shape
    r = c_rows.shape[0]
    tn = 1024
    return pl.pallas_call(
        _ada_kernel,
        grid=(depth, n // tn),
        in_specs=[pl.BlockSpec((r, d), lambda l, j: (0, 0)),
                  pl.BlockSpec((1, d, tn), lambda l, j: (l, 0, j)),
                  pl.BlockSpec((1, 1, tn), lambda l, j: (l, 0, j))],
        out_specs=pl.BlockSpec((1, r, tn), lambda l, j: (l, 0, j)),
        out_shape=jax.ShapeDtypeStruct((depth, r, n), F32),
        compiler_params=_cparams(("arbitrary", "arbitrary")),
        name="ada_mod",
    )(c_rows, ada_w, ada_b.reshape(depth, 1, n))


def _row_tile(want, rows_per_batch):
    return min(want, rows_per_batch)


def _mod_spec(mod, tm, tn, rows_per_batch, col_of):
    r = mod.shape[1]
    bpb = max(rows_per_batch // tm, 1)
    return pl.BlockSpec((1, r, tn), lambda i, j: (i // bpb, 0, col_of(j)))


def _stage_wi_kernel(w_ref, o_ref):
    ff = w_ref.shape[-1]
    o_ref[:, :ff] = w_ref[...].astype(BF16)
    o_ref[:, ff:] = jnp.zeros((o_ref.shape[0], o_ref.shape[1] - ff), BF16)


def _stage_wo_kernel(w_ref, o_ref):
    o_ref[...] = w_ref[...].astype(BF16)


def stage_ffn_weights(ffn_wi, ffn_wo, fp):
    depth, _, d, ff2 = ffn_wi.shape
    ff = ff2 // 2
    assert ff % LANES == 0 and fp % LANES == 0
    ls = depth * 2
    rb = min(128, d)
    wi_p = pl.pallas_call(
        _stage_wi_kernel,
        grid=(ls, d // rb, 2),
        in_specs=[pl.BlockSpec((None, rb, ff), lambda a, r, h: (a, r, h))],
        out_specs=pl.BlockSpec((None, rb, fp), lambda a, r, h: (a, r, h)),
        out_shape=jax.ShapeDtypeStruct((ls, d, 2 * fp), BF16),
        compiler_params=_cparams(("arbitrary", "arbitrary", "arbitrary")),
        name="stage_wi",
    )(ffn_wi.reshape(ls, d, ff2))
    wrb = max(r for r in range(SUBLANES_BF16, 1025, SUBLANES_BF16) if ff % r == 0)
    wo_b = pl.pallas_call(
        _stage_wo_kernel,
        grid=(ls, ff // wrb),
        in_specs=[pl.BlockSpec((None, wrb, d), lambda a, j: (a, j, 0))],
        out_specs=pl.BlockSpec((None, wrb, d), lambda a, j: (a, j, 0)),
        out_shape=jax.ShapeDtypeStruct((ls, ff, d), BF16),
        compiler_params=_cparams(("arbitrary", "arbitrary")),
        name="stage_wo",
    )(ffn_wo.reshape(ls, ff, d))
    return wi_p.reshape(depth, 2, d, 2 * fp), wo_b.reshape(depth, 2, ff, d)


def _ffn_kernel(*refs, nf, last_rows, final):
    if final:
        (x_ref, sh_ref, sc_ref, gt_ref, g_ref, wg_ref, wu_ref, wo_ref, fg_ref,
         o_ref, h_scr, acc_scr) = refs
    else:
        (x_ref, sh_ref, sc_ref, gt_ref, g_ref, wg_ref, wu_ref, wo_ref,
         o_ref, h_scr, acc_scr) = refs
    f = pl.program_id(1)

    @pl.when(f == 0)
    def _():
        h_scr[...] = _norm_mod(x_ref[...], g_ref[...], sc_ref[0], sh_ref[0]).astype(BF16)
        acc_scr[...] = jnp.zeros_like(acc_scr)

    h = h_scr[...]
    a = _dot(h, wg_ref[...])
    u = _dot(h, wu_ref[...])
    act = (a * jax.nn.sigmoid(a) * u).astype(BF16)
    tf = wo_ref.shape[0]

    @pl.when(f < nf - 1)
    def _():
        acc_scr[...] += _dot(act, wo_ref[...])

    @pl.when(f == nf - 1)
    def _():
        wo = wo_ref[...]
        if last_rows < tf:
            wo = jnp.where(lax.broadcasted_iota(I32, wo.shape, 0) < last_rows, wo, jnp.zeros_like(wo))
        acc_scr[...] += _dot(act, wo)
        y = x_ref[...] + 0.5 * gt_ref[0] * acc_scr[...]
        if final:
            y = y * lax.rsqrt(jnp.mean(y * y, -1, keepdims=True) + EPS) * fg_ref[...]
        o_ref[...] = y


FFN_TF = 512


def ffn_sublayer(x, shift, scale, gate, g, wi_p, wo_p, l, s, rows_per_batch, final_g=None):
    m, d = x.shape
    fp = wi_p.shape[3] // 2
    tf = FFN_TF
    nf = fp // tf
    last_rows = wo_p.shape[2] - (nf - 1) * tf
    tm = _row_tile(512, rows_per_batch)
    final = final_g is not None
    in_specs = [
        pl.BlockSpec((tm, d), lambda i, f: (i, 0)),
        _mod_spec(shift, tm, d, rows_per_batch, lambda f: 0),
        _mod_spec(scale, tm, d, rows_per_batch, lambda f: 0),
        _mod_spec(gate, tm, d, rows_per_batch, lambda f: 0),
        pl.BlockSpec((1, d), lambda i, f: (0, 0)),
        pl.BlockSpec((None, None, d, tf), lambda i, f: (l, s, 0, f)),
        pl.BlockSpec((None, None, d, tf), lambda i, f: (l, s, 0, nf + f)),
        pl.BlockSpec((None, None, tf, d), lambda i, f: (l, s, f, 0)),
    ]
    args = [x, shift, scale, gate, g.reshape(1, d), wi_p, wi_p, wo_p]
    if final:
        in_specs.append(pl.BlockSpec((1, d), lambda i, f: (0, 0)))
        args.append(final_g.reshape(1, d))
    return pl.pallas_call(
        functools.partial(_ffn_kernel, nf=nf, last_rows=last_rows, final=final),
        grid=(m // tm, nf),
        in_specs=in_specs,
        out_specs=pl.BlockSpec((tm, d), lambda i, f: (i, 0)),
        out_shape=jax.ShapeDtypeStruct((m, d), F32),
        scratch_shapes=[pltpu.VMEM((tm, d), BF16), pltpu.VMEM((tm, d), F32)],
        compiler_params=_cparams(("arbitrary", "arbitrary")),
        name="ffn",
    )(*args)


def _rope_tile(y, cos, sin, rope):
    if rope == 128:
        r = pltpu.roll(y, 64, 1)
    else:
        lane = lax.broadcasted_iota(I32, y.shape, 1)
        r = jnp.where((lane % 64) < 32, pltpu.roll(y, 96, 1), pltpu.roll(y, 32, 1))
    return y * cos + r * sin


LO = "bf16 residual"
HEADS = "bf16, one (rows, 128) slab per 128 output columns"
Q3 = "packed indexer query"
K3 = "packed indexer key"
_WIDTH = {Q3: 4, K3: 2}


def _emit(y, kind):
    if kind in (F32, BF16):
        return [y.astype(kind)]
    hi_b, lo_b = _split(y)
    if kind == LO:
        return [lo_b]
    hi, lo = hi_b.astype(F32), lo_b.astype(F32)
    first = lax.broadcasted_iota(I32, y.shape, 1) < 64
    if kind == K3:
        tiles = [jnp.where(first, hi, pltpu.roll(hi, 64, 1)), jnp.where(first, lo, 0.0)]
    else:
        hi_r = pltpu.roll(hi, 64, 1)
        tiles = [jnp.where(first, hi, pltpu.roll(lo, 64, 1)), jnp.where(first, hi, 0.0),
                 jnp.where(first, hi_r, lo), jnp.where(first, hi_r, 0.0)]
    return [t.astype(BF16) for t in tiles]


def _proj_kernel(*refs, rope, kinds, tn, precise):
    n_out = len(kinds)
    x_ref, sh_ref, sc_ref, g_ref, w_ref = refs[:5]
    pos = 5
    if rope:
        cos_ref, sin_ref = refs[5:7]
        pos = 7
    out_refs = refs[pos:pos + n_out]
    h_scr = refs[pos + n_out]
    j = pl.program_id(1)

    @pl.when(j == 0)
    def _():
        h_scr[...] = _norm_mod(x_ref[...], g_ref[...], sc_ref[0], sh_ref[0]).astype(h_scr.dtype)

    if precise:
        y = _dot3(h_scr[...], w_ref[...])
    else:
        y = _dot(h_scr[...], w_ref[...].astype(BF16))
    if rope:
        cos = cos_ref[...]
        sin = sin_ref[...]
    for c in range(tn // LANES):
        yc = y[:, c * LANES:(c + 1) * LANES]
        if rope:
            yc = _rope_tile(yc, cos, sin, rope)
        for o_ref, kind in zip(out_refs, kinds):
            if kind == HEADS:
                o_ref[c] = yc.astype(BF16)
                continue
            tiles = _emit(yc, kind)
            for k, tile in enumerate(tiles):
                at = (c * len(tiles) + k) * LANES
                o_ref[:, at:at + LANES] = tile


def proj(x, shift, scale, g, w, slot, col_off, n_cols, rows_per_batch, dtypes,
         rope=0, tables=None, precise=False):
    m, d = x.shape
    tn = min(512, n_cols)
    tm = _row_tile(1024, rows_per_batch)
    off = col_off // tn
    assert col_off % tn == 0 and n_cols % tn == 0
    in_specs = [
        pl.BlockSpec((tm, d), lambda i, j: (i, 0)),
        _mod_spec(shift, tm, d, rows_per_batch, lambda j: 0),
        _mod_spec(scale, tm, d, rows_per_batch, lambda j: 0),
        pl.BlockSpec((1, d), lambda i, j: (0, 0)),
        pl.BlockSpec((None, d, tn), lambda i, j: (slot, 0, off + j)),
    ]
    args = [x, shift, scale, g.reshape(1, d), w]
    if rope:
        cos, sin = tables
        nbt = max(cos.shape[0] // tm, 1)
        in_specs += [pl.BlockSpec((tm, LANES), lambda i, j: (i % nbt, 0))] * 2
        args += [cos, sin]
    out = pl.pallas_call(
        functools.partial(_proj_kernel, rope=rope, kinds=tuple(dtypes), tn=tn, precise=precise),
        grid=(m // tm, n_cols // tn),
        in_specs=in_specs,
        out_specs=[pl.BlockSpec((tn // LANES, tm, LANES), lambda i, j: (j, i, 0)) if dt == HEADS else
                   pl.BlockSpec((tm, tn * _WIDTH.get(dt, 1)), lambda i, j: (i, j)) for dt in dtypes],
        out_shape=[jax.ShapeDtypeStruct((n_cols // LANES, m, LANES), BF16) if dt == HEADS else
                   jax.ShapeDtypeStruct((m, n_cols * _WIDTH.get(dt, 1)), dt if dt in (F32, BF16) else BF16)
                   for dt in dtypes],
        scratch_shapes=[pltpu.VMEM((tm, d), F32 if precise else BF16)],
        compiler_params=_cparams(("arbitrary", "arbitrary")),
        name="proj",
    )(*args)
    return out


def _outproj_kernel(a_ref, x_ref, gt_ref, w_ref, o_ref):
    y = _dot(a_ref[...], w_ref[...].astype(BF16))
    o_ref[...] = x_ref[...] + gt_ref[0] * y


def outproj(a, x, gate, w, slot, rows_per_batch):
    m, d = x.shape
    k = a.shape[1]
    tn = 512
    tm = _row_tile(1024, rows_per_batch)
    return pl.pallas_call(
        _outproj_kernel,
        grid=(m // tm, d // tn),
        in_specs=[pl.BlockSpec((tm, k), lambda i, j: (i, 0)),
                  pl.BlockSpec((tm, tn), lambda i, j: (i, j)),
                  _mod_spec(gate, tm, tn, rows_per_batch, lambda j: j),
                  pl.BlockSpec((None, k, tn), lambda i, j: (slot, 0, j))],
        out_specs=pl.BlockSpec((tm, tn), lambda i, j: (i, j)),
        out_shape=jax.ShapeDtypeStruct((m, d), F32),
        compiler_params=_cparams(("arbitrary", "arbitrary")),
        name="outproj",
    )(a, x, gate, w)


def _mlstm_kernel(q_ref, k_ref, v_ref, o_ref, gates_ref, bias_ref, ng_ref,
                  hh_ref, c_ref, n_ref, m_ref, *, heads, dk, dv, chunk):
    c_idx = pl.program_id(1)

    @pl.when(c_idx == 0)
    def _():
        c_ref[...] = jnp.zeros_like(c_ref)
        n_ref[...] = jnp.zeros_like(n_ref)
        m_ref[...] = jnp.zeros_like(m_ref)

    L = chunk
    g = gates_ref[...] + bias_ref[...]
    gt = g.T
    lf = _log_sigmoid(g)
    lft = _log_sigmoid(gt)
    row = lax.broadcasted_iota(I32, (L, L), 0)
    col = lax.broadcasted_iota(I32, (L, L), 1)
    causal = col <= row
    qscale = dk ** -0.5
    for h in range(heads):
        icol = g[:, h:h + 1]
        fcol = lf[:, heads + h:heads + h + 1]
        irow = gt[h:h + 1, :]
        frow = lft[heads + h:heads + h + 1, :]
        b_col = jnp.sum(jnp.where(causal, frow, 0.0), axis=1, keepdims=True)
        b_row = jnp.sum(jnp.where(row <= col, fcol, 0.0), axis=0, keepdims=True)
        dmat = jnp.where(causal, b_col - b_row + irow, NEG)
        m_prev = m_ref[0, h:h + 1, 0:1]
        inter = b_col + m_prev
        m_t = jnp.maximum(inter, jnp.max(dmat, axis=1, keepdims=True))
        w_intra = jnp.exp(dmat - m_t)
        w_inter = jnp.exp(inter - m_t)
        qf = q_ref[:, h * dk:(h + 1) * dk] * qscale
        kf = k_ref[:, h * dk:(h + 1) * dk]
        vf = v_ref[:, h * dv:(h + 1) * dv]
        s = _dot3(qf, kf, _dot_nt) * w_intra
        c_prev = c_ref[0, h]
        n_prev = n_ref[0, h:h + 1, :]
        num = _dot3(s, vf) + w_inter * _dot3(qf, c_prev, _dot_nt)
        den = jnp.sum(s, axis=1, keepdims=True) + w_inter * jnp.sum(qf * n_prev, axis=1, keepdims=True)
        hh = num / jnp.maximum(jnp.abs(den), jnp.exp(-m_t))
        m_new = m_t[L - 1:L, :]
        b_last = b_col[L - 1:L, :]
        w_s = jnp.exp(b_last - b_col + icol - m_new)
        decay = jnp.exp(b_last + m_prev - m_new)
        vw_t = (vf * w_s).T
        c_ref[0, h] = decay * c_prev + _dot3(vw_t, kf)
        n_ref[0, h:h + 1, :] = decay * n_prev + jnp.sum(kf * w_s, axis=0, keepdims=True)
        m_ref[0, h:h + 1, :] = jnp.broadcast_to(m_new, (1, LANES))
        hn = hh * lax.rsqrt(jnp.mean(hh * hh, -1, keepdims=True) + EPS) * ng_ref[:, h * dv:(h + 1) * dv]
        hn = hn * jax.nn.sigmoid(o_ref[:, h * dv:(h + 1) * dv])
        hh_ref[:, h * dv:(h + 1) * dv] = hn.astype(BF16)


def mlstm_prompt(y, gates, b_if, norm_g, bn, t):
    heads = A_HEADS
    dv = norm_g.shape[-1]
    dk = dv // 2
    qk = heads * dk
    vw = heads * dv
    L = A_CHUNK
    nc = t // L
    bias = jnp.zeros((1, LANES), F32).at[0, :2 * heads].set(b_if)
    row = lambda b, c: b * nc + c
    outs = pl.pallas_call(
        functools.partial(_mlstm_kernel, heads=heads, dk=dk, dv=dv, chunk=L),
        grid=(bn, nc),
        in_specs=[pl.BlockSpec((L, qk), lambda b, c: (row(b, c), 0)),
                  pl.BlockSpec((L, qk), lambda b, c: (row(b, c), 1)),
                  pl.BlockSpec((L, vw), lambda b, c: (row(b, c), 1)),
                  pl.BlockSpec((L, vw), lambda b, c: (row(b, c), 2)),
                  pl.BlockSpec((L, LANES), lambda b, c: (row(b, c), 0)),
                  pl.BlockSpec((1, LANES), lambda b, c: (0, 0)),
                  pl.BlockSpec((1, vw), lambda b, c: (0, 0))],
        out_specs=[pl.BlockSpec((L, vw), lambda b, c: (row(b, c), 0)),
                   pl.BlockSpec((1, heads, dv, dk), lambda b, c: (b, 0, 0, 0)),
                   pl.BlockSpec((1, heads, dk), lambda b, c: (b, 0, 0)),
                   pl.BlockSpec((1, heads, LANES), lambda b, c: (b, 0, 0))],
        out_shape=[jax.ShapeDtypeStruct((bn * t, vw), BF16),
                   jax.ShapeDtypeStruct((bn, heads, dv, dk), F32),
                   jax.ShapeDtypeStruct((bn, heads, dk), F32),
                   jax.ShapeDtypeStruct((bn, heads, LANES), F32)],
        compiler_params=_cparams(("arbitrary", "arbitrary")),
        name="mlstm_scan",
    )(y, y, y, y, gates, bias, norm_g.reshape(1, vw))
    hh, c_fin, n_fin, m_fin = outs
    return hh, c_fin, n_fin, m_fin[:, :, 0]


def _mlstm_step_kernel(q_ref, k_ref, v_ref, o_ref, ig_ref, fg_ref, c0_ref, n0_ref, m0_ref, ng_ref,
                       h_ref, c_ref, n_ref, m_ref, *, dk):
    cst = c0_ref[0]
    n0 = n0_ref[0]
    m0 = m0_ref[0]
    q = q_ref[0] * dk ** -0.5
    k = k_ref[0]
    v = v_ref[0]
    ig = ig_ref[0]
    fc = _log_sigmoid(fg_ref[0])
    inter = fc + m0
    m_t = jnp.maximum(inter, ig)
    w_intra = jnp.exp(ig - m_t)
    w_inter = jnp.exp(inter - m_t)
    s = jnp.sum(q * k, axis=-1, keepdims=True) * w_intra
    cq = jnp.sum(cst * q, axis=-1, keepdims=True)
    num = s * v + w_inter * cq
    den = s + w_inter * jnp.sum(n0 * q, axis=-1, keepdims=True)
    hh = num / jnp.maximum(jnp.abs(den), jnp.exp(-m_t))
    w_s = w_intra
    decay = jnp.exp(inter - m_t)
    c_ref[0] = decay * cst + w_s * (v * k)
    n_ref[0] = decay * n0 + w_s * k
    m_ref[0] = m_t
    hn = hh * lax.rsqrt(jnp.mean(hh * hh, axis=1, keepdims=True) + EPS) * ng_ref[...]
    h_ref[0] = hn * jax.nn.sigmoid(o_ref[0])


def mlstm_sample(y, gates, b_if, norm_g, c0, n0, m0, bs):
    heads = A_HEADS
    dv = norm_g.shape[-1]
    dk = dv // 2
    qk = heads * dk
    vw = heads * dv
    y = y[:bs]
    q = y[:, :qk].reshape(bs, heads, 1, dk)
    k = y[:, qk:2 * qk].reshape(bs, heads, 1, dk)
    v = y[:, 2 * qk:2 * qk + vw].reshape(bs, heads, dv, 1)
    o = y[:, 2 * qk + vw:].reshape(bs, heads, dv, 1)
    gb = gates[:bs, :2 * heads] + b_if
    ig = gb[:, :heads].reshape(bs, heads, 1, 1)
    fg = gb[:, heads:].reshape(bs, heads, 1, 1)
    spec = lambda *shape: pl.BlockSpec((1,) + shape, lambda b: (b,) + (0,) * len(shape))
    outs = pl.pallas_call(
        functools.partial(_mlstm_step_kernel, dk=dk),
        grid=(bs,),
        in_specs=[spec(heads, 1, dk), spec(heads, 1, dk), spec(heads, dv, 1), spec(heads, dv, 1),
                  spec(heads, 1, 1), spec(heads, 1, 1),
                  spec(heads, dv, dk), spec(heads, 1, dk), spec(heads, 1, 1),
                  pl.BlockSpec((heads, dv, 1), lambda b: (0, 0, 0))],
        out_specs=[spec(heads, dv, 1), spec(heads, dv, dk), spec(heads, 1, dk), spec(heads, 1, 1)],
        out_shape=[jax.ShapeDtypeStruct((bs, heads, dv, 1), F32),
                   jax.ShapeDtypeStruct((bs, heads, dv, dk), F32),
                   jax.ShapeDtypeStruct((bs, heads, 1, dk), F32),
                   jax.ShapeDtypeStruct((bs, heads, 1, 1), F32)],
        compiler_params=_cparams(("arbitrary",)),
        name="mlstm_step",
    )(q, k, v, o, ig, fg, c0, n0.reshape(bs, heads, 1, dk), m0.reshape(bs, heads, 1, 1),
      norm_g.reshape(heads, dv, 1))
    hh, c_new, n_new, m_new = outs
    return hh.reshape(bs, vw), c_new, n_new.reshape(bs, heads, dk), m_new.reshape(bs, heads)


def _diff_lambda(lam_ref, lam_init):
    l = lam_ref[...]
    a = jnp.sum(l[0:1, :] * l[1:2, :], axis=1, keepdims=True)
    b = jnp.sum(l[2:3, :] * l[3:4, :], axis=1, keepdims=True)
    return jnp.exp(a) - jnp.exp(b) + lam_init


def _flash_step(s, vs, m_scr, l_scr, acc_scr, c):
    m_prev = m_scr[c]
    m_new = jnp.maximum(m_prev, jnp.max(s, axis=1, keepdims=True))
    alpha = jnp.exp(m_prev - m_new)
    p = jnp.exp(s - m_new)
    l_scr[c] = alpha * l_scr[c] + jnp.sum(p, axis=1, keepdims=True)
    acc_scr[c] = alpha * acc_scr[c] + _dot(p.astype(BF16), vs)
    m_scr[c] = m_new


def _diff_prompt_kernel(q_ref, k_ref, v_ref, lam_ref, ng_ref, o_ref, m_scr, l_scr, acc_scr,
                        *, tq, tk, dk, lam_init):
    i = pl.program_id(2)
    scale = dk ** -0.5
    m_scr[...] = jnp.full_like(m_scr, NEG)
    l_scr[...] = jnp.zeros_like(l_scr)
    acc_scr[...] = jnp.zeros_like(acc_scr)

    def chunk(j, masked):
        start = pl.multiple_of(j * tk, tk)
        vs = v_ref[pl.ds(start, tk), :]
        for c in range(2):
            s = _dot_nt(q_ref[:, c * dk:(c + 1) * dk], k_ref[pl.ds(start, tk), c * dk:(c + 1) * dk]) * scale
            if masked:
                row = i * tq + lax.broadcasted_iota(I32, (tq, tk), 0)
                col = start + lax.broadcasted_iota(I32, (tq, tk), 1)
                s = jnp.where(col <= row, s, NEG)
            _flash_step(s, vs, m_scr, l_scr, acc_scr, c)

    def body(j, carry):
        chunk(j, False)
        return carry

    n_full = (i * tq) // tk
    lax.fori_loop(0, n_full, body, 0)
    chunk(n_full, True)
    lam = _diff_lambda(lam_ref, lam_init)
    o = acc_scr[0] / l_scr[0] - lam * (acc_scr[1] / l_scr[1])
    o = o * lax.rsqrt(jnp.mean(o * o, -1, keepdims=True) + EPS) * ng_ref[...] * (1.0 - lam_init)
    o_ref[...] = o.astype(o_ref.dtype)


def diff_prompt(qb, kb, vb, lam, norm_g, lam_init, bn, t):
    heads = B_HEADS
    dv = norm_g.shape[-1]
    dk = dv // 2
    tq = min(256, t)
    tk = min(1024, t)
    nq = t // tq
    return pl.pallas_call(
        functools.partial(_diff_prompt_kernel, tq=tq, tk=tk, dk=dk, lam_init=lam_init),
        grid=(bn, heads, nq),
        in_specs=[pl.BlockSpec((tq, 2 * dk), lambda b, h, i: (b * nq + i, h)),
                  pl.BlockSpec((t, 2 * dk), lambda b, h, i: (b, h)),
                  pl.BlockSpec((t, dv), lambda b, h, i: (b, h)),
                  pl.BlockSpec(lam.shape, lambda b, h, i: (0, 0)),
                  pl.BlockSpec((1, dv), lambda b, h, i: (0, h))],
        out_specs=pl.BlockSpec((tq, dv), lambda b, h, i: (b * nq + i, h)),
        out_shape=jax.ShapeDtypeStruct((bn * t, heads * dv), BF16),
        scratch_shapes=[pltpu.VMEM((2, tq, 1), F32), pltpu.VMEM((2, tq, 1), F32),
                        pltpu.VMEM((2, tq, dv), F32)],
        compiler_params=_cparams(("arbitrary", "arbitrary", "arbitrary")),
        name="diff_prompt",
    )(qb, kb, vb, lam, norm_g.reshape(1, heads * dv))


def _paged_kernel(pt_ref, q_ref, kn_ref, vn_ref, *rest, mode, group, n_steps, pps, lam_init):
    k_refs, v_refs, rest = rest[:pps], rest[pps:2 * pps], rest[2 * pps:]
    if mode == "diff":
        lam_ref, ng_ref, o_ref, m_scr, l_scr, acc_scr = rest
    else:
        bias_ref, biasn_ref, o_ref, exp_scr, m_scr, l_scr, acc_scr = rest
    p = pl.program_id(1)
    rows, dk = q_ref.shape[1:]
    ncol = PAGE_SIZE * group
    scale = dk ** -0.5

    @pl.when(p == 0)
    def _():
        m_scr[...] = jnp.full_like(m_scr, NEG)
        l_scr[...] = jnp.zeros_like(l_scr)
        acc_scr[...] = jnp.zeros_like(acc_scr)
        if mode == "dsa":
            t = lax.broadcasted_iota(I32, (PAGE_SIZE, ncol), 0)
            c = lax.broadcasted_iota(I32, (PAGE_SIZE, ncol), 1)
            exp_scr[...] = jnp.where(c // group == t, 1.0, 0.0).astype(BF16)

    q = q_ref[0]
    qb = q.astype(BF16)

    def page_scores(k_ref):
        if mode == "diff":
            half = rows // 2
            return jnp.concatenate(
                [_dot_nt(qb[c * half:(c + 1) * half], k_ref[pl.ds(c, ncol, stride=2), :].astype(BF16))
                 for c in range(2)], axis=0)
        return _dot_nt(qb, k_ref[...].astype(BF16))

    s = jnp.concatenate([page_scores(k_ref) for k_ref in k_refs], axis=1)
    r_i = lax.broadcasted_iota(I32, s.shape, 0)
    c_i = lax.broadcasted_iota(I32, s.shape, 1)
    keep = (c_i % group) == (r_i % group)
    if mode == "dsa":
        sel = jnp.where(bias_ref[0, 0] == 0.0, 1.0, 0.0).astype(BF16)
        sel = jnp.concatenate(
            [_dot(jnp.broadcast_to(sel[pp:pp + 1], (8, PAGE_SIZE)), exp_scr[...])[0:1] for pp in range(pps)],
            axis=1)
        keep = keep & (sel > 0.5)
    s = jnp.where(keep, s * scale, NEG)
    m_prev = m_scr[...]
    m_new = jnp.maximum(m_prev, jnp.max(s, axis=1, keepdims=True))
    alpha = jnp.exp(m_prev - m_new)
    pe = jnp.exp(s - m_new)
    l_scr[...] = alpha * l_scr[...] + jnp.sum(pe, axis=1, keepdims=True)
    pv = _dot(pe[:, :ncol].astype(BF16), v_refs[0][...].astype(BF16))
    for pp in range(1, pps):
        pv = pv + _dot(pe[:, pp * ncol:(pp + 1) * ncol].astype(BF16), v_refs[pp][...].astype(BF16))
    acc_scr[...] = alpha * acc_scr[...] + pv
    m_scr[...] = m_new

    @pl.when(p == n_steps - 1)
    def _():
        s_n = jnp.sum(q * kn_ref[0], axis=1, keepdims=True) * scale
        v_n = vn_ref[0]
        if mode == "dsa":
            s_n = s_n + biasn_ref[0][:, 0:1]
        else:
            v_n = jnp.concatenate([v_n, v_n], axis=0)
        m_prev = m_scr[...]
        m_new = jnp.maximum(m_prev, s_n)
        alpha = jnp.exp(m_prev - m_new)
        pn = jnp.exp(s_n - m_new)
        l_fin = alpha * l_scr[...] + pn
        acc = (alpha * acc_scr[...] + pn * v_n) / l_fin
        if mode == "diff":
            half = rows // 2
            o = acc[:half] - _diff_lambda(lam_ref, lam_init) * acc[half:]
            o = o * lax.rsqrt(jnp.mean(o * o, -1, keepdims=True) + EPS) * ng_ref[...]
            o_ref[0] = o * (1.0 - lam_init)
        else:
            o_ref[0] = acc


def paged_attention(mode, q, k_new, v_new, cache_k, cache_v, slot, page_table, extra, lam_init=0.0):
    bs, n_pages = page_table.shape
    rows, dk = q.shape[1:]
    hv, dv = v_new.shape[1:]
    group = hv
    ncol = PAGE_SIZE * group
    pps = max(c for c in (4, 2, 1) if n_pages % c == 0)
    n_steps = n_pages // pps
    seq_spec = lambda a: pl.BlockSpec((1,) + a.shape[1:], lambda b, p, pt: (b, 0, 0))

    def page_specs(a):
        return [pl.BlockSpec((None, None) + a.shape[2:],
                             lambda b, p, pt, pp=pp: (slot, pt[b, p * pps + pp], 0, 0)) for pp in range(pps)]

    scratch = [pltpu.VMEM((rows, 1), F32), pltpu.VMEM((rows, 1), F32), pltpu.VMEM((rows, dv), F32)]
    if mode == "diff":
        lam, norm_g = extra
        extra_specs = [pl.BlockSpec(lam.shape, lambda b, p, pt: (0, 0)),
                       pl.BlockSpec(norm_g.shape, lambda b, p, pt: (0, 0))]
        extra_args = [lam, norm_g]
    else:
        bias, bias_new = extra
        bias = bias.reshape(bs, n_steps, pps, PAGE_SIZE)
        extra_specs = [pl.BlockSpec((1, 1, pps, PAGE_SIZE), lambda b, p, pt: (b, p, 0, 0)),
                       pl.BlockSpec((1, 1, LANES), lambda b, p, pt: (b, 0, 0))]
        extra_args = [bias, bias_new]
        scratch = [pltpu.VMEM((PAGE_SIZE, ncol), BF16)] + scratch
    grid_spec = pltpu.PrefetchScalarGridSpec(
        num_scalar_prefetch=1,
        grid=(bs, n_steps),
        in_specs=[seq_spec(q), seq_spec(k_new), seq_spec(v_new)] + page_specs(cache_k)
        + page_specs(cache_v) + extra_specs,
        out_specs=pl.BlockSpec((1, hv, dv), lambda b, p, pt: (b, 0, 0)),
        scratch_shapes=scratch,
    )
    return pl.pallas_call(
        functools.partial(_paged_kernel, mode=mode, group=group, n_steps=n_steps, pps=pps,
                          lam_init=lam_init),
        grid_spec=grid_spec,
        out_shape=jax.ShapeDtypeStruct((bs, hv, dv), F32),
        compiler_params=_cparams(("arbitrary", "arbitrary")),
        name="paged_" + mode,
    )(page_table, q, k_new, v_new, *([cache_k] * pps), *([cache_v] * pps), *extra_args)


def _sortable_key(s):
    bits = lax.bitcast_convert_type(s, I32)
    return bits ^ ((bits >> 31) & 0x7FFFFFFF)


def _bisect_threshold(count_ge, shape, n_top):
    def step(it, t_u):
        bit = 31 - it
        cand_u = t_u | jnp.left_shift(jnp.int32(1), bit)
        cnt = count_ge(cand_u ^ INT_MIN)
        return jnp.where(cnt >= n_top, cand_u, t_u)

    t_u = lax.fori_loop(0, 32, step, jnp.zeros(shape, I32))
    return jnp.maximum(t_u ^ INT_MIN, INT_MIN + 1)


def _dsa_prompt_kernel(qi_ref, wt_ref, ki_ref, q_ref, k_ref, v_ref, o_ref,
                       key_scr, bias_scr, wb_scr, m_scr, l_scr, acc_scr, *, tq, tk, dh, n_top, wcol):
    i = pl.program_id(1)
    h = pl.program_id(2)
    scale = dh ** -0.5
    idx_scale = IDX_DIM ** -0.5 * IDX_HEADS ** -0.5
    groups = tq // LANES
    n_att = (i * tq) // tk + 1

    @pl.when(h == 0)
    def _():
        w = wt_ref[...] * idx_scale
        for hh in range(IDX_HEADS):
            wb_scr[hh] = jnp.broadcast_to(w[:, wcol + hh:wcol + hh + 1], (tq, LANES))

        def score_chunk(j, masked):
            start = pl.multiple_of(j * tq, tq)
            kc = ki_ref[pl.ds(start, tq), :]
            acc = [jnp.zeros((tq, LANES), F32) for _ in range(groups)]
            for hh in range(IDX_HEADS):
                s = _dot_nt(qi_ref[:, hh * IDX_PACK:(hh + 1) * IDX_PACK], kc)
                wb = wb_scr[hh]
                for c in range(groups):
                    acc[c] = acc[c] + jnp.maximum(s[:, c * LANES:(c + 1) * LANES], 0.0) * wb
            for c in range(groups):
                key = _sortable_key(acc[c])
                if masked:
                    row = lax.broadcasted_iota(I32, (tq, LANES), 0)
                    col = lax.broadcasted_iota(I32, (tq, LANES), 1) + c * LANES
                    key = jnp.where(col <= row, key, INT_MIN)
                key_scr[:, pl.ds(pl.multiple_of(start + c * LANES, LANES), LANES)] = key

        def score_body(j, carry):
            score_chunk(j, False)
            return carry

        lax.fori_loop(0, i, score_body, 0)
        score_chunk(i, True)

        for rb in range(tq // BISECT_ROWS):
            rows = pl.ds(rb * BISECT_ROWS, BISECT_ROWS)

            def count_ge(cand, rows=rows):
                cand_b = jnp.broadcast_to(cand, (BISECT_ROWS, LANES))

                def cbody(j, part):
                    start = pl.multiple_of(j * tq, tq)
                    for c in range(groups):
                        kk = key_scr[rows, pl.ds(start + c * LANES, LANES)]
                        part = part + jnp.where(kk >= cand_b, 1.0, 0.0)
                    return part

                part = lax.fori_loop(0, i + 1, cbody, jnp.zeros((BISECT_ROWS, LANES), F32))
                return jnp.sum(part, axis=1, keepdims=True)

            thr = _bisect_threshold(count_ge, (BISECT_ROWS, 1), n_top)
            thr = jnp.broadcast_to(thr, (BISECT_ROWS, LANES))

            def bias_body(j, carry, rows=rows, thr=thr):
                start = pl.multiple_of(j * LANES, LANES)
                kk = key_scr[rows, pl.ds(start, LANES)]
                bias_scr[rows, pl.ds(start, LANES)] = jnp.where(kk >= thr, 0.0, NEG)
                return carry

            lax.fori_loop(0, (i + 1) * groups, bias_body, 0)

        def fill_body(j, carry):
            start = pl.multiple_of(j * LANES, LANES)
            bias_scr[:, pl.ds(start, LANES)] = jnp.full((tq, LANES), NEG, F32)
            return carry

        lax.fori_loop((i + 1) * groups, n_att * (tk // LANES), fill_body, 0)

    m_scr[...] = jnp.full_like(m_scr, NEG)
    l_scr[...] = jnp.zeros_like(l_scr)
    acc_scr[...] = jnp.zeros_like(acc_scr)
    qh = q_ref[...]

    def att_body(j, carry):
        start = pl.multiple_of(j * tk, tk)
        s = _dot_nt(qh, k_ref[pl.ds(start, tk), :]) * scale + bias_scr[:, pl.ds(start, tk)]
        _flash_step(s, v_ref[pl.ds(start, tk), :], m_scr, l_scr, acc_scr, 0)
        return carry

    lax.fori_loop(0, n_att, att_body, 0)
    o_ref[...] = (acc_scr[0] / l_scr[0]).astype(o_ref.dtype)


def dsa_prompt(qi3, tail, ki3, qb, kb, vb, bn, t, wcol):
    heads, _, dh = qb.shape
    tq = min(256, t)
    tk = min(1024, t)
    nq = t // tq
    n_top = min(TOPK_MAX, t // 4)
    return pl.pallas_call(
        functools.partial(_dsa_prompt_kernel, tq=tq, tk=tk, dh=dh, n_top=n_top, wcol=wcol),
        grid=(bn, nq, heads),
        in_specs=[pl.BlockSpec((tq, IDX_HEADS * IDX_PACK), lambda b, i, h: (b * nq + i, 0)),
                  pl.BlockSpec((tq, LANES), lambda b, i, h: (b * nq + i, 0)),
                  pl.BlockSpec((t, IDX_PACK), lambda b, i, h: (b, 0)),
                  pl.BlockSpec((None, tq, dh), lambda b, i, h: (h, b * nq + i, 0)),
                  pl.BlockSpec((None, t, dh), lambda b, i, h: (h, b, 0)),
                  pl.BlockSpec((None, t, dh), lambda b, i, h: (h, b, 0))],
        out_specs=pl.BlockSpec((tq, dh), lambda b, i, h: (b * nq + i, h)),
        out_shape=jax.ShapeDtypeStruct((bn * t, heads * dh), BF16),
        scratch_shapes=[pltpu.VMEM((tq, t), I32), pltpu.VMEM((tq, t), F32),
                        pltpu.VMEM((IDX_HEADS, tq, LANES), F32),
                        pltpu.VMEM((1, tq, 1), F32), pltpu.VMEM((1, tq, 1), F32),
                        pltpu.VMEM((1, tq, dh), F32)],
        compiler_params=_cparams(("arbitrary", "arbitrary", "arbitrary")),
        name="dsa_prompt",
    )(qi3, tail, ki3, qb, kb, vb)


def _dsa_index_kernel(pt_ref, qi_ref, w_ref, kin_ref, *rest, n_steps, pps, n_top):
    kp_refs, (bias_ref, biasn_ref, sc_scr) = rest[:pps], rest[pps:]
    p = pl.program_id(1)
    idx_scale = IDX_DIM ** -0.5 * IDX_HEADS ** -0.5
    qi = qi_ref[0]
    w = w_ref[0] * idx_scale
    for pp, kp_ref in enumerate(kp_refs):
        s = _dot3(qi, kp_ref[...], _dot_nt)
        sc_scr[pl.ds(p * pps + pp, 1), :] = jnp.sum(jnp.maximum(s, 0.0) * w, axis=0, keepdims=True)

    @pl.when(p == n_steps - 1)
    def _():
        s_n = jnp.sum(qi * kin_ref[0], axis=1, keepdims=True)
        sc_new = jnp.sum(jnp.maximum(s_n, 0.0) * w, axis=0, keepdims=True)
        key = _sortable_key(sc_scr[...])
        key_new = _sortable_key(sc_new)

        def count_ge(cand):
            hit = jnp.where(key >= cand, 1.0, 0.0)
            cnt = jnp.sum(jnp.sum(hit, axis=1, keepdims=True), axis=0, keepdims=True)
            return cnt + jnp.where(key_new >= cand, 1.0, 0.0)

        thr = _bisect_threshold(count_ge, (1, 1), n_top)
        bias_ref[0] = jnp.where(key >= thr, 0.0, NEG)
        biasn_ref[0] = jnp.broadcast_to(jnp.where(key_new >= thr, 0.0, NEG), (1, LANES))


def dsa_sample_index(qi, w, ki_new, cache_ki, slot, page_table):
    bs, n_pages = page_table.shape
    n_top = min(TOPK_MAX, (n_pages * PAGE_SIZE + 1) // 4)
    pps = max(c for c in (16, 8, 4, 2, 1) if n_pages % c == 0)
    n_steps = n_pages // pps
    grid_spec = pltpu.PrefetchScalarGridSpec(
        num_scalar_prefetch=1,
        grid=(bs, n_steps),
        in_specs=[pl.BlockSpec((1, IDX_HEADS, IDX_DIM), lambda b, p, pt: (b, 0, 0)),
                  pl.BlockSpec((1, IDX_HEADS, 1), lambda b, p, pt: (b, 0, 0)),
                  pl.BlockSpec((1, 1, IDX_DIM), lambda b, p, pt: (b, 0, 0))]
        + [pl.BlockSpec((None, None, PAGE_SIZE, IDX_DIM),
                        lambda b, p, pt, pp=pp: (slot, pt[b, p * pps + pp], 0, 0)) for pp in range(pps)],
        out_specs=[pl.BlockSpec((1, n_pages, PAGE_SIZE), lambda b, p, pt: (b, 0, 0)),
                   pl.BlockSpec((1, 1, LANES), lambda b, p, pt: (b, 0, 0))],
        scratch_shapes=[pltpu.VMEM((n_pages, PAGE_SIZE), F32)],
    )
    bias, bias_new = pl.pallas_call(
        functools.partial(_dsa_index_kernel, n_steps=n_steps, pps=pps, n_top=n_top),
        grid_spec=grid_spec,
        out_shape=[jax.ShapeDtypeStruct((bs, n_pages, PAGE_SIZE), F32),
                   jax.ShapeDtypeStruct((bs, 1, LANES), F32)],
        compiler_params=_cparams(("arbitrary", "arbitrary")),
        name="dsa_index",
    )(page_table, qi, w, ki_new, *([cache_ki] * pps))
    return bias, bias_new


def _rope_tables(pos, half):
    inv = ROPE_THETA ** (-jnp.arange(half, dtype=F32) / half)
    ang = pos.astype(F32)[:, None] * inv[None, :]
    return jnp.cos(ang), jnp.sin(ang)


def _tables_128(pos):
    c, s = _rope_tables(pos, 64)
    return jnp.concatenate([c, c], -1), jnp.concatenate([-s, s], -1)


def _tables_64(pos, tail):
    c, s = _rope_tables(pos, 32)
    if tail:
        one, zero = jnp.ones_like(c), jnp.zeros_like(s)
        return jnp.concatenate([c, c, one, one], -1), jnp.concatenate([-s, s, zero, zero], -1)
    return jnp.concatenate([c, c, c, c], -1), jnp.concatenate([-s, s, -s, s], -1)


def _pad_cols(w, n):
    return jnp.pad(w, ((0, 0), (0, 0), (0, n - w.shape[-1])))


def kernel(x_prompt, x_sample, c_prompt, c_sample, state_a_C, state_a_n, state_a_m, cache_b_k,
           cache_b_v, cache_c_k, cache_c_v, cache_c_kidx, page_table, ada_w, ada_b, norm_g, ffn_wi,
           ffn_wo, a_w_in, a_b_if, a_norm_g, a_w_out, b_w_in, b_lambda, b_norm_g, b_w_out, c_w_in,
           c_w_out, final_g):
    bp, t_p, d = x_prompt.shape
    bs = x_sample.shape[0]
    depth = ada_w.shape[0]
    n_pages = page_table.shape[1]
    past = n_pages * PAGE_SIZE
    n_pool = cache_b_k.shape[1]
    rs = SUBLANES_BF16 * ((bs + SUBLANES_BF16 - 1) // SUBLANES_BF16)
    mp = bp * t_p

    ff = ffn_wo.shape[2]
    fp = FFN_TF * ((ff + FFN_TF - 1) // FFN_TF)
    wi_p, wo_p = stage_ffn_weights(ffn_wi, ffn_wo, fp)

    c_rows = jnp.zeros((rs, d), F32).at[:bp].set(c_prompt).at[bp:bp + bs].set(c_sample)
    mod = ada_all(c_rows, ada_w, ada_b).reshape(depth, rs, N_ADA, d)

    xp = x_prompt.reshape(mp, d)
    xs = jnp.zeros((rs, d), F32).at[:bs].set(x_sample.reshape(bs, d))

    pos_p = jnp.arange(t_p, dtype=jnp.int32)
    pos_s = jnp.full((rs,), past, jnp.int32)
    tab128_p, tab128_s = _tables_128(pos_p), _tables_128(pos_s)
    tab64_p, tab64_s = _tables_64(pos_p, False), _tables_64(pos_s, False)
    tabt_p, tabt_s = _tables_64(pos_p, True), _tables_64(pos_s, True)

    a_main = 2 * A_HEADS * (a_norm_g.shape[-1] // 2) + 2 * A_HEADS * a_norm_g.shape[-1]
    a_tail_w = _pad_cols(a_w_in[:, :, a_main:], LANES)
    c_main = 3 * d + IDX_HEADS * IDX_DIM
    c_tail_w = _pad_cols(c_w_in[:, :, c_main:], LANES)

    kind_of = [i % 3 for i in range(depth)]
    outs = {k: [] for k in ("aCp", "anp", "amp", "aCs", "ans", "ams", "bkp", "bvp", "bks", "bvs",
                            "ckp", "cvp", "cip", "cks", "cvs", "cis")}
    for l in range(depth):
        kind = kind_of[l]
        j = kind_of[:l].count(kind)
        mp_l = [mod[l, :bp, k][:, None, :] for k in range(N_ADA)]
        ms_l = [mod[l, bp:bp + bs, k] for k in range(N_ADA)]
        ms_l = [jnp.zeros((1, rs, d), F32).at[0, :bs].set(m) for m in ms_l]

        xp = ffn_sublayer(xp, mp_l[0], mp_l[1], mp_l[2], norm_g[l, 0], wi_p, wo_p, l, 0, t_p)
        xs = ffn_sublayer(xs, ms_l[0], ms_l[1], ms_l[2], norm_g[l, 0], wi_p, wo_p, l, 0, rs)
        pj_p = functools.partial(proj, xp, mp_l[3], mp_l[4], norm_g[l, 1], rows_per_batch=t_p)
        pj_s = functools.partial(proj, xs, ms_l[3], ms_l[4], norm_g[l, 1], rows_per_batch=rs)

        if kind == 0:
            (y_p,) = pj_p(a_w_in, j, 0, a_main, dtypes=(F32,))
            (g_p,) = pj_p(a_tail_w, j, 0, LANES, dtypes=(F32,), precise=True)
            hh_p, c_p, n_p, m_p = mlstm_prompt(y_p, g_p, a_b_if[j], a_norm_g[j], bp, t_p)
            (y_s,) = pj_s(a_w_in, j, 0, a_main, dtypes=(F32,))
            (g_s,) = pj_s(a_tail_w, j, 0, LANES, dtypes=(F32,), precise=True)
            hh_s, c_s, n_s, m_s = mlstm_sample(y_s, g_s, a_b_if[j], a_norm_g[j],
                                               state_a_C[j], state_a_n[j], state_a_m[j], bs)
            op_in = hh_p
            os_in = jnp.zeros((rs, hh_s.shape[1]), BF16).at[:bs].set(hh_s.astype(BF16))
            w_out = a_w_out
            outs["aCp"].append(c_p); outs["anp"].append(n_p); outs["amp"].append(m_p)
            outs["aCs"].append(c_s); outs["ans"].append(n_s); outs["ams"].append(m_s)
        elif kind == 1:
            lam_init = 0.8 - 0.6 * math.exp(-0.3 * l)
            bqk = b_w_in.shape[2] // 3
            (q_pb,) = pj_p(b_w_in, j, 0, bqk, dtypes=(BF16,), rope=128, tables=tab128_p)
            k_p, k_pb = pj_p(b_w_in, j, bqk, bqk, dtypes=(F32, BF16), rope=128, tables=tab128_p)
            v_p, v_pb = pj_p(b_w_in, j, 2 * bqk, bqk, dtypes=(F32, BF16))
            op_in = diff_prompt(q_pb, k_pb, v_pb, b_lambda[j], b_norm_g[j], lam_init, bp, t_p)
            (q_s,) = pj_s(b_w_in, j, 0, bqk, dtypes=(F32,), rope=128, tables=tab128_s)
            (k_s,) = pj_s(b_w_in, j, bqk, bqk, dtypes=(F32,), rope=128, tables=tab128_s)
            (v_s,) = pj_s(b_w_in, j, 2 * bqk, bqk, dtypes=(F32,))
            comp_major = lambda a: a[:bs].reshape(bs, B_HEADS, 2, -1).transpose(0, 2, 1, 3).reshape(
                bs, 2 * B_HEADS, -1)
            o_s = paged_attention(
                "diff", comp_major(q_s), comp_major(k_s), v_s[:bs].reshape(bs, B_HEADS, -1),
                cache_b_k.reshape(cache_b_k.shape[0], n_pool, PAGE_SIZE * B_HEADS * 2, -1),
                cache_b_v.reshape(cache_b_v.shape[0], n_pool, PAGE_SIZE * B_HEADS, -1),
                j, page_table, (b_lambda[j], b_norm_g[j]), lam_init)
            os_in = jnp.zeros((rs, bqk), BF16).at[:bs].set(o_s.reshape(bs, bqk).astype(BF16))
            w_out = b_w_out
            outs["bkp"].append(k_p.reshape(bp, t_p, B_HEADS, 2, -1))
            outs["bvp"].append(v_p.reshape(bp, t_p, B_HEADS, -1))
            outs["bks"].append(k_s[:bs].reshape(bs, 1, B_HEADS, 2, -1))
            outs["bvs"].append(v_s[:bs].reshape(bs, 1, B_HEADS, -1))
        else:
            cw = d
            ni = IDX_HEADS * IDX_DIM
            (q_pb,) = pj_p(c_w_in, j, 0, cw, dtypes=(HEADS,), rope=128, tables=tab128_p)
            k_p, k_pb = pj_p(c_w_in, j, cw, cw, dtypes=(F32, HEADS), rope=128, tables=tab128_p)
            v_p, v_pb = pj_p(c_w_in, j, 2 * cw, cw, dtypes=(F32, HEADS))
            (qi_p3,) = pj_p(c_w_in, j, 3 * cw, ni, dtypes=(Q3,), rope=64, tables=tab64_p)
            t_pf, ki_p3 = pj_p(c_tail_w, j, 0, LANES, dtypes=(F32, K3), rope=64, tables=tabt_p)
            op_in = dsa_prompt(qi_p3, t_pf, ki_p3, q_pb, k_pb, v_pb, bp, t_p, IDX_DIM)
            (q_s,) = pj_s(c_w_in, j, 0, cw, dtypes=(F32,), rope=128, tables=tab128_s)
            (k_s,) = pj_s(c_w_in, j, cw, cw, dtypes=(F32,), rope=128, tables=tab128_s)
            (v_s,) = pj_s(c_w_in, j, 2 * cw, cw, dtypes=(F32,))
            (qi_s,) = pj_s(c_w_in, j, 3 * cw, ni, dtypes=(F32,), rope=64, tables=tab64_s)
            (t_s,) = pj_s(c_tail_w, j, 0, LANES, dtypes=(F32,), rope=64, tables=tabt_s)
            ki_s = t_s[:bs, :IDX_DIM]
            bias, bias_new = dsa_sample_index(
                qi_s[:bs].reshape(bs, IDX_HEADS, IDX_DIM),
                t_s[:bs, IDX_DIM:IDX_DIM + IDX_HEADS].reshape(bs, IDX_HEADS, 1),
                ki_s[:, None], cache_c_kidx, j, page_table)
            o_s = paged_attention(
                "dsa", q_s[:bs].reshape(bs, C_HEADS, -1), k_s[:bs].reshape(bs, C_HEADS, -1),
                v_s[:bs].reshape(bs, C_HEADS, -1),
                cache_c_k.reshape(cache_c_k.shape[0], n_pool, PAGE_SIZE * C_HEADS, -1),
                cache_c_v.reshape(cache_c_v.shape[0], n_pool, PAGE_SIZE * C_HEADS, -1),
                j, page_table, (bias, bias_new))
            os_in = jnp.zeros((rs, cw), BF16).at[:bs].set(o_s.reshape(bs, cw).astype(BF16))
            w_out = c_w_out
            outs["ckp"].append(k_p.reshape(bp, t_p, C_HEADS, -1))
            outs["cvp"].append(v_p.reshape(bp, t_p, C_HEADS, -1))
            outs["cip"].append(t_pf[:, :IDX_DIM].reshape(bp, t_p, IDX_DIM))
            outs["cks"].append(k_s[:bs].reshape(bs, 1, C_HEADS, -1))
            outs["cvs"].append(v_s[:bs].reshape(bs, 1, C_HEADS, -1))
            outs["cis"].append(ki_s.reshape(bs, 1, IDX_DIM))

        xp = outproj(op_in, xp, mp_l[5], w_out, j, t_p)
        xs = outproj(os_in, xs, ms_l[5], w_out, j, rs)
        fin = final_g if l == depth - 1 else None
        xp = ffn_sublayer(xp, mp_l[6], mp_l[7], mp_l[8], norm_g[l, 2], wi_p, wo_p, l, 1, t_p, fin)
        xs = ffn_sublayer(xs, ms_l[6], ms_l[7], ms_l[8], norm_g[l, 2], wi_p, wo_p, l, 1, rs, fin)

    st = lambda k: jnp.stack(outs[k])
    return (xp.reshape(bp, t_p, d), xs[:bs].reshape(bs, 1, d),
            st("aCp"), st("anp"), st("amp"), st("aCs"), st("ans"), st("ams"),
            st("bkp"), st("bvp"), st("bks"), st("bvs"),
            st("ckp"), st("cvp"), st("cip"), st("cks"), st("cvs"), st("cis"))
```

```python
import functools
import math

import jax
import jax.numpy as jnp
from jax import lax
from jax.experimental import pallas as pl
from jax.experimental.pallas import tpu as pltpu

F32 = jnp.float32
BF16 = jnp.bfloat16
I32 = jnp.int32

EPS = 1e-6
ROPE_THETA = 10000.0
N_ADA = 9
TOPK_MAX = 256
PAGE_SIZE = 128
A_HEADS = 8
A_CHUNK = 128
B_HEADS = 8
C_HEADS = 16
IDX_HEADS = 16
IDX_DIM = 64
IDX_PACK = 4 * IDX_DIM

LANES = 128
SUBLANES_BF16 = 16
VMEM_LIMIT = 56 * 1024 * 1024
BISECT_ROWS = 128
NEG = -1e30
INT_MIN = -2 ** 31


def _cparams(sem):
    return pltpu.CompilerParams(dimension_semantics=sem, vmem_limit_bytes=VMEM_LIMIT)


def _dot(a, b):
    return jnp.dot(a, b, preferred_element_type=F32)


def _dot_nt(a, b):
    return lax.dot_general(a, b, (((1,), (1,)), ((), ())), preferred_element_type=F32)


def _split(x):
    hi = x.astype(BF16)
    return hi, (x - hi.astype(F32)).astype(BF16)


def _dot3(a, b, dot=_dot):
    a_hi, a_lo = _split(a)
    b_hi, b_lo = _split(b)
    return dot(a_hi, b_hi) + (dot(a_hi, b_lo) + dot(a_lo, b_hi))


def _norm_mod(x, g, scale, shift):
    y = x * lax.rsqrt(jnp.mean(x * x, -1, keepdims=True) + EPS) * g
    return y * (1.0 + scale) + shift


def _log_sigmoid(x):
    return jnp.minimum(x, 0.0) - jnp.log1p(jnp.exp(-jnp.abs(x)))


def _ada_kernel(c_ref, w_ref, b_ref, o_ref):
    c = c_ref[...]
    a = (c * jax.nn.sigmoid(c)).astype(BF16)
    o_ref[0] = _dot(a, w_ref[0].astype(BF16)) + b_ref[0]


def ada_all(c_rows, ada_w, ada_b):
    depth, d, n = ada_w.shape
    r = c_rows.shape[0]
    tn = 1024
    return pl.pallas_call(
        _ada_kernel,
        grid=(depth, n // tn),
        in_specs=[pl.BlockSpec((r, d), lambda l, j: (0, 0)),
                  pl.BlockSpec((1, d, tn), lambda l, j: (l, 0, j)),
                  pl.BlockSpec((1, 1, tn), lambda l, j: (l, 0, j))],
        out_specs=pl.BlockSpec((1, r, tn), lambda l, j: (l, 0, j)),
        out_shape=jax.ShapeDtypeStruct((depth, r, n), F32),
        compiler_params=_cparams(("arbitrary", "arbitrary")),
        name="ada_mod",
    )(c_rows, ada_w, ada_b.reshape(depth, 1, n))


def _row_tile(want, rows_per_batch):
    return min(want, rows_per_batch)


def _mod_spec(mod, tm, tn, rows_per_batch, col_of):
    r = mod.shape[1]
    bpb = max(rows_per_batch // tm, 1)
    return pl.BlockSpec((1, r, tn), lambda i, j: (i // bpb, 0, col_of(j)))


def _stage_wi_kernel(w_ref, o_ref):
    ff = w_ref.shape[-1]
    o_ref[:, :ff] = w_ref[...].astype(BF16)
    o_ref[:, ff:] = jnp.zeros((o_ref.shape[0], o_ref.shape[1] - ff), BF16)


def _stage_wo_kernel(w_ref, o_ref, *, n_blocks, last_rows):
    o_ref[...] = w_ref[...].astype(BF16)
    if last_rows < o_ref.shape[0]:

        @pl.when(pl.program_id(1) == n_blocks - 1)
        def _():
            o_ref[last_rows:, :] = jnp.zeros((o_ref.shape[0] - last_rows, o_ref.shape[1]), BF16)


def stage_ffn_weights(ffn_wi, ffn_wo, fp):
    depth, _, d, ff2 = ffn_wi.shape
    ff = ff2 // 2
    assert ff % LANES == 0 and fp % LANES == 0
    ls = depth * 2
    rb = min(128, d)
    wi_p = pl.pallas_call(
        _stage_wi_kernel,
        grid=(ls, d // rb, 2),
        in_specs=[pl.BlockSpec((None, rb, ff), lambda a, r, h: (a, r, h))],
        out_specs=pl.BlockSpec((None, rb, fp), lambda a, r, h: (a, r, h)),
        out_shape=jax.ShapeDtypeStruct((ls, d, 2 * fp), BF16),
        compiler_params=_cparams(("arbitrary", "arbitrary", "arbitrary")),
        name="stage_wi",
    )(ffn_wi.reshape(ls, d, ff2))
    wrb = max(r for r in range(SUBLANES_BF16, 1025, SUBLANES_BF16) if fp % r == 0)
    n_blocks = fp // wrb
    wo_p = pl.pallas_call(
        functools.partial(_stage_wo_kernel, n_blocks=n_blocks, last_rows=ff - (n_blocks - 1) * wrb),
        grid=(ls, n_blocks),
        in_specs=[pl.BlockSpec((None, wrb, d), lambda a, j: (a, j, 0))],
        out_specs=pl.BlockSpec((None, wrb, d), lambda a, j: (a, j, 0)),
        out_shape=jax.ShapeDtypeStruct((ls, fp, d), BF16),
        compiler_params=_cparams(("arbitrary", "arbitrary")),
        name="stage_wo",
    )(ffn_wo.reshape(ls, ff, d))
    return wi_p.reshape(depth, 2, d, 2 * fp), wo_p.reshape(depth, 2, fp, d)


def _ffn_kernel(*refs, nf, final):
    if final:
        (x_ref, sh_ref, sc_ref, gt_ref, g_ref, wg_ref, wu_ref, wo_ref, fg_ref,
         o_ref, h_scr, acc_scr) = refs
    else:
        (x_ref, sh_ref, sc_ref, gt_ref, g_ref, wg_ref, wu_ref, wo_ref,
         o_ref, h_scr, acc_scr) = refs
    f = pl.program_id(1)

    @pl.when(f == 0)
    def _():
        h_scr[...] = _norm_mod(x_ref[...], g_ref[...], sc_ref[0], sh_ref[0]).astype(BF16)
        acc_scr[...] = jnp.zeros_like(acc_scr)

    h = h_scr[...]
    a = _dot(h, wg_ref[...])
    u = _dot(h, wu_ref[...])
    act = (a * jax.nn.sigmoid(a) * u).astype(BF16)
    acc_scr[...] += _dot(act, wo_ref[...])

    @pl.when(f == nf - 1)
    def _():
        y = x_ref[...] + 0.5 * gt_ref[0] * acc_scr[...]
        if final:
            y = y * lax.rsqrt(jnp.mean(y * y, -1, keepdims=True) + EPS) * fg_ref[...]
        o_ref[...] = y


FFN_TF = 512


def ffn_sublayer(x, shift, scale, gate, g, wi_p, wo_p, l, s, rows_per_batch, final_g=None):
    m, d = x.shape
    fp = wo_p.shape[2]
    tf = FFN_TF
    nf = fp // tf
    tm = _row_tile(512, rows_per_batch)
    final = final_g is not None
    in_specs = [
        pl.BlockSpec((tm, d), lambda i, f: (i, 0)),
        _mod_spec(shift, tm, d, rows_per_batch, lambda f: 0),
        _mod_spec(scale, tm, d, rows_per_batch, lambda f: 0),
        _mod_spec(gate, tm, d, rows_per_batch, lambda f: 0),
        pl.BlockSpec((1, d), lambda i, f: (0, 0)),
        pl.BlockSpec((None, None, d, tf), lambda i, f: (l, s, 0, f)),
        pl.BlockSpec((None, None, d, tf), lambda i, f: (l, s, 0, nf + f)),
        pl.BlockSpec((None, None, tf, d), lambda i, f: (l, s, f, 0)),
    ]
    args = [x, shift, scale, gate, g.reshape(1, d), wi_p, wi_p, wo_p]
    if final:
        in_specs.append(pl.BlockSpec((1, d), lambda i, f: (0, 0)))
        args.append(final_g.reshape(1, d))
    return pl.pallas_call(
        functools.partial(_ffn_kernel, nf=nf, final=final),
        grid=(m // tm, nf),
        in_specs=in_specs,
        out_specs=pl.BlockSpec((tm, d), lambda i, f: (i, 0)),
        out_shape=jax.ShapeDtypeStruct((m, d), F32),
        scratch_shapes=[pltpu.VMEM((tm, d), BF16), pltpu.VMEM((tm, d), F32)],
        compiler_params=_cparams(("arbitrary", "arbitrary")),
        name="ffn",
    )(*args)


def _rope_tile(y, cos, sin, rope):
    if rope == 128:
        r = pltpu.roll(y, 64, 1)
    else:
        lane = lax.broadcasted_iota(I32, y.shape, 1)
        r = jnp.where((lane % 64) < 32, pltpu.roll(y, 96, 1), pltpu.roll(y, 32, 1))
    return y * cos + r * sin


LO = "bf16 residual"
HEADS = "bf16, one (rows, 128) slab per 128 output columns"
Q3 = "packed indexer query"
K3 = "packed indexer key"
_WIDTH = {Q3: 4, K3: 2}


def _emit(y, kind):
    if kind in (F32, BF16):
        return [y.astype(kind)]
    hi_b, lo_b = _split(y)
    if kind == LO:
        return [lo_b]
    hi, lo = hi_b.astype(F32), lo_b.astype(F32)
    first = lax.broadcasted_iota(I32, y.shape, 1) < 64
    if kind == K3:
        tiles = [jnp.where(first, hi, pltpu.roll(hi, 64, 1)), jnp.where(first, lo, 0.0)]
    else:
        hi_r = pltpu.roll(hi, 64, 1)
        tiles = [jnp.where(first, hi, pltpu.roll(lo, 64, 1)), jnp.where(first, hi, 0.0),
                 jnp.where(first, hi_r, lo), jnp.where(first, hi_r, 0.0)]
    return [t.astype(BF16) for t in tiles]


def _proj_kernel(*refs, rope, kinds, tn, precise):
    n_out = len(kinds)
    x_ref, sh_ref, sc_ref, g_ref, w_ref = refs[:5]
    pos = 5
    if rope:
        cos_ref, sin_ref = refs[5:7]
        pos = 7
    out_refs = refs[pos:pos + n_out]
    h_scr = refs[pos + n_out]
    j = pl.program_id(1)

    @pl.when(j == 0)
    def _():
        h_scr[...] = _norm_mod(x_ref[...], g_ref[...], sc_ref[0], sh_ref[0]).astype(h_scr.dtype)

    if precise:
        y = _dot3(h_scr[...], w_ref[...])
    else:
        y = _dot(h_scr[...], w_ref[...].astype(BF16))
    if rope:
        cos = cos_ref[...]
        sin = sin_ref[...]
    for c in range(tn // LANES):
        yc = y[:, c * LANES:(c + 1) * LANES]
        if rope:
            yc = _rope_tile(yc, cos, sin, rope)
        for o_ref, kind in zip(out_refs, kinds):
            if kind == HEADS:
                o_ref[c] = yc.astype(BF16)
                continue
            tiles = _emit(yc, kind)
            for k, tile in enumerate(tiles):
                at = (c * len(tiles) + k) * LANES
                o_ref[:, at:at + LANES] = tile


def proj(x, shift, scale, g, w, slot, col_off, n_cols, rows_per_batch, dtypes,
         rope=0, tables=None, precise=False):
    m, d = x.shape
    tn = min(512, n_cols)
    tm = _row_tile(1024, rows_per_batch)
    off = col_off // tn
    assert col_off % tn == 0 and n_cols % tn == 0
    in_specs = [
        pl.BlockSpec((tm, d), lambda i, j: (i, 0)),
        _mod_spec(shift, tm, d, rows_per_batch, lambda j: 0),
        _mod_spec(scale, tm, d, rows_per_batch, lambda j: 0),
        pl.BlockSpec((1, d), lambda i, j: (0, 0)),
        pl.BlockSpec((None, d, tn), lambda i, j: (slot, 0, off + j)),
    ]
    args = [x, shift, scale, g.reshape(1, d), w]
    if rope:
        cos, sin = tables
        nbt = max(cos.shape[0] // tm, 1)
        in_specs += [pl.BlockSpec((tm, LANES), lambda i, j: (i % nbt, 0))] * 2
        args += [cos, sin]
    out = pl.pallas_call(
        functools.partial(_proj_kernel, rope=rope, kinds=tuple(dtypes), tn=tn, precise=precise),
        grid=(m // tm, n_cols // tn),
        in_specs=in_specs,
        out_specs=[pl.BlockSpec((tn // LANES, tm, LANES), lambda i, j: (j, i, 0)) if dt == HEADS else
                   pl.BlockSpec((tm, tn * _WIDTH.get(dt, 1)), lambda i, j: (i, j)) for dt in dtypes],
        out_shape=[jax.ShapeDtypeStruct((n_cols // LANES, m, LANES), BF16) if dt == HEADS else
                   jax.ShapeDtypeStruct((m, n_cols * _WIDTH.get(dt, 1)), dt if dt in (F32, BF16) else BF16)
                   for dt in dtypes],
        scratch_shapes=[pltpu.VMEM((tm, d), F32 if precise else BF16)],
        compiler_params=_cparams(("arbitrary", "arbitrary")),
        name="proj",
    )(*args)
    return out


def _outproj_kernel(a_ref, x_ref, gt_ref, w_ref, o_ref):
    y = _dot(a_ref[...], w_ref[...].astype(BF16))
    o_ref[...] = x_ref[...] + gt_ref[0] * y


def outproj(a, x, gate, w, slot, rows_per_batch):
    m, d = x.shape
    k = a.shape[1]
    tn = 512
    tm = _row_tile(1024, rows_per_batch)
    return pl.pallas_call(
        _outproj_kernel,
        grid=(m // tm, d // tn),
        in_specs=[pl.BlockSpec((tm, k), lambda i, j: (i, 0)),
                  pl.BlockSpec((tm, tn), lambda i, j: (i, j)),
                  _mod_spec(gate, tm, tn, rows_per_batch, lambda j: j),
                  pl.BlockSpec((None, k, tn), lambda i, j: (slot, 0, j))],
        out_specs=pl.BlockSpec((tm, tn), lambda i, j: (i, j)),
        out_shape=jax.ShapeDtypeStruct((m, d), F32),
        compiler_params=_cparams(("arbitrary", "arbitrary")),
        name="outproj",
    )(a, x, gate, w)


def _mlstm_kernel(q_ref, k_ref, v_ref, o_ref, gates_ref, bias_ref, ng_ref,
                  hh_ref, c_ref, n_ref, m_ref, *, heads, dk, dv, chunk):
    c_idx = pl.program_id(1)

    @pl.when(c_idx == 0)
    def _():
        c_ref[...] = jnp.zeros_like(c_ref)
        n_ref[...] = jnp.zeros_like(n_ref)
        m_ref[...] = jnp.zeros_like(m_ref)

    L = chunk
    g = gates_ref[...] + bias_ref[...]
    gt = g.T
    lf = _log_sigmoid(g)
    lft = _log_sigmoid(gt)
    row = lax.broadcasted_iota(I32, (L, L), 0)
    col = lax.broadcasted_iota(I32, (L, L), 1)
    causal = col <= row
    qscale = dk ** -0.5
    for h in range(heads):
        icol = g[:, h:h + 1]
        fcol = lf[:, heads + h:heads + h + 1]
        irow = gt[h:h + 1, :]
        frow = lft[heads + h:heads + h + 1, :]
        b_col = jnp.sum(jnp.where(causal, frow, 0.0), axis=1, keepdims=True)
        b_row = jnp.sum(jnp.where(row <= col, fcol, 0.0), axis=0, keepdims=True)
        dmat = jnp.where(causal, b_col - b_row + irow, NEG)
        m_prev = m_ref[0, h:h + 1, 0:1]
        inter = b_col + m_prev
        m_t = jnp.maximum(inter, jnp.max(dmat, axis=1, keepdims=True))
        w_intra = jnp.exp(dmat - m_t)
        w_inter = jnp.exp(inter - m_t)
        qf = q_ref[:, h * dk:(h + 1) * dk] * qscale
        kf = k_ref[:, h * dk:(h + 1) * dk]
        vf = v_ref[:, h * dv:(h + 1) * dv]
        s = _dot3(qf, kf, _dot_nt) * w_intra
        c_prev = c_ref[0, h]
        n_prev = n_ref[0, h:h + 1, :]
        num = _dot3(s, vf) + w_inter * _dot3(qf, c_prev, _dot_nt)
        den = jnp.sum(s, axis=1, keepdims=True) + w_inter * jnp.sum(qf * n_prev, axis=1, keepdims=True)
        hh = num / jnp.maximum(jnp.abs(den), jnp.exp(-m_t))
        m_new = m_t[L - 1:L, :]
        b_last = b_col[L - 1:L, :]
        w_s = jnp.exp(b_last - b_col + icol - m_new)
        decay = jnp.exp(b_last + m_prev - m_new)
        vw_t = (vf * w_s).T
        c_ref[0, h] = decay * c_prev + _dot3(vw_t, kf)
        n_ref[0, h:h + 1, :] = decay * n_prev + jnp.sum(kf * w_s, axis=0, keepdims=True)
        m_ref[0, h:h + 1, :] = jnp.broadcast_to(m_new, (1, LANES))
        hn = hh * lax.rsqrt(jnp.mean(hh * hh, -1, keepdims=True) + EPS) * ng_ref[:, h * dv:(h + 1) * dv]
        hn = hn * jax.nn.sigmoid(o_ref[:, h * dv:(h + 1) * dv])
        hh_ref[:, h * dv:(h + 1) * dv] = hn.astype(BF16)


def mlstm_prompt(y, gates, b_if, norm_g, bn, t):
    heads = A_HEADS
    dv = norm_g.shape[-1]
    dk = dv // 2
    qk = heads * dk
    vw = heads * dv
    L = A_CHUNK
    nc = t // L
    bias = jnp.zeros((1, LANES), F32).at[0, :2 * heads].set(b_if)
    row = lambda b, c: b * nc + c
    outs = pl.pallas_call(
        functools.partial(_mlstm_kernel, heads=heads, dk=dk, dv=dv, chunk=L),
        grid=(bn, nc),
        in_specs=[pl.BlockSpec((L, qk), lambda b, c: (row(b, c), 0)),
                  pl.BlockSpec((L, qk), lambda b, c: (row(b, c), 1)),
                  pl.BlockSpec((L, vw), lambda b, c: (row(b, c), 1)),
                  pl.BlockSpec((L, vw), lambda b, c: (row(b, c), 2)),
                  pl.BlockSpec((L, LANES), lambda b, c: (row(b, c), 0)),
                  pl.BlockSpec((1, LANES), lambda b, c: (0, 0)),
                  pl.BlockSpec((1, vw), lambda b, c: (0, 0))],
        out_specs=[pl.BlockSpec((L, vw), lambda b, c: (row(b, c), 0)),
                   pl.BlockSpec((1, heads, dv, dk), lambda b, c: (b, 0, 0, 0)),
                   pl.BlockSpec((1, heads, dk), lambda b, c: (b, 0, 0)),
                   pl.BlockSpec((1, heads, LANES), lambda b, c: (b, 0, 0))],
        out_shape=[jax.ShapeDtypeStruct((bn * t, vw), BF16),
                   jax.ShapeDtypeStruct((bn, heads, dv, dk), F32),
                   jax.ShapeDtypeStruct((bn, heads, dk), F32),
                   jax.ShapeDtypeStruct((bn, heads, LANES), F32)],
        compiler_params=_cparams(("arbitrary", "arbitrary")),
        name="mlstm_scan",
    )(y, y, y, y, gates, bias, norm_g.reshape(1, vw))
    hh, c_fin, n_fin, m_fin = outs
    return hh, c_fin, n_fin, m_fin[:, :, 0]


def _mlstm_step_kernel(q_ref, k_ref, v_ref, o_ref, ig_ref, fg_ref, c0_ref, n0_ref, m0_ref, ng_ref,
                       h_ref, c_ref, n_ref, m_ref, *, dk):
    cst = c0_ref[0]
    n0 = n0_ref[0]
    m0 = m0_ref[0]
    q = q_ref[0] * dk ** -0.5
    k = k_ref[0]
    v = v_ref[0]
    ig = ig_ref[0]
    fc = _log_sigmoid(fg_ref[0])
    inter = fc + m0
    m_t = jnp.maximum(inter, ig)
    w_intra = jnp.exp(ig - m_t)
    w_inter = jnp.exp(inter - m_t)
    s = jnp.sum(q * k, axis=-1, keepdims=True) * w_intra
    cq = jnp.sum(cst * q, axis=-1, keepdims=True)
    num = s * v + w_inter * cq
    den = s + w_inter * jnp.sum(n0 * q, axis=-1, keepdims=True)
    hh = num / jnp.maximum(jnp.abs(den), jnp.exp(-m_t))
    w_s = w_intra
    decay = jnp.exp(inter - m_t)
    c_ref[0] = decay * cst + w_s * (v * k)
    n_ref[0] = decay * n0 + w_s * k
    m_ref[0] = m_t
    hn = hh * lax.rsqrt(jnp.mean(hh * hh, axis=1, keepdims=True) + EPS) * ng_ref[...]
    h_ref[0] = hn * jax.nn.sigmoid(o_ref[0])


def mlstm_sample(y, gates, b_if, norm_g, c0, n0, m0, bs):
    heads = A_HEADS
    dv = norm_g.shape[-1]
    dk = dv // 2
    qk = heads * dk
    vw = heads * dv
    y = y[:bs]
    q = y[:, :qk].reshape(bs, heads, 1, dk)
    k = y[:, qk:2 * qk].reshape(bs, heads, 1, dk)
    v = y[:, 2 * qk:2 * qk + vw].reshape(bs, heads, dv, 1)
    o = y[:, 2 * qk + vw:].reshape(bs, heads, dv, 1)
    gb = gates[:bs, :2 * heads] + b_if
    ig = gb[:, :heads].reshape(bs, heads, 1, 1)
    fg = gb[:, heads:].reshape(bs, heads, 1, 1)
    spec = lambda *shape: pl.BlockSpec((1,) + shape, lambda b: (b,) + (0,) * len(shape))
    outs = pl.pallas_call(
        functools.partial(_mlstm_step_kernel, dk=dk),
        grid=(bs,),
        in_specs=[spec(heads, 1, dk), spec(heads, 1, dk), spec(heads, dv, 1), spec(heads, dv, 1),
                  spec(heads, 1, 1), spec(heads, 1, 1),
                  spec(heads, dv, dk), spec(heads, 1, dk), spec(heads, 1, 1),
                  pl.BlockSpec((heads, dv, 1), lambda b: (0, 0, 0))],
        out_specs=[spec(heads, dv, 1), spec(heads, dv, dk), spec(heads, 1, dk), spec(heads, 1, 1)],
        out_shape=[jax.ShapeDtypeStruct((bs, heads, dv, 1), F32),
                   jax.ShapeDtypeStruct((bs, heads, dv, dk), F32),
                   jax.ShapeDtypeStruct((bs, heads, 1, dk), F32),
                   jax.ShapeDtypeStruct((bs, heads, 1, 1), F32)],
        compiler_params=_cparams(("arbitrary",)),
        name="mlstm_step",
    )(q, k, v, o, ig, fg, c0, n0.reshape(bs, heads, 1, dk), m0.reshape(bs, heads, 1, 1),
      norm_g.reshape(heads, dv, 1))
    hh, c_new, n_new, m_new = outs
    return hh.reshape(bs, vw), c_new, n_new.reshape(bs, heads, dk), m_new.reshape(bs, heads)


def _diff_lambda(lam_ref, lam_init):
    l = lam_ref[...]
    a = jnp.sum(l[0:1, :] * l[1:2, :], axis=1, keepdims=True)
    b = jnp.sum(l[2:3, :] * l[3:4, :], axis=1, keepdims=True)
    return jnp.exp(a) - jnp.exp(b) + lam_init


def _flash_step(s, vs, m_scr, l_scr, acc_scr, c):
    m_prev = m_scr[c]
    m_new = jnp.maximum(m_prev, jnp.max(s, axis=1, keepdims=True))
    alpha = jnp.exp(m_prev - m_new)
    p = jnp.exp(s - m_new)
    l_scr[c] = alpha * l_scr[c] + jnp.sum(p, axis=1, keepdims=True)
    acc_scr[c] = alpha * acc_scr[c] + _dot(p.astype(BF16), vs)
    m_scr[c] = m_new


def _diff_prompt_kernel(q_ref, k_ref, v_ref, lam_ref, ng_ref, o_ref, m_scr, l_scr, acc_scr,
                        *, tq, tk, dk, lam_init):
    i = pl.program_id(2)
    scale = dk ** -0.5
    m_scr[...] = jnp.full_like(m_scr, NEG)
    l_scr[...] = jnp.zeros_like(l_scr)
    acc_scr[...] = jnp.zeros_like(acc_scr)

    def chunk(j, masked):
        start = pl.multiple_of(j * tk, tk)
        vs = v_ref[pl.ds(start, tk), :]
        for c in range(2):
            s = _dot_nt(q_ref[:, c * dk:(c + 1) * dk], k_ref[pl.ds(start, tk), c * dk:(c + 1) * dk]) * scale
            if masked:
                row = i * tq + lax.broadcasted_iota(I32, (tq, tk), 0)
                col = start + lax.broadcasted_iota(I32, (tq, tk), 1)
                s = jnp.where(col <= row, s, NEG)
            _flash_step(s, vs, m_scr, l_scr, acc_scr, c)

    def body(j, carry):
        chunk(j, False)
        return carry

    n_full = (i * tq) // tk
    lax.fori_loop(0, n_full, body, 0)
    chunk(n_full, True)
    lam = _diff_lambda(lam_ref, lam_init)
    o = acc_scr[0] / l_scr[0] - lam * (acc_scr[1] / l_scr[1])
    o = o * lax.rsqrt(jnp.mean(o * o, -1, keepdims=True) + EPS) * ng_ref[...] * (1.0 - lam_init)
    o_ref[...] = o.astype(o_ref.dtype)


def diff_prompt(qb, kb, vb, lam, norm_g, lam_init, bn, t):
    heads = B_HEADS
    dv = norm_g.shape[-1]
    dk = dv // 2
    tq = min(256, t)
    tk = min(1024, t)
    nq = t // tq
    return pl.pallas_call(
        functools.partial(_diff_prompt_kernel, tq=tq, tk=tk, dk=dk, lam_init=lam_init),
        grid=(bn, heads, nq),
        in_specs=[pl.BlockSpec((tq, 2 * dk), lambda b, h, i: (b * nq + i, h)),
                  pl.BlockSpec((t, 2 * dk), lambda b, h, i: (b, h)),
                  pl.BlockSpec((t, dv), lambda b, h, i: (b, h)),
                  pl.BlockSpec(lam.shape, lambda b, h, i: (0, 0)),
                  pl.BlockSpec((1, dv), lambda b, h, i: (0, h))],
        out_specs=pl.BlockSpec((tq, dv), lambda b, h, i: (b * nq + i, h)),
        out_shape=jax.ShapeDtypeStruct((bn * t, heads * dv), BF16),
        scratch_shapes=[pltpu.VMEM((2, tq, 1), F32), pltpu.VMEM((2, tq, 1), F32),
                        pltpu.VMEM((2, tq, dv), F32)],
        compiler_params=_cparams(("arbitrary", "arbitrary", "arbitrary")),
        name="diff_prompt",
    )(qb, kb, vb, lam, norm_g.reshape(1, heads * dv))


def _paged_kernel(pt_ref, q_ref, kn_ref, vn_ref, *rest, mode, group, n_steps, pps, lam_init):
    k_refs, v_refs, rest = rest[:pps], rest[pps:2 * pps], rest[2 * pps:]
    if mode == "diff":
        lam_ref, ng_ref, o_ref, m_scr, l_scr, acc_scr = rest
    else:
        bias_ref, biasn_ref, o_ref, exp_scr, m_scr, l_scr, acc_scr = rest
    p = pl.program_id(1)
    rows, dk = q_ref.shape[1:]
    ncol = PAGE_SIZE * group
    scale = dk ** -0.5

    @pl.when(p == 0)
    def _():
        m_scr[...] = jnp.full_like(m_scr, NEG)
        l_scr[...] = jnp.zeros_like(l_scr)
        acc_scr[...] = jnp.zeros_like(acc_scr)
        if mode == "dsa":
            t = lax.broadcasted_iota(I32, (PAGE_SIZE, ncol), 0)
            c = lax.broadcasted_iota(I32, (PAGE_SIZE, ncol), 1)
            exp_scr[...] = jnp.where(c // group == t, 1.0, 0.0).astype(BF16)

    q = q_ref[0]
    qb = q.astype(BF16)

    def page_scores(k_ref):
        if mode == "diff":
            half = rows // 2
            return jnp.concatenate(
                [_dot_nt(qb[c * half:(c + 1) * half], k_ref[pl.ds(c, ncol, stride=2), :].astype(BF16))
                 for c in range(2)], axis=0)
        return _dot_nt(qb, k_ref[...].astype(BF16))

    s = jnp.concatenate([page_scores(k_ref) for k_ref in k_refs], axis=1)
    r_i = lax.broadcasted_iota(I32, s.shape, 0)
    c_i = lax.broadcasted_iota(I32, s.shape, 1)
    keep = (c_i % group) == (r_i % group)
    if mode == "dsa":
        sel = jnp.where(bias_ref[0, 0] == 0.0, 1.0, 0.0).astype(BF16)
        sel = jnp.concatenate(
            [_dot(jnp.broadcast_to(sel[pp:pp + 1], (8, PAGE_SIZE)), exp_scr[...])[0:1] for pp in range(pps)],
            axis=1)
        keep = keep & (sel > 0.5)
    s = jnp.where(keep, s * scale, NEG)
    m_prev = m_scr[...]
    m_new = jnp.maximum(m_prev, jnp.max(s, axis=1, keepdims=True))
    alpha = jnp.exp(m_prev - m_new)
    pe = jnp.exp(s - m_new)
    l_scr[...] = alpha * l_scr[...] + jnp.sum(pe, axis=1, keepdims=True)
    pv = _dot(pe[:, :ncol].astype(BF16), v_refs[0][...].astype(BF16))
    for pp in range(1, pps):
        pv = pv + _dot(pe[:, pp * ncol:(pp + 1) * ncol].astype(BF16), v_refs[pp][...].astype(BF16))
    acc_scr[...] = alpha * acc_scr[...] + pv
    m_scr[...] = m_new

    @pl.when(p == n_steps - 1)
    def _():
        s_n = jnp.sum(q * kn_ref[0], axis=1, keepdims=True) * scale
        v_n = vn_ref[0]
        if mode == "dsa":
            s_n = s_n + biasn_ref[0][:, 0:1]
        else:
            v_n = jnp.concatenate([v_n, v_n], axis=0)
        m_prev = m_scr[...]
        m_new = jnp.maximum(m_prev, s_n)
        alpha = jnp.exp(m_prev - m_new)
        pn = jnp.exp(s_n - m_new)
        l_fin = alpha * l_scr[...] + pn
        acc = (alpha * acc_scr[...] + pn * v_n) / l_fin
        if mode == "diff":
            half = rows // 2
            o = acc[:half] - _diff_lambda(lam_ref, lam_init) * acc[half:]
            o = o * lax.rsqrt(jnp.mean(o * o, -1, keepdims=True) + EPS) * ng_ref[...]
            o_ref[0] = o * (1.0 - lam_init)
        else:
            o_ref[0] = acc


def paged_attention(mode, q, k_new, v_new, cache_k, cache_v, slot, page_table, extra, lam_init=0.0):
    bs, n_pages = page_table.shape
    rows, dk = q.shape[1:]
    hv, dv = v_new.shape[1:]
    group = hv
    ncol = PAGE_SIZE * group
    pps = max(c for c in (4, 2, 1) if n_pages % c == 0)
    n_steps = n_pages // pps
    seq_spec = lambda a: pl.BlockSpec((1,) + a.shape[1:], lambda b, p, pt: (b, 0, 0))

    def page_specs(a):
        return [pl.BlockSpec((None, None) + a.shape[2:],
                             lambda b, p, pt, pp=pp: (slot, pt[b, p * pps + pp], 0, 0)) for pp in range(pps)]

    scratch = [pltpu.VMEM((rows, 1), F32), pltpu.VMEM((rows, 1), F32), pltpu.VMEM((rows, dv), F32)]
    if mode == "diff":
        lam, norm_g = extra
        extra_specs = [pl.BlockSpec(lam.shape, lambda b, p, pt: (0, 0)),
                       pl.BlockSpec(norm_g.shape, lambda b, p, pt: (0, 0))]
        extra_args = [lam, norm_g]
    else:
        bias, bias_new = extra
        bias = bias.reshape(bs, n_steps, pps, PAGE_SIZE)
        extra_specs = [pl.BlockSpec((1, 1, pps, PAGE_SIZE), lambda b, p, pt: (b, p, 0, 0)),
                       pl.BlockSpec((1, 1, LANES), lambda b, p, pt: (b, 0, 0))]
        extra_args = [bias, bias_new]
        scratch = [pltpu.VMEM((PAGE_SIZE, ncol), BF16)] + scratch
    grid_spec = pltpu.PrefetchScalarGridSpec(
        num_scalar_prefetch=1,
        grid=(bs, n_steps),
        in_specs=[seq_spec(q), seq_spec(k_new), seq_spec(v_new)] + page_specs(cache_k)
        + page_specs(cache_v) + extra_specs,
        out_specs=pl.BlockSpec((1, hv, dv), lambda b, p, pt: (b, 0, 0)),
        scratch_shapes=scratch,
    )
    return pl.pallas_call(
        functools.partial(_paged_kernel, mode=mode, group=group, n_steps=n_steps, pps=pps,
                          lam_init=lam_init),
        grid_spec=grid_spec,
        out_shape=jax.ShapeDtypeStruct((bs, hv, dv), F32),
        compiler_params=_cparams(("arbitrary", "arbitrary")),
        name="paged_" + mode,
    )(page_table, q, k_new, v_new, *([cache_k] * pps), *([cache_v] * pps), *extra_args)


def _sortable_key(s):
    bits = lax.bitcast_convert_type(s, I32)
    return bits ^ ((bits >> 31) & 0x7FFFFFFF)


def _bisect_threshold(count_ge, shapes, n_top):
    def step(it, t_us):
        bit = jnp.left_shift(jnp.int32(1), 31 - it)
        cands = [t_u | bit for t_u in t_us]
        cnts = count_ge([c ^ INT_MIN for c in cands])
        return tuple(jnp.where(cnt >= n_top, c, t_u) for cnt, c, t_u in zip(cnts, cands, t_us))

    t_us = lax.fori_loop(0, 32, step, tuple(jnp.zeros(shape, I32) for shape in shapes))
    return [jnp.maximum(t_u ^ INT_MIN, INT_MIN + 1) for t_u in t_us]


def _dsa_prompt_kernel(qi_ref, wt_ref, ki_ref, q_ref, k_ref, v_ref, o_ref,
                       key_scr, bias_scr, wb_scr, m_scr, l_scr, acc_scr, *, tq, tk, dh, n_top, wcol):
    i = pl.program_id(1)
    h = pl.program_id(2)
    scale = dh ** -0.5
    idx_scale = IDX_DIM ** -0.5 * IDX_HEADS ** -0.5
    groups = tq // LANES
    n_att = (i * tq) // tk + 1

    @pl.when(h == 0)
    def _():
        w = wt_ref[...] * idx_scale
        for hh in range(IDX_HEADS):
            wb_scr[hh] = jnp.broadcast_to(w[:, wcol + hh:wcol + hh + 1], (tq, LANES))

        def score_chunk(j, masked):
            start = pl.multiple_of(j * tq, tq)
            kc = ki_ref[pl.ds(start, tq), :]
            acc = [jnp.zeros((tq, LANES), F32) for _ in range(groups)]
            for hh in range(IDX_HEADS):
                s = _dot_nt(qi_ref[:, hh * IDX_PACK:(hh + 1) * IDX_PACK], kc)
                wb = wb_scr[hh]
                for c in range(groups):
                    acc[c] = acc[c] + jnp.maximum(s[:, c * LANES:(c + 1) * LANES], 0.0) * wb
            for c in range(groups):
                key = _sortable_key(acc[c])
                if masked:
                    row = lax.broadcasted_iota(I32, (tq, LANES), 0)
                    col = lax.broadcasted_iota(I32, (tq, LANES), 1) + c * LANES
                    key = jnp.where(col <= row, key, INT_MIN)
                key_scr[:, pl.ds(pl.multiple_of(start + c * LANES, LANES), LANES)] = key

        def score_body(j, carry):
            score_chunk(j, False)
            return carry

        lax.fori_loop(0, i, score_body, 0)
        score_chunk(i, True)

        row_blocks = [pl.ds(rb * BISECT_ROWS, BISECT_ROWS) for rb in range(tq // BISECT_ROWS)]

        def count_ge(cands):
            parts = []
            for rows, cand in zip(row_blocks, cands):
                cand_b = jnp.broadcast_to(cand, (BISECT_ROWS, LANES))

                def cbody(j, part, rows=rows, cand_b=cand_b):
                    start = pl.multiple_of(j * tq, tq)
                    for c in range(groups):
                        kk = key_scr[rows, pl.ds(start + c * LANES, LANES)]
                        part = part + jnp.where(kk >= cand_b, 1.0, 0.0)
                    return part

                parts.append(lax.fori_loop(0, i + 1, cbody, jnp.zeros((BISECT_ROWS, LANES), F32)))
            return [jnp.sum(part, axis=1, keepdims=True) for part in parts]

        thrs = _bisect_threshold(count_ge, [(BISECT_ROWS, 1)] * len(row_blocks), n_top)
        for rows, thr in zip(row_blocks, thrs):
            thr = jnp.broadcast_to(thr, (BISECT_ROWS, LANES))

            def bias_body(j, carry, rows=rows, thr=thr):
                start = pl.multiple_of(j * LANES, LANES)
                kk = key_scr[rows, pl.ds(start, LANES)]
                bias_scr[rows, pl.ds(start, LANES)] = jnp.where(kk >= thr, 0.0, NEG)
                return carry

            lax.fori_loop(0, (i + 1) * groups, bias_body, 0)

        def fill_body(j, carry):
            start = pl.multiple_of(j * LANES, LANES)
            bias_scr[:, pl.ds(start, LANES)] = jnp.full((tq, LANES), NEG, F32)
            return carry

        lax.fori_loop((i + 1) * groups, n_att * (tk // LANES), fill_body, 0)

    m_scr[...] = jnp.full_like(m_scr, NEG)
    l_scr[...] = jnp.zeros_like(l_scr)
    acc_scr[...] = jnp.zeros_like(acc_scr)
    qh = q_ref[...]

    def att_body(j, carry):
        start = pl.multiple_of(j * tk, tk)
        s = _dot_nt(qh, k_ref[pl.ds(start, tk), :]) * scale + bias_scr[:, pl.ds(start, tk)]
        _flash_step(s, v_ref[pl.ds(start, tk), :], m_scr, l_scr, acc_scr, 0)
        return carry

    lax.fori_loop(0, n_att, att_body, 0)
    o_ref[...] = (acc_scr[0] / l_scr[0]).astype(o_ref.dtype)


def dsa_prompt(qi3, tail, ki3, qb, kb, vb, bn, t, wcol):
    heads, _, dh = qb.shape
    tq = min(256, t)
    tk = min(1024, t)
    nq = t // tq
    n_top = min(TOPK_MAX, t // 4)
    return pl.pallas_call(
        functools.partial(_dsa_prompt_kernel, tq=tq, tk=tk, dh=dh, n_top=n_top, wcol=wcol),
        grid=(bn, nq, heads),
        in_specs=[pl.BlockSpec((tq, IDX_HEADS * IDX_PACK), lambda b, i, h: (b * nq + i, 0)),
                  pl.BlockSpec((tq, LANES), lambda b, i, h: (b * nq + i, 0)),
                  pl.BlockSpec((t, IDX_PACK), lambda b, i, h: (b, 0)),
                  pl.BlockSpec((None, tq, dh), lambda b, i, h: (h, b * nq + i, 0)),
                  pl.BlockSpec((None, t, dh), lambda b, i, h: (h, b, 0)),
                  pl.BlockSpec((None, t, dh), lambda b, i, h: (h, b, 0))],
        out_specs=pl.BlockSpec((tq, dh), lambda b, i, h: (b * nq + i, h)),
        out_shape=jax.ShapeDtypeStruct((bn * t, heads * dh), BF16),
        scratch_shapes=[pltpu.VMEM((tq, t), I32), pltpu.VMEM((tq, t), F32),
                        pltpu.VMEM((IDX_HEADS, tq, LANES), F32),
                        pltpu.VMEM((1, tq, 1), F32), pltpu.VMEM((1, tq, 1), F32),
                        pltpu.VMEM((1, tq, dh), F32)],
        compiler_params=_cparams(("arbitrary", "arbitrary", "arbitrary")),
        name="dsa_prompt",
    )(qi3, tail, ki3, qb, kb, vb)


def _dsa_index_kernel(pt_ref, qi_ref, w_ref, kin_ref, *rest, n_steps, pps, n_top):
    kp_refs, (bias_ref, biasn_ref, sc_scr) = rest[:pps], rest[pps:]
    p = pl.program_id(1)
    idx_scale = IDX_DIM ** -0.5 * IDX_HEADS ** -0.5
    qi = qi_ref[0]
    w = w_ref[0] * idx_scale
    for pp, kp_ref in enumerate(kp_refs):
        s = _dot3(qi, kp_ref[...], _dot_nt)
        sc_scr[pl.ds(p * pps + pp, 1), :] = jnp.sum(jnp.maximum(s, 0.0) * w, axis=0, keepdims=True)

    @pl.when(p == n_steps - 1)
    def _():
        s_n = jnp.sum(qi * kin_ref[0], axis=1, keepdims=True)
        sc_new = jnp.sum(jnp.maximum(s_n, 0.0) * w, axis=0, keepdims=True)
        key = _sortable_key(sc_scr[...])
        key_new = _sortable_key(sc_new)

        def count_ge(cands):
            (cand,) = cands
            hit = jnp.where(key >= cand, 1.0, 0.0)
            cnt = jnp.sum(jnp.sum(hit, axis=1, keepdims=True), axis=0, keepdims=True)
            return [cnt + jnp.where(key_new >= cand, 1.0, 0.0)]

        (thr,) = _bisect_threshold(count_ge, [(1, 1)], n_top)
        bias_ref[0] = jnp.where(key >= thr, 0.0, NEG)
        biasn_ref[0] = jnp.broadcast_to(jnp.where(key_new >= thr, 0.0, NEG), (1, LANES))


def dsa_sample_index(qi, w, ki_new, cache_ki, slot, page_table):
    bs, n_pages = page_table.shape
    n_top = min(TOPK_MAX, (n_pages * PAGE_SIZE + 1) // 4)
    pps = max(c for c in (16, 8, 4, 2, 1) if n_pages % c == 0)
    n_steps = n_pages // pps
    grid_spec = pltpu.PrefetchScalarGridSpec(
        num_scalar_prefetch=1,
        grid=(bs, n_steps),
        in_specs=[pl.BlockSpec((1, IDX_HEADS, IDX_DIM), lambda b, p, pt: (b, 0, 0)),
                  pl.BlockSpec((1, IDX_HEADS, 1), lambda b, p, pt: (b, 0, 0)),
                  pl.BlockSpec((1, 1, IDX_DIM), lambda b, p, pt: (b, 0, 0))]
        + [pl.BlockSpec((None, None, PAGE_SIZE, IDX_DIM),
                        lambda b, p, pt, pp=pp: (slot, pt[b, p * pps + pp], 0, 0)) for pp in range(pps)],
        out_specs=[pl.BlockSpec((1, n_pages, PAGE_SIZE), lambda b, p, pt: (b, 0, 0)),
                   pl.BlockSpec((1, 1, LANES), lambda b, p, pt: (b, 0, 0))],
        scratch_shapes=[pltpu.VMEM((n_pages, PAGE_SIZE), F32)],
    )
    bias, bias_new = pl.pallas_call(
        functools.partial(_dsa_index_kernel, n_steps=n_steps, pps=pps, n_top=n_top),
        grid_spec=grid_spec,
        out_shape=[jax.ShapeDtypeStruct((bs, n_pages, PAGE_SIZE), F32),
                   jax.ShapeDtypeStruct((bs, 1, LANES), F32)],
        compiler_params=_cparams(("arbitrary", "arbitrary")),
        name="dsa_index",
    )(page_table, qi, w, ki_new, *([cache_ki] * pps))
    return bias, bias_new


def _rope_tables(pos, half):
    inv = ROPE_THETA ** (-jnp.arange(half, dtype=F32) / half)
    ang = pos.astype(F32)[:, None] * inv[None, :]
    return jnp.cos(ang), jnp.sin(ang)


def _tables_128(pos):
    c, s = _rope_tables(pos, 64)
    return jnp.concatenate([c, c], -1), jnp.concatenate([-s, s], -1)


def _tables_64(pos, tail):
    c, s = _rope_tables(pos, 32)
    if tail:
        one, zero = jnp.ones_like(c), jnp.zeros_like(s)
        return jnp.concatenate([c, c, one, one], -1), jnp.concatenate([-s, s, zero, zero], -1)
    return jnp.concatenate([c, c, c, c], -1), jnp.concatenate([-s, s, -s, s], -1)


def _pad_cols(w, n):
    return jnp.pad(w, ((0, 0), (0, 0), (0, n - w.shape[-1])))


def kernel(x_prompt, x_sample, c_prompt, c_sample, state_a_C, state_a_n, state_a_m, cache_b_k,
           cache_b_v, cache_c_k, cache_c_v, cache_c_kidx, page_table, ada_w, ada_b, norm_g, ffn_wi,
           ffn_wo, a_w_in, a_b_if, a_norm_g, a_w_out, b_w_in, b_lambda, b_norm_g, b_w_out, c_w_in,
           c_w_out, final_g):
    bp, t_p, d = x_prompt.shape
    bs = x_sample.shape[0]
    depth = ada_w.shape[0]
    n_pages = page_table.shape[1]
    past = n_pages * PAGE_SIZE
    n_pool = cache_b_k.shape[1]
    rs = SUBLANES_BF16 * ((bs + SUBLANES_BF16 - 1) // SUBLANES_BF16)
    mp = bp * t_p

    ff = ffn_wo.shape[2]
    fp = FFN_TF * ((ff + FFN_TF - 1) // FFN_TF)
    wi_p, wo_p = stage_ffn_weights(ffn_wi, ffn_wo, fp)

    c_rows = jnp.zeros((rs, d), F32).at[:bp].set(c_prompt).at[bp:bp + bs].set(c_sample)
    mod = ada_all(c_rows, ada_w, ada_b).reshape(depth, rs, N_ADA, d)

    xp = x_prompt.reshape(mp, d)
    xs = jnp.zeros((rs, d), F32).at[:bs].set(x_sample.reshape(bs, d))

    pos_p = jnp.arange(t_p, dtype=jnp.int32)
    pos_s = jnp.full((rs,), past, jnp.int32)
    tab128_p, tab128_s = _tables_128(pos_p), _tables_128(pos_s)
    tab64_p, tab64_s = _tables_64(pos_p, False), _tables_64(pos_s, False)
    tabt_p, tabt_s = _tables_64(pos_p, True), _tables_64(pos_s, True)

    a_main = 2 * A_HEADS * (a_norm_g.shape[-1] // 2) + 2 * A_HEADS * a_norm_g.shape[-1]
    a_tail_w = _pad_cols(a_w_in[:, :, a_main:], LANES)
    c_main = 3 * d + IDX_HEADS * IDX_DIM
    c_tail_w = _pad_cols(c_w_in[:, :, c_main:], LANES)

    kind_of = [i % 3 for i in range(depth)]
    outs = {k: [] for k in ("aCp", "anp", "amp", "aCs", "ans", "ams", "bkp", "bvp", "bks", "bvs",
                            "ckp", "cvp", "cip", "cks", "cvs", "cis")}
    for l in range(depth):
        kind = kind_of[l]
        j = kind_of[:l].count(kind)
        mp_l = [mod[l, :bp, k][:, None, :] for k in range(N_ADA)]
        ms_l = [mod[l, bp:bp + bs, k] for k in range(N_ADA)]
        ms_l = [jnp.zeros((1, rs, d), F32).at[0, :bs].set(m) for m in ms_l]

        xp = ffn_sublayer(xp, mp_l[0], mp_l[1], mp_l[2], norm_g[l, 0], wi_p, wo_p, l, 0, t_p)
        xs = ffn_sublayer(xs, ms_l[0], ms_l[1], ms_l[2], norm_g[l, 0], wi_p, wo_p, l, 0, rs)
        pj_p = functools.partial(proj, xp, mp_l[3], mp_l[4], norm_g[l, 1], rows_per_batch=t_p)
        pj_s = functools.partial(proj, xs, ms_l[3], ms_l[4], norm_g[l, 1], rows_per_batch=rs)

        if kind == 0:
            (y_p,) = pj_p(a_w_in, j, 0, a_main, dtypes=(F32,))
            (g_p,) = pj_p(a_tail_w, j, 0, LANES, dtypes=(F32,), precise=True)
            hh_p, c_p, n_p, m_p = mlstm_prompt(y_p, g_p, a_b_if[j], a_norm_g[j], bp, t_p)
            (y_s,) = pj_s(a_w_in, j, 0, a_main, dtypes=(F32,))
            (g_s,) = pj_s(a_tail_w, j, 0, LANES, dtypes=(F32,), precise=True)
            hh_s, c_s, n_s, m_s = mlstm_sample(y_s, g_s, a_b_if[j], a_norm_g[j],
                                               state_a_C[j], state_a_n[j], state_a_m[j], bs)
            op_in = hh_p
            os_in = jnp.zeros((rs, hh_s.shape[1]), BF16).at[:bs].set(hh_s.astype(BF16))
            w_out = a_w_out
            outs["aCp"].append(c_p); outs["anp"].append(n_p); outs["amp"].append(m_p)
            outs["aCs"].append(c_s); outs["ans"].append(n_s); outs["ams"].append(m_s)
        elif kind == 1:
            lam_init = 0.8 - 0.6 * math.exp(-0.3 * l)
            bqk = b_w_in.shape[2] // 3
            (q_pb,) = pj_p(b_w_in, j, 0, bqk, dtypes=(BF16,), rope=128, tables=tab128_p)
            k_p, k_pb = pj_p(b_w_in, j, bqk, bqk, dtypes=(F32, BF16), rope=128, tables=tab128_p)
            v_p, v_pb = pj_p(b_w_in, j, 2 * bqk, bqk, dtypes=(F32, BF16))
            op_in = diff_prompt(q_pb, k_pb, v_pb, b_lambda[j], b_norm_g[j], lam_init, bp, t_p)
            (q_s,) = pj_s(b_w_in, j, 0, bqk, dtypes=(F32,), rope=128, tables=tab128_s)
            (k_s,) = pj_s(b_w_in, j, bqk, bqk, dtypes=(F32,), rope=128, tables=tab128_s)
            (v_s,) = pj_s(b_w_in, j, 2 * bqk, bqk, dtypes=(F32,))
            comp_major = lambda a: a[:bs].reshape(bs, B_HEADS, 2, -1).transpose(0, 2, 1, 3).reshape(
                bs, 2 * B_HEADS, -1)
            o_s = paged_attention(
                "diff", comp_major(q_s), comp_major(k_s), v_s[:bs].reshape(bs, B_HEADS, -1),
                cache_b_k.reshape(cache_b_k.shape[0], n_pool, PAGE_SIZE * B_HEADS * 2, -1),
                cache_b_v.reshape(cache_b_v.shape[0], n_pool, PAGE_SIZE * B_HEADS, -1),
                j, page_table, (b_lambda[j], b_norm_g[j]), lam_init)
            os_in = jnp.zeros((rs, bqk), BF16).at[:bs].set(o_s.reshape(bs, bqk).astype(BF16))
            w_out = b_w_out
            outs["bkp"].append(k_p.reshape(bp, t_p, B_HEADS, 2, -1))
            outs["bvp"].append(v_p.reshape(bp, t_p, B_HEADS, -1))
            outs["bks"].append(k_s[:bs].reshape(bs, 1, B_HEADS, 2, -1))
            outs["bvs"].append(v_s[:bs].reshape(bs, 1, B_HEADS, -1))
        else:
            cw = d
            ni = IDX_HEADS * IDX_DIM
            (q_pb,) = pj_p(c_w_in, j, 0, cw, dtypes=(HEADS,), rope=128, tables=tab128_p)
            k_p, k_pb = pj_p(c_w_in, j, cw, cw, dtypes=(F32, HEADS), rope=128, tables=tab128_p)
            v_p, v_pb = pj_p(c_w_in, j, 2 * cw, cw, dtypes=(F32, HEADS))
            (qi_p3,) = pj_p(c_w_in, j, 3 * cw, ni, dtypes=(Q3,), rope=64, tables=tab64_p)
            t_pf, ki_p3 = pj_p(c_tail_w, j, 0, LANES, dtypes=(F32, K3), rope=64, tables=tabt_p)
            op_in = dsa_prompt(qi_p3, t_pf, ki_p3, q_pb, k_pb, v_pb, bp, t_p, IDX_DIM)
            (q_s,) = pj_s(c_w_in, j, 0, cw, dtypes=(F32,), rope=128, tables=tab128_s)
            (k_s,) = pj_s(c_w_in, j, cw, cw, dtypes=(F32,), rope=128, tables=tab128_s)
            (v_s,) = pj_s(c_w_in, j, 2 * cw, cw, dtypes=(F32,))
            (qi_s,) = pj_s(c_w_in, j, 3 * cw, ni, dtypes=(F32,), rope=64, tables=tab64_s)
            (t_s,) = pj_s(c_tail_w, j, 0, LANES, dtypes=(F32,), rope=64, tables=tabt_s)
            ki_s = t_s[:bs, :IDX_DIM]
            bias, bias_new = dsa_sample_index(
                qi_s[:bs].reshape(bs, IDX_HEADS, IDX_DIM),
                t_s[:bs, IDX_DIM:IDX_DIM + IDX_HEADS].reshape(bs, IDX_HEADS, 1),
                ki_s[:, None], cache_c_kidx, j, page_table)
            o_s = paged_attention(
                "dsa", q_s[:bs].reshape(bs, C_HEADS, -1), k_s[:bs].reshape(bs, C_HEADS, -1),
                v_s[:bs].reshape(bs, C_HEADS, -1),
                cache_c_k.reshape(cache_c_k.shape[0], n_pool, PAGE_SIZE * C_HEADS, -1),
                cache_c_v.reshape(cache_c_v.shape[0], n_pool, PAGE_SIZE * C_HEADS, -1),
                j, page_table, (bias, bias_new))
            os_in = jnp.zeros((rs, cw), BF16).at[:bs].set(o_s.reshape(bs, cw).astype(BF16))
            w_out = c_w_out
            outs["ckp"].append(k_p.reshape(bp, t_p, C_HEADS, -1))
            outs["cvp"].append(v_p.reshape(bp, t_p, C_HEADS, -1))
            outs["cip"].append(t_pf[:, :IDX_DIM].reshape(bp, t_p, IDX_DIM))
            outs["cks"].append(k_s[:bs].reshape(bs, 1, C_HEADS, -1))
            outs["cvs"].append(v_s[:bs].reshape(bs, 1, C_HEADS, -1))
            outs["cis"].append(ki_s.reshape(bs, 1, IDX_DIM))

        xp = outproj(op_in, xp, mp_l[5], w_out, j, t_p)
        xs = outproj(os_in, xs, ms_l[5], w_out, j, rs)
        fin = final_g if l == depth - 1 else None
        xp = ffn_sublayer(xp, mp_l[6], mp_l[7], mp_l[8], norm_g[l, 2], wi_p, wo_p, l, 1, t_p, fin)
        xs = ffn_sublayer(xs, ms_l[6], ms_l[7], ms_l[8], norm_g[l, 2], wi_p, wo_p, l, 1, rs, fin)

    st = lambda k: jnp.stack(outs[k])
    return (xp.reshape(bp, t_p, d), xs[:bs].reshape(bs, 1, d),
            st("aCp"), st("anp"), st("amp"), st("aCs"), st("ans"), st("ams"),
            st("bkp"), st("bvp"), st("bks"), st("bvs"),
            st("ckp"), st("cvp"), st("cip"), st("cks"), st("cvs"), st("cis"))
```

```python
import functools
import math

import jax
import jax.numpy as jnp
from jax import lax
from jax.experimental import pallas as pl
from jax.experimental.pallas import tpu as pltpu

F32 = jnp.float32
BF16 = jnp.bfloat16
I32 = jnp.int32

EPS = 1e-6
ROPE_THETA = 10000.0
N_ADA = 9
TOPK_MAX = 256
PAGE_SIZE = 128
A_HEADS = 8
A_CHUNK = 128
B_HEADS = 8
C_HEADS = 16
IDX_HEADS = 16
IDX_DIM = 64
IDX_PACK = 4 * IDX_DIM

LANES = 128
SUBLANES_BF16 = 16
VMEM_LIMIT = 56 * 1024 * 1024
BISECT_ROWS = 128
NEG = -1e30
INT_MIN = -2 ** 31


def _cparams(sem):
    return pltpu.CompilerParams(dimension_semantics=sem, vmem_limit_bytes=VMEM_LIMIT)


def _dot(a, b):
    return jnp.dot(a, b, preferred_element_type=F32)


def _dot_nt(a, b):
    return lax.dot_general(a, b, (((1,), (1,)), ((), ())), preferred_element_type=F32)


def _split(x):
    hi = x.astype(BF16)
    return hi, (x - hi.astype(F32)).astype(BF16)


def _dot3(a, b, dot=_dot):
    a_hi, a_lo = _split(a)
    b_hi, b_lo = _split(b)
    return dot(a_hi, b_hi) + (dot(a_hi, b_lo) + dot(a_lo, b_hi))


def _norm_mod(x, g, scale, shift):
    y = x * lax.rsqrt(jnp.mean(x * x, -1, keepdims=True) + EPS) * g
    return y * (1.0 + scale) + shift


NORM_ROWS = 16
NORM_UNROLL = 4


def _norm_mod_to(h_scr, x_ref, g_ref, sc_ref, sh_ref):
    tm = x_ref.shape[0]
    nr = min(NORM_ROWS, tm)
    per_row = sc_ref.shape[1] != 1
    g = g_ref[...]
    if not per_row:
        gain = g * (1.0 + sc_ref[0])
        shift = sh_ref[0]

    def body(r, carry):
        rows = pl.ds(pl.multiple_of(r * nr, nr), nr)
        x = x_ref[rows, :]
        y = x * lax.rsqrt(jnp.mean(x * x, -1, keepdims=True) + EPS)
        if per_row:
            h = y * (g * (1.0 + sc_ref[0, rows, :])) + sh_ref[0, rows, :]
        else:
            h = y * gain + shift
        h_scr[rows, :] = h.astype(h_scr.dtype)
        return carry

    lax.fori_loop(0, tm // nr, body, 0, unroll=min(NORM_UNROLL, tm // nr))


def _log_sigmoid(x):
    return jnp.minimum(x, 0.0) - jnp.log1p(jnp.exp(-jnp.abs(x)))


def _ada_kernel(c_ref, w_ref, b_ref, o_ref):
    c = c_ref[...]
    a = (c * jax.nn.sigmoid(c)).astype(BF16)
    o_ref[0] = _dot(a, w_ref[0].astype(BF16)) + b_ref[0]


def ada_all(c_rows, ada_w, ada_b):
    depth, d, n = ada_w.shape
    r = c_rows.shape[0]
    tn = 1024
    return pl.pallas_call(
        _ada_kernel,
        grid=(depth, n // tn),
        in_specs=[pl.BlockSpec((r, d), lambda l, j: (0, 0)),
                  pl.BlockSpec((1, d, tn), lambda l, j: (l, 0, j)),
                  pl.BlockSpec((1, 1, tn), lambda l, j: (l, 0, j))],
        out_specs=pl.BlockSpec((1, r, tn), lambda l, j: (l, 0, j)),
        out_shape=jax.ShapeDtypeStruct((depth, r, n), F32),
        compiler_params=_cparams(("arbitrary", "arbitrary")),
        name="ada_mod",
    )(c_rows, ada_w, ada_b.reshape(depth, 1, n))


def _row_tile(want, rows_per_batch):
    return min(want, rows_per_batch)


def _mod_spec(mod, tm, tn, rows_per_batch, col_of):
    r = mod.shape[1]
    bpb = max(rows_per_batch // tm, 1)
    return pl.BlockSpec((1, r, tn), lambda i, j: (i // bpb, 0, col_of(j)))


def _stage_wi_kernel(w_ref, o_ref):
    ff = w_ref.shape[-1]
    o_ref[:, :ff] = w_ref[...].astype(BF16)
    o_ref[:, ff:] = jnp.zeros((o_ref.shape[0], o_ref.shape[1] - ff), BF16)


def _stage_wo_kernel(w_ref, o_ref, *, n_blocks, last_rows):
    o_ref[...] = w_ref[...].astype(BF16)
    if last_rows < o_ref.shape[0]:

        @pl.when(pl.program_id(1) == n_blocks - 1)
        def _():
            o_ref[last_rows:, :] = jnp.zeros((o_ref.shape[0] - last_rows, o_ref.shape[1]), BF16)


def stage_ffn_weights(ffn_wi, ffn_wo, fp):
    depth, _, d, ff2 = ffn_wi.shape
    ff = ff2 // 2
    assert ff % LANES == 0 and fp % LANES == 0
    ls = depth * 2
    rb = min(128, d)
    wi_p = pl.pallas_call(
        _stage_wi_kernel,
        grid=(ls, d // rb, 2),
        in_specs=[pl.BlockSpec((None, rb, ff), lambda a, r, h: (a, r, h))],
        out_specs=pl.BlockSpec((None, rb, fp), lambda a, r, h: (a, r, h)),
        out_shape=jax.ShapeDtypeStruct((ls, d, 2 * fp), BF16),
        compiler_params=_cparams(("arbitrary", "arbitrary", "arbitrary")),
        name="stage_wi",
    )(ffn_wi.reshape(ls, d, ff2))
    wrb = max(r for r in range(SUBLANES_BF16, 1025, SUBLANES_BF16) if fp % r == 0)
    n_blocks = fp // wrb
    wo_p = pl.pallas_call(
        functools.partial(_stage_wo_kernel, n_blocks=n_blocks, last_rows=ff - (n_blocks - 1) * wrb),
        grid=(ls, n_blocks),
        in_specs=[pl.BlockSpec((None, wrb, d), lambda a, j: (a, j, 0))],
        out_specs=pl.BlockSpec((None, wrb, d), lambda a, j: (a, j, 0)),
        out_shape=jax.ShapeDtypeStruct((ls, fp, d), BF16),
        compiler_params=_cparams(("arbitrary", "arbitrary")),
        name="stage_wo",
    )(ffn_wo.reshape(ls, ff, d))
    return wi_p.reshape(depth, 2, d, 2 * fp), wo_p.reshape(depth, 2, fp, d)


def _ffn_kernel(*refs, nf, final):
    if final:
        (x_ref, sh_ref, sc_ref, gt_ref, g_ref, wg_ref, wu_ref, wo_ref, fg_ref,
         o_ref, h_scr, acc_scr) = refs
    else:
        (x_ref, sh_ref, sc_ref, gt_ref, g_ref, wg_ref, wu_ref, wo_ref,
         o_ref, h_scr, acc_scr) = refs
    f = pl.program_id(1)

    @pl.when(f == 0)
    def _():
        _norm_mod_to(h_scr, x_ref, g_ref, sc_ref, sh_ref)
        acc_scr[...] = jnp.zeros_like(acc_scr)

    h = h_scr[...]
    a = _dot(h, wg_ref[...])
    u = _dot(h, wu_ref[...])
    act = (a * jax.nn.sigmoid(a) * u).astype(BF16)
    acc_scr[...] += _dot(act, wo_ref[...])

    @pl.when(f == nf - 1)
    def _():
        y = x_ref[...] + 0.5 * gt_ref[0] * acc_scr[...]
        if final:
            y = y * lax.rsqrt(jnp.mean(y * y, -1, keepdims=True) + EPS) * fg_ref[...]
        o_ref[...] = y


FFN_TF = 512


def ffn_sublayer(x, shift, scale, gate, g, wi_p, wo_p, l, s, rows_per_batch, final_g=None):
    m, d = x.shape
    fp = wo_p.shape[2]
    tf = FFN_TF
    nf = fp // tf
    tm = _row_tile(512, rows_per_batch)
    final = final_g is not None
    in_specs = [
        pl.BlockSpec((tm, d), lambda i, f: (i, 0)),
        _mod_spec(shift, tm, d, rows_per_batch, lambda f: 0),
        _mod_spec(scale, tm, d, rows_per_batch, lambda f: 0),
        _mod_spec(gate, tm, d, rows_per_batch, lambda f: 0),
        pl.BlockSpec((1, d), lambda i, f: (0, 0)),
        pl.BlockSpec((None, None, d, tf), lambda i, f: (l, s, 0, f)),
        pl.BlockSpec((None, None, d, tf), lambda i, f: (l, s, 0, nf + f)),
        pl.BlockSpec((None, None, tf, d), lambda i, f: (l, s, f, 0)),
    ]
    args = [x, shift, scale, gate, g.reshape(1, d), wi_p, wi_p, wo_p]
    if final:
        in_specs.append(pl.BlockSpec((1, d), lambda i, f: (0, 0)))
        args.append(final_g.reshape(1, d))
    return pl.pallas_call(
        functools.partial(_ffn_kernel, nf=nf, final=final),
        grid=(m // tm, nf),
        in_specs=in_specs,
        out_specs=pl.BlockSpec((tm, d), lambda i, f: (i, 0)),
        out_shape=jax.ShapeDtypeStruct((m, d), F32),
        scratch_shapes=[pltpu.VMEM((tm, d), BF16), pltpu.VMEM((tm, d), F32)],
        compiler_params=_cparams(("arbitrary", "arbitrary")),
        name="ffn",
    )(*args)


def _rope_tile(y, cos, sin, rope):
    if rope == 128:
        r = pltpu.roll(y, 64, 1)
    else:
        lane = lax.broadcasted_iota(I32, y.shape, 1)
        r = jnp.where((lane % 64) < 32, pltpu.roll(y, 96, 1), pltpu.roll(y, 32, 1))
    return y * cos + r * sin


LO = "bf16 residual"
HEADS = "bf16, one (rows, 128) slab per 128 output columns"
Q3 = "packed indexer query"
K3 = "packed indexer key"
_WIDTH = {Q3: 4, K3: 2}


def _emit(y, kind):
    if kind in (F32, BF16):
        return [y.astype(kind)]
    hi_b, lo_b = _split(y)
    if kind == LO:
        return [lo_b]
    hi, lo = hi_b.astype(F32), lo_b.astype(F32)
    first = lax.broadcasted_iota(I32, y.shape, 1) < 64
    if kind == K3:
        tiles = [jnp.where(first, hi, pltpu.roll(hi, 64, 1)), jnp.where(first, lo, 0.0)]
    else:
        hi_r = pltpu.roll(hi, 64, 1)
        tiles = [jnp.where(first, hi, pltpu.roll(lo, 64, 1)), jnp.where(first, hi, 0.0),
                 jnp.where(first, hi_r, lo), jnp.where(first, hi_r, 0.0)]
    return [t.astype(BF16) for t in tiles]


def _proj_kernel(*refs, rope, kinds, tn, precise):
    n_out = len(kinds)
    x_ref, sh_ref, sc_ref, g_ref, w_ref = refs[:5]
    pos = 5
    if rope:
        cos_ref, sin_ref = refs[5:7]
        pos = 7
    out_refs = refs[pos:pos + n_out]
    h_scr = refs[pos + n_out]
    j = pl.program_id(1)

    @pl.when(j == 0)
    def _():
        _norm_mod_to(h_scr, x_ref, g_ref, sc_ref, sh_ref)

    if precise:
        y = _dot3(h_scr[...], w_ref[...])
    else:
        y = _dot(h_scr[...], w_ref[...].astype(BF16))
    if rope:
        cos = cos_ref[...]
        sin = sin_ref[...]
    for c in range(tn // LANES):
        yc = y[:, c * LANES:(c + 1) * LANES]
        if rope:
            yc = _rope_tile(yc, cos, sin, rope)
        for o_ref, kind in zip(out_refs, kinds):
            if kind == HEADS:
                o_ref[c] = yc.astype(BF16)
                continue
            tiles = _emit(yc, kind)
            for k, tile in enumerate(tiles):
                at = (c * len(tiles) + k) * LANES
                o_ref[:, at:at + LANES] = tile


def proj(x, shift, scale, g, w, slot, col_off, n_cols, rows_per_batch, dtypes,
         rope=0, tables=None, precise=False):
    m, d = x.shape
    tn = min(512, n_cols)
    tm = _row_tile(1024, rows_per_batch)
    off = col_off // tn
    assert col_off % tn == 0 and n_cols % tn == 0
    in_specs = [
        pl.BlockSpec((tm, d), lambda i, j: (i, 0)),
        _mod_spec(shift, tm, d, rows_per_batch, lambda j: 0),
        _mod_spec(scale, tm, d, rows_per_batch, lambda j: 0),
        pl.BlockSpec((1, d), lambda i, j: (0, 0)),
        pl.BlockSpec((None, d, tn), lambda i, j: (slot, 0, off + j)),
    ]
    args = [x, shift, scale, g.reshape(1, d), w]
    if rope:
        cos, sin = tables
        nbt = max(cos.shape[0] // tm, 1)
        in_specs += [pl.BlockSpec((tm, LANES), lambda i, j: (i % nbt, 0))] * 2
        args += [cos, sin]
    out = pl.pallas_call(
        functools.partial(_proj_kernel, rope=rope, kinds=tuple(dtypes), tn=tn, precise=precise),
        grid=(m // tm, n_cols // tn),
        in_specs=in_specs,
        out_specs=[pl.BlockSpec((tn // LANES, tm, LANES), lambda i, j: (j, i, 0)) if dt == HEADS else
                   pl.BlockSpec((tm, tn * _WIDTH.get(dt, 1)), lambda i, j: (i, j)) for dt in dtypes],
        out_shape=[jax.ShapeDtypeStruct((n_cols // LANES, m, LANES), BF16) if dt == HEADS else
                   jax.ShapeDtypeStruct((m, n_cols * _WIDTH.get(dt, 1)), dt if dt in (F32, BF16) else BF16)
                   for dt in dtypes],
        scratch_shapes=[pltpu.VMEM((tm, d), F32 if precise else BF16)],
        compiler_params=_cparams(("arbitrary", "arbitrary")),
        name="proj",
    )(*args)
    return out


def _outproj_kernel(a_ref, x_ref, gt_ref, w_ref, o_ref):
    y = _dot(a_ref[...], w_ref[...].astype(BF16))
    o_ref[...] = x_ref[...] + gt_ref[0] * y


def outproj(a, x, gate, w, slot, rows_per_batch):
    m, d = x.shape
    k = a.shape[1]
    tn = 512
    tm = _row_tile(1024, rows_per_batch)
    return pl.pallas_call(
        _outproj_kernel,
        grid=(m // tm, d // tn),
        in_specs=[pl.BlockSpec((tm, k), lambda i, j: (i, 0)),
                  pl.BlockSpec((tm, tn), lambda i, j: (i, j)),
                  _mod_spec(gate, tm, tn, rows_per_batch, lambda j: j),
                  pl.BlockSpec((None, k, tn), lambda i, j: (slot, 0, j))],
        out_specs=pl.BlockSpec((tm, tn), lambda i, j: (i, j)),
        out_shape=jax.ShapeDtypeStruct((m, d), F32),
        compiler_params=_cparams(("arbitrary", "arbitrary")),
        name="outproj",
    )(a, x, gate, w)


def _mlstm_kernel(q_ref, k_ref, v_ref, o_ref, gates_ref, bias_ref, ng_ref,
                  hh_ref, c_ref, n_ref, m_ref, *, heads, dk, dv, chunk):
    c_idx = pl.program_id(1)

    @pl.when(c_idx == 0)
    def _():
        c_ref[...] = jnp.zeros_like(c_ref)
        n_ref[...] = jnp.zeros_like(n_ref)
        m_ref[...] = jnp.zeros_like(m_ref)

    L = chunk
    g = gates_ref[...] + bias_ref[...]
    gt = g.T
    lf = _log_sigmoid(g)
    lft = _log_sigmoid(gt)
    row = lax.broadcasted_iota(I32, (L, L), 0)
    col = lax.broadcasted_iota(I32, (L, L), 1)
    causal = col <= row
    qscale = dk ** -0.5
    for h in range(heads):
        icol = g[:, h:h + 1]
        fcol = lf[:, heads + h:heads + h + 1]
        irow = gt[h:h + 1, :]
        frow = lft[heads + h:heads + h + 1, :]
        b_col = jnp.sum(jnp.where(causal, frow, 0.0), axis=1, keepdims=True)
        b_row = jnp.sum(jnp.where(row <= col, fcol, 0.0), axis=0, keepdims=True)
        dmat = jnp.where(causal, b_col - b_row + irow, NEG)
        m_prev = m_ref[0, h:h + 1, 0:1]
        inter = b_col + m_prev
        m_t = jnp.maximum(inter, jnp.max(dmat, axis=1, keepdims=True))
        w_intra = jnp.exp(dmat - m_t)
        w_inter = jnp.exp(inter - m_t)
        qf = q_ref[:, h * dk:(h + 1) * dk] * qscale
        kf = k_ref[:, h * dk:(h + 1) * dk]
        vf = v_ref[:, h * dv:(h + 1) * dv]
        s = _dot3(qf, kf, _dot_nt) * w_intra
        c_prev = c_ref[0, h]
        n_prev = n_ref[0, h:h + 1, :]
        num = _dot3(s, vf) + w_inter * _dot3(qf, c_prev, _dot_nt)
        den = jnp.sum(s, axis=1, keepdims=True) + w_inter * jnp.sum(qf * n_prev, axis=1, keepdims=True)
        hh = num / jnp.maximum(jnp.abs(den), jnp.exp(-m_t))
        m_new = m_t[L - 1:L, :]
        b_last = b_col[L - 1:L, :]
        w_s = jnp.exp(b_last - b_col + icol - m_new)
        decay = jnp.exp(b_last + m_prev - m_new)
        vw_t = (vf * w_s).T
        c_ref[0, h] = decay * c_prev + _dot3(vw_t, kf)
        n_ref[0, h:h + 1, :] = decay * n_prev + jnp.sum(kf * w_s, axis=0, keepdims=True)
        m_ref[0, h:h + 1, :] = jnp.broadcast_to(m_new, (1, LANES))
        hn = hh * lax.rsqrt(jnp.mean(hh * hh, -1, keepdims=True) + EPS) * ng_ref[:, h * dv:(h + 1) * dv]
        hn = hn * jax.nn.sigmoid(o_ref[:, h * dv:(h + 1) * dv])
        hh_ref[:, h * dv:(h + 1) * dv] = hn.astype(BF16)


def mlstm_prompt(y, gates, b_if, norm_g, bn, t):
    heads = A_HEADS
    dv = norm_g.shape[-1]
    dk = dv // 2
    qk = heads * dk
    vw = heads * dv
    L = A_CHUNK
    nc = t // L
    bias = jnp.zeros((1, LANES), F32).at[0, :2 * heads].set(b_if)
    row = lambda b, c: b * nc + c
    outs = pl.pallas_call(
        functools.partial(_mlstm_kernel, heads=heads, dk=dk, dv=dv, chunk=L),
        grid=(bn, nc),
        in_specs=[pl.BlockSpec((L, qk), lambda b, c: (row(b, c), 0)),
                  pl.BlockSpec((L, qk), lambda b, c: (row(b, c), 1)),
                  pl.BlockSpec((L, vw), lambda b, c: (row(b, c), 1)),
                  pl.BlockSpec((L, vw), lambda b, c: (row(b, c), 2)),
                  pl.BlockSpec((L, LANES), lambda b, c: (row(b, c), 0)),
                  pl.BlockSpec((1, LANES), lambda b, c: (0, 0)),
                  pl.BlockSpec((1, vw), lambda b, c: (0, 0))],
        out_specs=[pl.BlockSpec((L, vw), lambda b, c: (row(b, c), 0)),
                   pl.BlockSpec((1, heads, dv, dk), lambda b, c: (b, 0, 0, 0)),
                   pl.BlockSpec((1, heads, dk), lambda b, c: (b, 0, 0)),
                   pl.BlockSpec((1, heads, LANES), lambda b, c: (b, 0, 0))],
        out_shape=[jax.ShapeDtypeStruct((bn * t, vw), BF16),
                   jax.ShapeDtypeStruct((bn, heads, dv, dk), F32),
                   jax.ShapeDtypeStruct((bn, heads, dk), F32),
                   jax.ShapeDtypeStruct((bn, heads, LANES), F32)],
        compiler_params=_cparams(("arbitrary", "arbitrary")),
        name="mlstm_scan",
    )(y, y, y, y, gates, bias, norm_g.reshape(1, vw))
    hh, c_fin, n_fin, m_fin = outs
    return hh, c_fin, n_fin, m_fin[:, :, 0]


def _mlstm_step_kernel(q_ref, k_ref, v_ref, o_ref, ig_ref, fg_ref, c0_ref, n0_ref, m0_ref, ng_ref,
                       h_ref, c_ref, n_ref, m_ref, *, dk):
    cst = c0_ref[0]
    n0 = n0_ref[0]
    m0 = m0_ref[0]
    q = q_ref[0] * dk ** -0.5
    k = k_ref[0]
    v = v_ref[0]
    ig = ig_ref[0]
    fc = _log_sigmoid(fg_ref[0])
    inter = fc + m0
    m_t = jnp.maximum(inter, ig)
    w_intra = jnp.exp(ig - m_t)
    w_inter = jnp.exp(inter - m_t)
    s = jnp.sum(q * k, axis=-1, keepdims=True) * w_intra
    cq = jnp.sum(cst * q, axis=-1, keepdims=True)
    num = s * v + w_inter * cq
    den = s + w_inter * jnp.sum(n0 * q, axis=-1, keepdims=True)
    hh = num / jnp.maximum(jnp.abs(den), jnp.exp(-m_t))
    w_s = w_intra
    decay = jnp.exp(inter - m_t)
    c_ref[0] = decay * cst + w_s * (v * k)
    n_ref[0] = decay * n0 + w_s * k
    m_ref[0] = m_t
    hn = hh * lax.rsqrt(jnp.mean(hh * hh, axis=1, keepdims=True) + EPS) * ng_ref[...]
    h_ref[0] = hn * jax.nn.sigmoid(o_ref[0])


def mlstm_sample(y, gates, b_if, norm_g, c0, n0, m0, bs):
    heads = A_HEADS
    dv = norm_g.shape[-1]
    dk = dv // 2
    qk = heads * dk
    vw = heads * dv
    y = y[:bs]
    q = y[:, :qk].reshape(bs, heads, 1, dk)
    k = y[:, qk:2 * qk].reshape(bs, heads, 1, dk)
    v = y[:, 2 * qk:2 * qk + vw].reshape(bs, heads, dv, 1)
    o = y[:, 2 * qk + vw:].reshape(bs, heads, dv, 1)
    gb = gates[:bs, :2 * heads] + b_if
    ig = gb[:, :heads].reshape(bs, heads, 1, 1)
    fg = gb[:, heads:].reshape(bs, heads, 1, 1)
    spec = lambda *shape: pl.BlockSpec((1,) + shape, lambda b: (b,) + (0,) * len(shape))
    outs = pl.pallas_call(
        functools.partial(_mlstm_step_kernel, dk=dk),
        grid=(bs,),
        in_specs=[spec(heads, 1, dk), spec(heads, 1, dk), spec(heads, dv, 1), spec(heads, dv, 1),
                  spec(heads, 1, 1), spec(heads, 1, 1),
                  spec(heads, dv, dk), spec(heads, 1, dk), spec(heads, 1, 1),
                  pl.BlockSpec((heads, dv, 1), lambda b: (0, 0, 0))],
        out_specs=[spec(heads, dv, 1), spec(heads, dv, dk), spec(heads, 1, dk), spec(heads, 1, 1)],
        out_shape=[jax.ShapeDtypeStruct((bs, heads, dv, 1), F32),
                   jax.ShapeDtypeStruct((bs, heads, dv, dk), F32),
                   jax.ShapeDtypeStruct((bs, heads, 1, dk), F32),
                   jax.ShapeDtypeStruct((bs, heads, 1, 1), F32)],
        compiler_params=_cparams(("arbitrary",)),
        name="mlstm_step",
    )(q, k, v, o, ig, fg, c0, n0.reshape(bs, heads, 1, dk), m0.reshape(bs, heads, 1, 1),
      norm_g.reshape(heads, dv, 1))
    hh, c_new, n_new, m_new = outs
    return hh.reshape(bs, vw), c_new, n_new.reshape(bs, heads, dk), m_new.reshape(bs, heads)


def _diff_lambda(lam_ref, lam_init):
    l = lam_ref[...]
    a = jnp.sum(l[0:1, :] * l[1:2, :], axis=1, keepdims=True)
    b = jnp.sum(l[2:3, :] * l[3:4, :], axis=1, keepdims=True)
    return jnp.exp(a) - jnp.exp(b) + lam_init


def _flash_step(s, vs, m_scr, l_scr, acc_scr, c):
    m_prev = m_scr[c]
    m_new = jnp.maximum(m_prev, jnp.max(s, axis=1, keepdims=True))
    alpha = jnp.exp(m_prev - m_new)
    p = jnp.exp(s - m_new)
    l_scr[c] = alpha * l_scr[c] + jnp.sum(p, axis=1, keepdims=True)
    acc_scr[c] = alpha * acc_scr[c] + _dot(p.astype(BF16), vs)
    m_scr[c] = m_new


def _diff_prompt_kernel(q_ref, k_ref, v_ref, lam_ref, ng_ref, o_ref, m_scr, l_scr, acc_scr,
                        *, tq, tk, dk, lam_init):
    i = pl.program_id(2)
    scale = dk ** -0.5
    m_scr[...] = jnp.full_like(m_scr, NEG)
    l_scr[...] = jnp.zeros_like(l_scr)
    acc_scr[...] = jnp.zeros_like(acc_scr)

    def chunk(j, masked):
        start = pl.multiple_of(j * tk, tk)
        vs = v_ref[pl.ds(start, tk), :]
        for c in range(2):
            s = _dot_nt(q_ref[:, c * dk:(c + 1) * dk], k_ref[pl.ds(start, tk), c * dk:(c + 1) * dk]) * scale
            if masked:
                row = i * tq + lax.broadcasted_iota(I32, (tq, tk), 0)
                col = start + lax.broadcasted_iota(I32, (tq, tk), 1)
                s = jnp.where(col <= row, s, NEG)
            _flash_step(s, vs, m_scr, l_scr, acc_scr, c)

    def body(j, carry):
        chunk(j, False)
        return carry

    n_full = (i * tq) // tk
    lax.fori_loop(0, n_full, body, 0)
    chunk(n_full, True)
    lam = _diff_lambda(lam_ref, lam_init)
    o = acc_scr[0] / l_scr[0] - lam * (acc_scr[1] / l_scr[1])
    o = o * lax.rsqrt(jnp.mean(o * o, -1, keepdims=True) + EPS) * ng_ref[...] * (1.0 - lam_init)
    o_ref[...] = o.astype(o_ref.dtype)


def diff_prompt(qb, kb, vb, lam, norm_g, lam_init, bn, t):
    heads = B_HEADS
    dv = norm_g.shape[-1]
    dk = dv // 2
    tq = min(256, t)
    tk = min(1024, t)
    nq = t // tq
    return pl.pallas_call(
        functools.partial(_diff_prompt_kernel, tq=tq, tk=tk, dk=dk, lam_init=lam_init),
        grid=(bn, heads, nq),
        in_specs=[pl.BlockSpec((tq, 2 * dk), lambda b, h, i: (b * nq + i, h)),
                  pl.BlockSpec((t, 2 * dk), lambda b, h, i: (b, h)),
                  pl.BlockSpec((t, dv), lambda b, h, i: (b, h)),
                  pl.BlockSpec(lam.shape, lambda b, h, i: (0, 0)),
                  pl.BlockSpec((1, dv), lambda b, h, i: (0, h))],
        out_specs=pl.BlockSpec((tq, dv), lambda b, h, i: (b * nq + i, h)),
        out_shape=jax.ShapeDtypeStruct((bn * t, heads * dv), BF16),
        scratch_shapes=[pltpu.VMEM((2, tq, 1), F32), pltpu.VMEM((2, tq, 1), F32),
                        pltpu.VMEM((2, tq, dv), F32)],
        compiler_params=_cparams(("arbitrary", "arbitrary", "arbitrary")),
        name="diff_prompt",
    )(qb, kb, vb, lam, norm_g.reshape(1, heads * dv))


def _paged_kernel(pt_ref, q_ref, kn_ref, vn_ref, *rest, mode, group, n_steps, pps, lam_init):
    k_refs, v_refs, rest = rest[:pps], rest[pps:2 * pps], rest[2 * pps:]
    if mode == "diff":
        lam_ref, ng_ref, o_ref, m_scr, l_scr, acc_scr = rest
    else:
        bias_ref, biasn_ref, o_ref, exp_scr, m_scr, l_scr, acc_scr = rest
    p = pl.program_id(1)
    rows, dk = q_ref.shape[1:]
    ncol = PAGE_SIZE * group
    scale = dk ** -0.5

    @pl.when(p == 0)
    def _():
        m_scr[...] = jnp.full_like(m_scr, NEG)
        l_scr[...] = jnp.zeros_like(l_scr)
        acc_scr[...] = jnp.zeros_like(acc_scr)
        if mode == "dsa":
            t = lax.broadcasted_iota(I32, (PAGE_SIZE, ncol), 0)
            c = lax.broadcasted_iota(I32, (PAGE_SIZE, ncol), 1)
            exp_scr[...] = jnp.where(c // group == t, 1.0, 0.0).astype(BF16)

    q = q_ref[0]
    qb = q.astype(BF16)

    def page_scores(k_ref):
        if mode == "diff":
            half = rows // 2
            return jnp.concatenate(
                [_dot_nt(qb[c * half:(c + 1) * half], k_ref[pl.ds(c, ncol, stride=2), :].astype(BF16))
                 for c in range(2)], axis=0)
        return _dot_nt(qb, k_ref[...].astype(BF16))

    s = jnp.concatenate([page_scores(k_ref) for k_ref in k_refs], axis=1)
    r_i = lax.broadcasted_iota(I32, s.shape, 0)
    c_i = lax.broadcasted_iota(I32, s.shape, 1)
    keep = (c_i % group) == (r_i % group)
    if mode == "dsa":
        sel = jnp.where(bias_ref[0, 0] == 0.0, 1.0, 0.0).astype(BF16)
        sel = jnp.concatenate(
            [_dot(jnp.broadcast_to(sel[pp:pp + 1], (8, PAGE_SIZE)), exp_scr[...])[0:1] for pp in range(pps)],
            axis=1)
        keep = keep & (sel > 0.5)
    s = jnp.where(keep, s * scale, NEG)
    m_prev = m_scr[...]
    m_new = jnp.maximum(m_prev, jnp.max(s, axis=1, keepdims=True))
    alpha = jnp.exp(m_prev - m_new)
    pe = jnp.exp(s - m_new)
    l_scr[...] = alpha * l_scr[...] + jnp.sum(pe, axis=1, keepdims=True)
    pv = _dot(pe[:, :ncol].astype(BF16), v_refs[0][...].astype(BF16))
    for pp in range(1, pps):
        pv = pv + _dot(pe[:, pp * ncol:(pp + 1) * ncol].astype(BF16), v_refs[pp][...].astype(BF16))
    acc_scr[...] = alpha * acc_scr[...] + pv
    m_scr[...] = m_new

    @pl.when(p == n_steps - 1)
    def _():
        s_n = jnp.sum(q * kn_ref[0], axis=1, keepdims=True) * scale
        v_n = vn_ref[0]
        if mode == "dsa":
            s_n = s_n + biasn_ref[0][:, 0:1]
        else:
            v_n = jnp.concatenate([v_n, v_n], axis=0)
        m_prev = m_scr[...]
        m_new = jnp.maximum(m_prev, s_n)
        alpha = jnp.exp(m_prev - m_new)
        pn = jnp.exp(s_n - m_new)
        l_fin = alpha * l_scr[...] + pn
        acc = (alpha * acc_scr[...] + pn * v_n) / l_fin
        if mode == "diff":
            half = rows // 2
            o = acc[:half] - _diff_lambda(lam_ref, lam_init) * acc[half:]
            o = o * lax.rsqrt(jnp.mean(o * o, -1, keepdims=True) + EPS) * ng_ref[...]
            o_ref[0] = o * (1.0 - lam_init)
        else:
            o_ref[0] = acc


def paged_attention(mode, q, k_new, v_new, cache_k, cache_v, slot, page_table, extra, lam_init=0.0):
    bs, n_pages = page_table.shape
    rows, dk = q.shape[1:]
    hv, dv = v_new.shape[1:]
    group = hv
    ncol = PAGE_SIZE * group
    pps = max(c for c in (4, 2, 1) if n_pages % c == 0)
    n_steps = n_pages // pps
    seq_spec = lambda a: pl.BlockSpec((1,) + a.shape[1:], lambda b, p, pt: (b, 0, 0))

    def page_specs(a):
        return [pl.BlockSpec((None, None) + a.shape[2:],
                             lambda b, p, pt, pp=pp: (slot, pt[b, p * pps + pp], 0, 0)) for pp in range(pps)]

    scratch = [pltpu.VMEM((rows, 1), F32), pltpu.VMEM((rows, 1), F32), pltpu.VMEM((rows, dv), F32)]
    if mode == "diff":
        lam, norm_g = extra
        extra_specs = [pl.BlockSpec(lam.shape, lambda b, p, pt: (0, 0)),
                       pl.BlockSpec(norm_g.shape, lambda b, p, pt: (0, 0))]
        extra_args = [lam, norm_g]
    else:
        bias, bias_new = extra
        bias = bias.reshape(bs, n_steps, pps, PAGE_SIZE)
        extra_specs = [pl.BlockSpec((1, 1, pps, PAGE_SIZE), lambda b, p, pt: (b, p, 0, 0)),
                       pl.BlockSpec((1, 1, LANES), lambda b, p, pt: (b, 0, 0))]
        extra_args = [bias, bias_new]
        scratch = [pltpu.VMEM((PAGE_SIZE, ncol), BF16)] + scratch
    grid_spec = pltpu.PrefetchScalarGridSpec(
        num_scalar_prefetch=1,
        grid=(bs, n_steps),
        in_specs=[seq_spec(q), seq_spec(k_new), seq_spec(v_new)] + page_specs(cache_k)
        + page_specs(cache_v) + extra_specs,
        out_specs=pl.BlockSpec((1, hv, dv), lambda b, p, pt: (b, 0, 0)),
        scratch_shapes=scratch,
    )
    return pl.pallas_call(
        functools.partial(_paged_kernel, mode=mode, group=group, n_steps=n_steps, pps=pps,
                          lam_init=lam_init),
        grid_spec=grid_spec,
        out_shape=jax.ShapeDtypeStruct((bs, hv, dv), F32),
        compiler_params=_cparams(("arbitrary", "arbitrary")),
        name="paged_" + mode,
    )(page_table, q, k_new, v_new, *([cache_k] * pps), *([cache_v] * pps), *extra_args)


def _sortable_key(s):
    bits = lax.bitcast_convert_type(s, I32)
    return bits ^ ((bits >> 31) & 0x7FFFFFFF)


def _bisect_threshold(count_ge, shapes, n_top):
    def step(it, t_us):
        bit = jnp.left_shift(jnp.int32(1), 31 - it)
        cands = [t_u | bit for t_u in t_us]
        cnts = count_ge([c ^ INT_MIN for c in cands])
        return tuple(jnp.where(cnt >= n_top, c, t_u) for cnt, c, t_u in zip(cnts, cands, t_us))

    t_us = lax.fori_loop(0, 32, step, tuple(jnp.zeros(shape, I32) for shape in shapes))
    return [jnp.maximum(t_u ^ INT_MIN, INT_MIN + 1) for t_u in t_us]


def _dsa_prompt_kernel(qi_ref, wt_ref, ki_ref, q_ref, k_ref, v_ref, o_ref,
                       key_scr, bias_scr, wb_scr, m_scr, l_scr, acc_scr, *, tq, tk, dh, n_top, wcol):
    i = pl.program_id(1)
    h = pl.program_id(2)
    scale = dh ** -0.5
    idx_scale = IDX_DIM ** -0.5 * IDX_HEADS ** -0.5
    groups = tq // LANES
    n_att = (i * tq) // tk + 1

    @pl.when(h == 0)
    def _():
        w = wt_ref[...] * idx_scale
        for hh in range(IDX_HEADS):
            wb_scr[hh] = jnp.broadcast_to(w[:, wcol + hh:wcol + hh + 1], (tq, LANES))

        def score_chunk(j, masked):
            start = pl.multiple_of(j * tq, tq)
            kc = ki_ref[pl.ds(start, tq), :]
            acc = [jnp.zeros((tq, LANES), F32) for _ in range(groups)]
            for hh in range(IDX_HEADS):
                s = _dot_nt(qi_ref[:, hh * IDX_PACK:(hh + 1) * IDX_PACK], kc)
                wb = wb_scr[hh]
                for c in range(groups):
                    acc[c] = acc[c] + jnp.maximum(s[:, c * LANES:(c + 1) * LANES], 0.0) * wb
            for c in range(groups):
                key = _sortable_key(acc[c])
                if masked:
                    row = lax.broadcasted_iota(I32, (tq, LANES), 0)
                    col = lax.broadcasted_iota(I32, (tq, LANES), 1) + c * LANES
                    key = jnp.where(col <= row, key, INT_MIN)
                key_scr[:, pl.ds(pl.multiple_of(start + c * LANES, LANES), LANES)] = key

        def score_body(j, carry):
            score_chunk(j, False)
            return carry

        lax.fori_loop(0, i, score_body, 0)
        score_chunk(i, True)

        row_blocks = [pl.ds(rb * BISECT_ROWS, BISECT_ROWS) for rb in range(tq // BISECT_ROWS)]

        def count_ge(cands):
            parts = []
            for rows, cand in zip(row_blocks, cands):
                cand_b = jnp.broadcast_to(cand, (BISECT_ROWS, LANES))

                def cbody(j, part, rows=rows, cand_b=cand_b):
                    start = pl.multiple_of(j * tq, tq)
                    for c in range(groups):
                        kk = key_scr[rows, pl.ds(start + c * LANES, LANES)]
                        part = part + jnp.where(kk >= cand_b, 1.0, 0.0)
                    return part

                parts.append(lax.fori_loop(0, i + 1, cbody, jnp.zeros((BISECT_ROWS, LANES), F32)))
            return [jnp.sum(part, axis=1, keepdims=True) for part in parts]

        thrs = _bisect_threshold(count_ge, [(BISECT_ROWS, 1)] * len(row_blocks), n_top)
        for rows, thr in zip(row_blocks, thrs):
            thr = jnp.broadcast_to(thr, (BISECT_ROWS, LANES))

            def bias_body(j, carry, rows=rows, thr=thr):
                start = pl.multiple_of(j * LANES, LANES)
                kk = key_scr[rows, pl.ds(start, LANES)]
                bias_scr[rows, pl.ds(start, LANES)] = jnp.where(kk >= thr, 0.0, NEG)
                return carry

            lax.fori_loop(0, (i + 1) * groups, bias_body, 0)

        def fill_body(j, carry):
            start = pl.multiple_of(j * LANES, LANES)
            bias_scr[:, pl.ds(start, LANES)] = jnp.full((tq, LANES), NEG, F32)
            return carry

        lax.fori_loop((i + 1) * groups, n_att * (tk // LANES), fill_body, 0)

    m_scr[...] = jnp.full_like(m_scr, NEG)
    l_scr[...] = jnp.zeros_like(l_scr)
    acc_scr[...] = jnp.zeros_like(acc_scr)
    qh = q_ref[...]

    def att_body(j, carry):
        start = pl.multiple_of(j * tk, tk)
        s = _dot_nt(qh, k_ref[pl.ds(start, tk), :]) * scale + bias_scr[:, pl.ds(start, tk)]
        _flash_step(s, v_ref[pl.ds(start, tk), :], m_scr, l_scr, acc_scr, 0)
        return carry

    lax.fori_loop(0, n_att, att_body, 0)
    o_ref[...] = (acc_scr[0] / l_scr[0]).astype(o_ref.dtype)


def dsa_prompt(qi3, tail, ki3, qb, kb, vb, bn, t, wcol):
    heads, _, dh = qb.shape
    tq = min(256, t)
    tk = min(1024, t)
    nq = t // tq
    n_top = min(TOPK_MAX, t // 4)
    return pl.pallas_call(
        functools.partial(_dsa_prompt_kernel, tq=tq, tk=tk, dh=dh, n_top=n_top, wcol=wcol),
        grid=(bn, nq, heads),
        in_specs=[pl.BlockSpec((tq, IDX_HEADS * IDX_PACK), lambda b, i, h: (b * nq + i, 0)),
                  pl.BlockSpec((tq, LANES), lambda b, i, h: (b * nq + i, 0)),
                  pl.BlockSpec((t, IDX_PACK), lambda b, i, h: (b, 0)),
                  pl.BlockSpec((None, tq, dh), lambda b, i, h: (h, b * nq + i, 0)),
                  pl.BlockSpec((None, t, dh), lambda b, i, h: (h, b, 0)),
                  pl.BlockSpec((None, t, dh), lambda b, i, h: (h, b, 0))],
        out_specs=pl.BlockSpec((tq, dh), lambda b, i, h: (b * nq + i, h)),
        out_shape=jax.ShapeDtypeStruct((bn * t, heads * dh), BF16),
        scratch_shapes=[pltpu.VMEM((tq, t), I32), pltpu.VMEM((tq, t), F32),
                        pltpu.VMEM((IDX_HEADS, tq, LANES), F32),
                        pltpu.VMEM((1, tq, 1), F32), pltpu.VMEM((1, tq, 1), F32),
                        pltpu.VMEM((1, tq, dh), F32)],
        compiler_params=_cparams(("arbitrary", "arbitrary", "arbitrary")),
        name="dsa_prompt",
    )(qi3, tail, ki3, qb, kb, vb)


def _dsa_index_kernel(pt_ref, qi_ref, w_ref, kin_ref, *rest, n_steps, pps, n_top):
    kp_refs, (bias_ref, biasn_ref, sc_scr) = rest[:pps], rest[pps:]
    p = pl.program_id(1)
    idx_scale = IDX_DIM ** -0.5 * IDX_HEADS ** -0.5
    qi = qi_ref[0]
    w = w_ref[0] * idx_scale
    for pp, kp_ref in enumerate(kp_refs):
        s = _dot3(qi, kp_ref[...], _dot_nt)
        sc_scr[pl.ds(p * pps + pp, 1), :] = jnp.sum(jnp.maximum(s, 0.0) * w, axis=0, keepdims=True)

    @pl.when(p == n_steps - 1)
    def _():
        s_n = jnp.sum(qi * kin_ref[0], axis=1, keepdims=True)
        sc_new = jnp.sum(jnp.maximum(s_n, 0.0) * w, axis=0, keepdims=True)
        key = _sortable_key(sc_scr[...])
        key_new = _sortable_key(sc_new)

        def count_ge(cands):
            (cand,) = cands
            hit = jnp.where(key >= cand, 1.0, 0.0)
            cnt = jnp.sum(jnp.sum(hit, axis=1, keepdims=True), axis=0, keepdims=True)
            return [cnt + jnp.where(key_new >= cand, 1.0, 0.0)]

        (thr,) = _bisect_threshold(count_ge, [(1, 1)], n_top)
        bias_ref[0] = jnp.where(key >= thr, 0.0, NEG)
        biasn_ref[0] = jnp.broadcast_to(jnp.where(key_new >= thr, 0.0, NEG), (1, LANES))


def dsa_sample_index(qi, w, ki_new, cache_ki, slot, page_table):
    bs, n_pages = page_table.shape
    n_top = min(TOPK_MAX, (n_pages * PAGE_SIZE + 1) // 4)
    pps = max(c for c in (16, 8, 4, 2, 1) if n_pages % c == 0)
    n_steps = n_pages // pps
    grid_spec = pltpu.PrefetchScalarGridSpec(
        num_scalar_prefetch=1,
        grid=(bs, n_steps),
        in_specs=[pl.BlockSpec((1, IDX_HEADS, IDX_DIM), lambda b, p, pt: (b, 0, 0)),
                  pl.BlockSpec((1, IDX_HEADS, 1), lambda b, p, pt: (b, 0, 0)),
                  pl.BlockSpec((1, 1, IDX_DIM), lambda b, p, pt: (b, 0, 0))]
        + [pl.BlockSpec((None, None, PAGE_SIZE, IDX_DIM),
                        lambda b, p, pt, pp=pp: (slot, pt[b, p * pps + pp], 0, 0)) for pp in range(pps)],
        out_specs=[pl.BlockSpec((1, n_pages, PAGE_SIZE), lambda b, p, pt: (b, 0, 0)),
                   pl.BlockSpec((1, 1, LANES), lambda b, p, pt: (b, 0, 0))],
        scratch_shapes=[pltpu.VMEM((n_pages, PAGE_SIZE), F32)],
    )
    bias, bias_new = pl.pallas_call(
        functools.partial(_dsa_index_kernel, n_steps=n_steps, pps=pps, n_top=n_top),
        grid_spec=grid_spec,
        out_shape=[jax.ShapeDtypeStruct((bs, n_pages, PAGE_SIZE), F32),
                   jax.ShapeDtypeStruct((bs, 1, LANES), F32)],
        compiler_params=_cparams(("arbitrary", "arbitrary")),
        name="dsa_index",
    )(page_table, qi, w, ki_new, *([cache_ki] * pps))
    return bias, bias_new


def _rope_tables(pos, half):
    inv = ROPE_THETA ** (-jnp.arange(half, dtype=F32) / half)
    ang = pos.astype(F32)[:, None] * inv[None, :]
    return jnp.cos(ang), jnp.sin(ang)


def _tables_128(pos):
    c, s = _rope_tables(pos, 64)
    return jnp.concatenate([c, c], -1), jnp.concatenate([-s, s], -1)


def _tables_64(pos, tail):
    c, s = _rope_tables(pos, 32)
    if tail:
        one, zero = jnp.ones_like(c), jnp.zeros_like(s)
        return jnp.concatenate([c, c, one, one], -1), jnp.concatenate([-s, s, zero, zero], -1)
    return jnp.concatenate([c, c, c, c], -1), jnp.concatenate([-s, s, -s, s], -1)


def _pad_cols(w, n):
    return jnp.pad(w, ((0, 0), (0, 0), (0, n - w.shape[-1])))


def kernel(x_prompt, x_sample, c_prompt, c_sample, state_a_C, state_a_n, state_a_m, cache_b_k,
           cache_b_v, cache_c_k, cache_c_v, cache_c_kidx, page_table, ada_w, ada_b, norm_g, ffn_wi,
           ffn_wo, a_w_in, a_b_if, a_norm_g, a_w_out, b_w_in, b_lambda, b_norm_g, b_w_out, c_w_in,
           c_w_out, final_g):
    bp, t_p, d = x_prompt.shape
    bs = x_sample.shape[0]
    depth = ada_w.shape[0]
    n_pages = page_table.shape[1]
    past = n_pages * PAGE_SIZE
    n_pool = cache_b_k.shape[1]
    rs = SUBLANES_BF16 * ((bs + SUBLANES_BF16 - 1) // SUBLANES_BF16)
    mp = bp * t_p

    ff = ffn_wo.shape[2]
    fp = FFN_TF * ((ff + FFN_TF - 1) // FFN_TF)
    wi_p, wo_p = stage_ffn_weights(ffn_wi, ffn_wo, fp)

    c_rows = jnp.zeros((rs, d), F32).at[:bp].set(c_prompt).at[bp:bp + bs].set(c_sample)
    mod = ada_all(c_rows, ada_w, ada_b).reshape(depth, rs, N_ADA, d)

    xp = x_prompt.reshape(mp, d)
    xs = jnp.zeros((rs, d), F32).at[:bs].set(x_sample.reshape(bs, d))

    pos_p = jnp.arange(t_p, dtype=jnp.int32)
    pos_s = jnp.full((rs,), past, jnp.int32)
    tab128_p, tab128_s = _tables_128(pos_p), _tables_128(pos_s)
    tab64_p, tab64_s = _tables_64(pos_p, False), _tables_64(pos_s, False)
    tabt_p, tabt_s = _tables_64(pos_p, True), _tables_64(pos_s, True)

    a_main = 2 * A_HEADS * (a_norm_g.shape[-1] // 2) + 2 * A_HEADS * a_norm_g.shape[-1]
    a_tail_w = _pad_cols(a_w_in[:, :, a_main:], LANES)
    c_main = 3 * d + IDX_HEADS * IDX_DIM
    c_tail_w = _pad_cols(c_w_in[:, :, c_main:], LANES)
    a_w_in, b_w_in, c_w_in = (w.astype(BF16) for w in (a_w_in, b_w_in, c_w_in))
    a_w_out, b_w_out, c_w_out = (w.astype(BF16) for w in (a_w_out, b_w_out, c_w_out))

    kind_of = [i % 3 for i in range(depth)]
    outs = {k: [] for k in ("aCp", "anp", "amp", "aCs", "ans", "ams", "bkp", "bvp", "bks", "bvs",
                            "ckp", "cvp", "cip", "cks", "cvs", "cis")}
    for l in range(depth):
        kind = kind_of[l]
        j = kind_of[:l].count(kind)
        mp_l = [mod[l, :bp, k][:, None, :] for k in range(N_ADA)]
        ms_l = [mod[l, bp:bp + bs, k] for k in range(N_ADA)]
        ms_l = [jnp.zeros((1, rs, d), F32).at[0, :bs].set(m) for m in ms_l]

        xp = ffn_sublayer(xp, mp_l[0], mp_l[1], mp_l[2], norm_g[l, 0], wi_p, wo_p, l, 0, t_p)
        xs = ffn_sublayer(xs, ms_l[0], ms_l[1], ms_l[2], norm_g[l, 0], wi_p, wo_p, l, 0, rs)
        pj_p = functools.partial(proj, xp, mp_l[3], mp_l[4], norm_g[l, 1], rows_per_batch=t_p)
        pj_s = functools.partial(proj, xs, ms_l[3], ms_l[4], norm_g[l, 1], rows_per_batch=rs)

        if kind == 0:
            (y_p,) = pj_p(a_w_in, j, 0, a_main, dtypes=(F32,))
            (g_p,) = pj_p(a_tail_w, j, 0, LANES, dtypes=(F32,), precise=True)
            hh_p, c_p, n_p, m_p = mlstm_prompt(y_p, g_p, a_b_if[j], a_norm_g[j], bp, t_p)
            (y_s,) = pj_s(a_w_in, j, 0, a_main, dtypes=(F32,))
            (g_s,) = pj_s(a_tail_w, j, 0, LANES, dtypes=(F32,), precise=True)
            hh_s, c_s, n_s, m_s = mlstm_sample(y_s, g_s, a_b_if[j], a_norm_g[j],
                                               state_a_C[j], state_a_n[j], state_a_m[j], bs)
            op_in = hh_p
            os_in = jnp.zeros((rs, hh_s.shape[1]), BF16).at[:bs].set(hh_s.astype(BF16))
            w_out = a_w_out
            outs["aCp"].append(c_p); outs["anp"].append(n_p); outs["amp"].append(m_p)
            outs["aCs"].append(c_s); outs["ans"].append(n_s); outs["ams"].append(m_s)
        elif kind == 1:
            lam_init = 0.8 - 0.6 * math.exp(-0.3 * l)
            bqk = b_w_in.shape[2] // 3
            (q_pb,) = pj_p(b_w_in, j, 0, bqk, dtypes=(BF16,), rope=128, tables=tab128_p)
            k_p, k_pb = pj_p(b_w_in, j, bqk, bqk, dtypes=(F32, BF16), rope=128, tables=tab128_p)
            v_p, v_pb = pj_p(b_w_in, j, 2 * bqk, bqk, dtypes=(F32, BF16))
            op_in = diff_prompt(q_pb, k_pb, v_pb, b_lambda[j], b_norm_g[j], lam_init, bp, t_p)
            (q_s,) = pj_s(b_w_in, j, 0, bqk, dtypes=(F32,), rope=128, tables=tab128_s)
            (k_s,) = pj_s(b_w_in, j, bqk, bqk, dtypes=(F32,), rope=128, tables=tab128_s)
            (v_s,) = pj_s(b_w_in, j, 2 * bqk, bqk, dtypes=(F32,))
            comp_major = lambda a: a[:bs].reshape(bs, B_HEADS, 2, -1).transpose(0, 2, 1, 3).reshape(
                bs, 2 * B_HEADS, -1)
            o_s = paged_attention(
                "diff", comp_major(q_s), comp_major(k_s), v_s[:bs].reshape(bs, B_HEADS, -1),
                cache_b_k.reshape(cache_b_k.shape[0], n_pool, PAGE_SIZE * B_HEADS * 2, -1),
                cache_b_v.reshape(cache_b_v.shape[0], n_pool, PAGE_SIZE * B_HEADS, -1),
                j, page_table, (b_lambda[j], b_norm_g[j]), lam_init)
            os_in = jnp.zeros((rs, bqk), BF16).at[:bs].set(o_s.reshape(bs, bqk).astype(BF16))
            w_out = b_w_out
            outs["bkp"].append(k_p.reshape(bp, t_p, B_HEADS, 2, -1))
            outs["bvp"].append(v_p.reshape(bp, t_p, B_HEADS, -1))
            outs["bks"].append(k_s[:bs].reshape(bs, 1, B_HEADS, 2, -1))
            outs["bvs"].append(v_s[:bs].reshape(bs, 1, B_HEADS, -1))
        else:
            cw = d
            ni = IDX_HEADS * IDX_DIM
            (q_pb,) = pj_p(c_w_in, j, 0, cw, dtypes=(HEADS,), rope=128, tables=tab128_p)
            k_p, k_pb = pj_p(c_w_in, j, cw, cw, dtypes=(F32, HEADS), rope=128, tables=tab128_p)
            v_p, v_pb = pj_p(c_w_in, j, 2 * cw, cw, dtypes=(F32, HEADS))
            (qi_p3,) = pj_p(c_w_in, j, 3 * cw, ni, dtypes=(Q3,), rope=64, tables=tab64_p)
            t_pf, ki_p3 = pj_p(c_tail_w, j, 0, LANES, dtypes=(F32, K3), rope=64, tables=tabt_p)
            op_in = dsa_prompt(qi_p3, t_pf, ki_p3, q_pb, k_pb, v_pb, bp, t_p, IDX_DIM)
            (q_s,) = pj_s(c_w_in, j, 0, cw, dtypes=(F32,), rope=128, tables=tab128_s)
            (k_s,) = pj_s(c_w_in, j, cw, cw, dtypes=(F32,), rope=128, tables=tab128_s)
            (v_s,) = pj_s(c_w_in, j, 2 * cw, cw, dtypes=(F32,))
            (qi_s,) = pj_s(c_w_in, j, 3 * cw, ni, dtypes=(F32,), rope=64, tables=tab64_s)
            (t_s,) = pj_s(c_tail_w, j, 0, LANES, dtypes=(F32,), rope=64, tables=tabt_s)
            ki_s = t_s[:bs, :IDX_DIM]
            bias, bias_new = dsa_sample_index(
                qi_s[:bs].reshape(bs, IDX_HEADS, IDX_DIM),
                t_s[:bs, IDX_DIM:IDX_DIM + IDX_HEADS].reshape(bs, IDX_HEADS, 1),
                ki_s[:, None], cache_c_kidx, j, page_table)
            o_s = paged_attention(
                "dsa", q_s[:bs].reshape(bs, C_HEADS, -1), k_s[:bs].reshape(bs, C_HEADS, -1),
                v_s[:bs].reshape(bs, C_HEADS, -1),
                cache_c_k.reshape(cache_c_k.shape[0], n_pool, PAGE_SIZE * C_HEADS, -1),
                cache_c_v.reshape(cache_c_v.shape[0], n_pool, PAGE_SIZE * C_HEADS, -1),
                j, page_table, (bias, bias_new))
            os_in = jnp.zeros((rs, cw), BF16).at[:bs].set(o_s.reshape(bs, cw).astype(BF16))
            w_out = c_w_out
            outs["ckp"].append(k_p.reshape(bp, t_p, C_HEADS, -1))
            outs["cvp"].append(v_p.reshape(bp, t_p, C_HEADS, -1))
            outs["cip"].append(t_pf[:, :IDX_DIM].reshape(bp, t_p, IDX_DIM))
            outs["cks"].append(k_s[:bs].reshape(bs, 1, C_HEADS, -1))
            outs["cvs"].append(v_s[:bs].reshape(bs, 1, C_HEADS, -1))
            outs["cis"].append(ki_s.reshape(bs, 1, IDX_DIM))

        xp = outproj(op_in, xp, mp_l[5], w_out, j, t_p)
        xs = outproj(os_in, xs, ms_l[5], w_out, j, rs)
        fin = final_g if l == depth - 1 else None
        xp = ffn_sublayer(xp, mp_l[6], mp_l[7], mp_l[8], norm_g[l, 2], wi_p, wo_p, l, 1, t_p, fin)
        xs = ffn_sublayer(xs, ms_l[6], ms_l[7], ms_l[8], norm_g[l, 2], wi_p, wo_p, l, 1, rs, fin)

    st = lambda k: jnp.stack(outs[k])
    return (xp.reshape(bp, t_p, d), xs[:bs].reshape(bs, 1, d),
            st("aCp"), st("anp"), st("amp"), st("aCs"), st("ans"), st("ams"),
            st("bkp"), st("bvp"), st("bks"), st("bvs"),
            st("ckp"), st("cvp"), st("cip"), st("cks"), st("cvs"), st("cis"))
```

```python
import functools
import math

import jax
import jax.numpy as jnp
from jax import lax
from jax.experimental import pallas as pl
from jax.experimental.pallas import tpu as pltpu

F32 = jnp.float32
BF16 = jnp.bfloat16
I32 = jnp.int32

EPS = 1e-6
ROPE_THETA = 10000.0
N_ADA = 9
TOPK_MAX = 256
PAGE_SIZE = 128
A_HEADS = 8
A_CHUNK = 128
B_HEADS = 8
C_HEADS = 16
IDX_HEADS = 16
IDX_DIM = 64
IDX_PACK = 4 * IDX_DIM

LANES = 128
SUBLANES_BF16 = 16
VMEM_LIMIT = 56 * 1024 * 1024
BISECT_ROWS = 128
NEG = -1e30
INT_MIN = -2 ** 31


def _cparams(sem):
    return pltpu.CompilerParams(dimension_semantics=sem, vmem_limit_bytes=VMEM_LIMIT)


def _dot(a, b):
    return jnp.dot(a, b, preferred_element_type=F32)


def _dot_nt(a, b):
    return lax.dot_general(a, b, (((1,), (1,)), ((), ())), preferred_element_type=F32)


def _split(x):
    hi = x.astype(BF16)
    return hi, (x - hi.astype(F32)).astype(BF16)


def _dot3(a, b, dot=_dot):
    a_hi, a_lo = _split(a)
    b_hi, b_lo = _split(b)
    return dot(a_hi, b_hi) + (dot(a_hi, b_lo) + dot(a_lo, b_hi))


def _norm_mod(x, g, scale, shift):
    y = x * lax.rsqrt(jnp.mean(x * x, -1, keepdims=True) + EPS) * g
    return y * (1.0 + scale) + shift


NORM_ROWS = 16
NORM_UNROLL = 4


def _norm_mod_to(h_scr, x_ref, g_ref, sc_ref, sh_ref):
    tm = x_ref.shape[0]
    nr = min(NORM_ROWS, tm)
    per_row = sc_ref.shape[1] != 1
    g = g_ref[...]
    if not per_row:
        gain = g * (1.0 + sc_ref[0])
        shift = sh_ref[0]

    def body(r, carry):
        rows = pl.ds(pl.multiple_of(r * nr, nr), nr)
        x = x_ref[rows, :]
        y = x * lax.rsqrt(jnp.mean(x * x, -1, keepdims=True) + EPS)
        if per_row:
            h = y * (g * (1.0 + sc_ref[0, rows, :])) + sh_ref[0, rows, :]
        else:
            h = y * gain + shift
        h_scr[rows, :] = h.astype(h_scr.dtype)
        return carry

    lax.fori_loop(0, tm // nr, body, 0, unroll=min(NORM_UNROLL, tm // nr))


def _log_sigmoid(x):
    return jnp.minimum(x, 0.0) - jnp.log1p(jnp.exp(-jnp.abs(x)))


def _ada_kernel(c_ref, w_ref, b_ref, o_ref):
    c = c_ref[...]
    a = (c * jax.nn.sigmoid(c)).astype(BF16)
    o_ref[0] = _dot(a, w_ref[0].astype(BF16)) + b_ref[0]


def ada_all(c_rows, ada_w, ada_b):
    depth, d, n = ada_w.shape
    r = c_rows.shape[0]
    tn = 1024
    return pl.pallas_call(
        _ada_kernel,
        grid=(depth, n // tn),
        in_specs=[pl.BlockSpec((r, d), lambda l, j: (0, 0)),
                  pl.BlockSpec((1, d, tn), lambda l, j: (l, 0, j)),
                  pl.BlockSpec((1, 1, tn), lambda l, j: (l, 0, j))],
        out_specs=pl.BlockSpec((1, r, tn), lambda l, j: (l, 0, j)),
        out_shape=jax.ShapeDtypeStruct((depth, r, n), F32),
        compiler_params=_cparams(("arbitrary", "arbitrary")),
        name="ada_mod",
    )(c_rows, ada_w, ada_b.reshape(depth, 1, n))


def _row_tile(want, rows_per_batch):
    return min(want, rows_per_batch)


def _mod_spec(mod, tm, tn, rows_per_batch, col_of):
    r = mod.shape[1]
    bpb = max(rows_per_batch // tm, 1)
    return pl.BlockSpec((1, r, tn), lambda i, j: (i // bpb, 0, col_of(j)))


def _stage_wi_kernel(w_ref, o_ref):
    ff = w_ref.shape[-1]
    o_ref[:, :ff] = w_ref[...].astype(BF16)
    o_ref[:, ff:] = jnp.zeros((o_ref.shape[0], o_ref.shape[1] - ff), BF16)


def _stage_wo_kernel(w_ref, o_ref, *, n_blocks, last_rows):
    o_ref[...] = w_ref[...].astype(BF16)
    if last_rows < o_ref.shape[0]:

        @pl.when(pl.program_id(1) == n_blocks - 1)
        def _():
            o_ref[last_rows:, :] = jnp.zeros((o_ref.shape[0] - last_rows, o_ref.shape[1]), BF16)


def stage_ffn_weights(ffn_wi, ffn_wo, fp):
    depth, _, d, ff2 = ffn_wi.shape
    ff = ff2 // 2
    assert ff % LANES == 0 and fp % LANES == 0
    ls = depth * 2
    rb = min(128, d)
    wi_p = pl.pallas_call(
        _stage_wi_kernel,
        grid=(ls, d // rb, 2),
        in_specs=[pl.BlockSpec((None, rb, ff), lambda a, r, h: (a, r, h))],
        out_specs=pl.BlockSpec((None, rb, fp), lambda a, r, h: (a, r, h)),
        out_shape=jax.ShapeDtypeStruct((ls, d, 2 * fp), BF16),
        compiler_params=_cparams(("arbitrary", "arbitrary", "arbitrary")),
        name="stage_wi",
    )(ffn_wi.reshape(ls, d, ff2))
    wrb = max(r for r in range(SUBLANES_BF16, 1025, SUBLANES_BF16) if fp % r == 0)
    n_blocks = fp // wrb
    wo_p = pl.pallas_call(
        functools.partial(_stage_wo_kernel, n_blocks=n_blocks, last_rows=ff - (n_blocks - 1) * wrb),
        grid=(ls, n_blocks),
        in_specs=[pl.BlockSpec((None, wrb, d), lambda a, j: (a, j, 0))],
        out_specs=pl.BlockSpec((None, wrb, d), lambda a, j: (a, j, 0)),
        out_shape=jax.ShapeDtypeStruct((ls, fp, d), BF16),
        compiler_params=_cparams(("arbitrary", "arbitrary")),
        name="stage_wo",
    )(ffn_wo.reshape(ls, ff, d))
    return wi_p.reshape(depth, 2, d, 2 * fp), wo_p.reshape(depth, 2, fp, d)


def _ffn_kernel(*refs, nf, final):
    if final:
        (x_ref, sh_ref, sc_ref, gt_ref, g_ref, wg_ref, wu_ref, wo_ref, fg_ref,
         o_ref, h_scr, acc_scr) = refs
    else:
        (x_ref, sh_ref, sc_ref, gt_ref, g_ref, wg_ref, wu_ref, wo_ref,
         o_ref, h_scr, acc_scr) = refs
    f = pl.program_id(1)

    @pl.when(f == 0)
    def _():
        _norm_mod_to(h_scr, x_ref, g_ref, sc_ref, sh_ref)
        acc_scr[...] = jnp.zeros_like(acc_scr)

    h = h_scr[...]
    a = _dot(h, wg_ref[...])
    u = _dot(h, wu_ref[...])
    act = (a * jax.nn.sigmoid(a) * u).astype(BF16)
    acc_scr[...] += _dot(act, wo_ref[...])

    @pl.when(f == nf - 1)
    def _():
        y = x_ref[...] + 0.5 * gt_ref[0] * acc_scr[...]
        if final:
            y = y * lax.rsqrt(jnp.mean(y * y, -1, keepdims=True) + EPS) * fg_ref[...]
        o_ref[...] = y


FFN_TF = 512


def ffn_sublayer(x, shift, scale, gate, g, wi_p, wo_p, l, s, rows_per_batch, final_g=None):
    m, d = x.shape
    fp = wo_p.shape[2]
    tf = FFN_TF
    nf = fp // tf
    tm = _row_tile(512, rows_per_batch)
    final = final_g is not None
    in_specs = [
        pl.BlockSpec((tm, d), lambda i, f: (i, 0)),
        _mod_spec(shift, tm, d, rows_per_batch, lambda f: 0),
        _mod_spec(scale, tm, d, rows_per_batch, lambda f: 0),
        _mod_spec(gate, tm, d, rows_per_batch, lambda f: 0),
        pl.BlockSpec((1, d), lambda i, f: (0, 0)),
        pl.BlockSpec((None, None, d, tf), lambda i, f: (l, s, 0, f)),
        pl.BlockSpec((None, None, d, tf), lambda i, f: (l, s, 0, nf + f)),
        pl.BlockSpec((None, None, tf, d), lambda i, f: (l, s, f, 0)),
    ]
    args = [x, shift, scale, gate, g.reshape(1, d), wi_p, wi_p, wo_p]
    if final:
        in_specs.append(pl.BlockSpec((1, d), lambda i, f: (0, 0)))
        args.append(final_g.reshape(1, d))
    return pl.pallas_call(
        functools.partial(_ffn_kernel, nf=nf, final=final),
        grid=(m // tm, nf),
        in_specs=in_specs,
        out_specs=pl.BlockSpec((tm, d), lambda i, f: (i, 0)),
        out_shape=jax.ShapeDtypeStruct((m, d), F32),
        scratch_shapes=[pltpu.VMEM((tm, d), BF16), pltpu.VMEM((tm, d), F32)],
        compiler_params=_cparams(("arbitrary", "arbitrary")),
        name="ffn",
    )(*args)


def _rope_tile(y, cos, sin, rope):
    if rope == 128:
        r = pltpu.roll(y, 64, 1)
    else:
        lane = lax.broadcasted_iota(I32, y.shape, 1)
        r = jnp.where((lane % 64) < 32, pltpu.roll(y, 96, 1), pltpu.roll(y, 32, 1))
    return y * cos + r * sin


LO = "bf16 residual"
HEADS = "bf16, one (rows, 128) slab per 128 output columns"
Q3 = "packed indexer query"
K3 = "packed indexer key"
_WIDTH = {Q3: 4, K3: 2}


def _emit(y, kind):
    if kind in (F32, BF16):
        return [y.astype(kind)]
    hi_b, lo_b = _split(y)
    if kind == LO:
        return [lo_b]
    hi, lo = hi_b.astype(F32), lo_b.astype(F32)
    first = lax.broadcasted_iota(I32, y.shape, 1) < 64
    if kind == K3:
        tiles = [jnp.where(first, hi, pltpu.roll(hi, 64, 1)), jnp.where(first, lo, 0.0)]
    else:
        hi_r = pltpu.roll(hi, 64, 1)
        tiles = [jnp.where(first, hi, pltpu.roll(lo, 64, 1)), jnp.where(first, hi, 0.0),
                 jnp.where(first, hi_r, lo), jnp.where(first, hi_r, 0.0)]
    return [t.astype(BF16) for t in tiles]


def _proj_kernel(*refs, rope, kinds, tn, precise):
    n_out = len(kinds)
    x_ref, sh_ref, sc_ref, g_ref, w_ref = refs[:5]
    pos = 5
    if rope:
        cos_ref, sin_ref = refs[5:7]
        pos = 7
    out_refs = refs[pos:pos + n_out]
    h_scr = refs[pos + n_out]
    j = pl.program_id(1)

    @pl.when(j == 0)
    def _():
        _norm_mod_to(h_scr, x_ref, g_ref, sc_ref, sh_ref)

    if precise:
        y = _dot3(h_scr[...], w_ref[...])
    else:
        y = _dot(h_scr[...], w_ref[...].astype(BF16))
    if rope:
        cos = cos_ref[...]
        sin = sin_ref[...]
    for c in range(tn // LANES):
        yc = y[:, c * LANES:(c + 1) * LANES]
        if rope:
            yc = _rope_tile(yc, cos, sin, rope)
        for o_ref, kind in zip(out_refs, kinds):
            if kind == HEADS:
                o_ref[c] = yc.astype(BF16)
                continue
            tiles = _emit(yc, kind)
            for k, tile in enumerate(tiles):
                at = (c * len(tiles) + k) * LANES
                o_ref[:, at:at + LANES] = tile


def proj(x, shift, scale, g, w, slot, col_off, n_cols, rows_per_batch, dtypes,
         rope=0, tables=None, precise=False):
    m, d = x.shape
    tn = min(1024, n_cols)
    tm = _row_tile(1024, rows_per_batch)
    off = col_off // tn
    assert col_off % tn == 0 and n_cols % tn == 0
    in_specs = [
        pl.BlockSpec((tm, d), lambda i, j: (i, 0)),
        _mod_spec(shift, tm, d, rows_per_batch, lambda j: 0),
        _mod_spec(scale, tm, d, rows_per_batch, lambda j: 0),
        pl.BlockSpec((1, d), lambda i, j: (0, 0)),
        pl.BlockSpec((None, d, tn), lambda i, j: (slot, 0, off + j)),
    ]
    args = [x, shift, scale, g.reshape(1, d), w]
    if rope:
        cos, sin = tables
        nbt = max(cos.shape[0] // tm, 1)
        in_specs += [pl.BlockSpec((tm, LANES), lambda i, j: (i % nbt, 0))] * 2
        args += [cos, sin]
    out = pl.pallas_call(
        functools.partial(_proj_kernel, rope=rope, kinds=tuple(dtypes), tn=tn, precise=precise),
        grid=(m // tm, n_cols // tn),
        in_specs=in_specs,
        out_specs=[pl.BlockSpec((tn // LANES, tm, LANES), lambda i, j: (j, i, 0)) if dt == HEADS else
                   pl.BlockSpec((tm, tn * _WIDTH.get(dt, 1)), lambda i, j: (i, j)) for dt in dtypes],
        out_shape=[jax.ShapeDtypeStruct((n_cols // LANES, m, LANES), BF16) if dt == HEADS else
                   jax.ShapeDtypeStruct((m, n_cols * _WIDTH.get(dt, 1)), dt if dt in (F32, BF16) else BF16)
                   for dt in dtypes],
        scratch_shapes=[pltpu.VMEM((tm, d), F32 if precise else BF16)],
        compiler_params=_cparams(("arbitrary", "arbitrary")),
        name="proj",
    )(*args)
    return out


def _outproj_kernel(a_ref, x_ref, gt_ref, w_ref, o_ref):
    y = _dot(a_ref[...], w_ref[...].astype(BF16))
    o_ref[...] = x_ref[...] + gt_ref[0] * y


def outproj(a, x, gate, w, slot, rows_per_batch):
    m, d = x.shape
    k = a.shape[1]
    tn = 512
    tm = _row_tile(1024, rows_per_batch)
    return pl.pallas_call(
        _outproj_kernel,
        grid=(m // tm, d // tn),
        in_specs=[pl.BlockSpec((tm, k), lambda i, j: (i, 0)),
                  pl.BlockSpec((tm, tn), lambda i, j: (i, j)),
                  _mod_spec(gate, tm, tn, rows_per_batch, lambda j: j),
                  pl.BlockSpec((None, k, tn), lambda i, j: (slot, 0, j))],
        out_specs=pl.BlockSpec((tm, tn), lambda i, j: (i, j)),
        out_shape=jax.ShapeDtypeStruct((m, d), F32),
        compiler_params=_cparams(("arbitrary", "arbitrary")),
        name="outproj",
    )(a, x, gate, w)


def _mlstm_kernel(q_ref, k_ref, v_ref, o_ref, gates_ref, bias_ref, ng_ref,
                  hh_ref, c_ref, n_ref, m_ref, *, heads, dk, dv, chunk):
    c_idx = pl.program_id(1)

    @pl.when(c_idx == 0)
    def _():
        c_ref[...] = jnp.zeros_like(c_ref)
        n_ref[...] = jnp.zeros_like(n_ref)
        m_ref[...] = jnp.zeros_like(m_ref)

    L = chunk
    g = gates_ref[...] + bias_ref[...]
    gt = g.T
    lf = _log_sigmoid(g)
    lft = _log_sigmoid(gt)
    row = lax.broadcasted_iota(I32, (L, L), 0)
    col = lax.broadcasted_iota(I32, (L, L), 1)
    causal = col <= row
    qscale = dk ** -0.5
    for h in range(heads):
        icol = g[:, h:h + 1]
        fcol = lf[:, heads + h:heads + h + 1]
        irow = gt[h:h + 1, :]
        frow = lft[heads + h:heads + h + 1, :]
        b_col = jnp.sum(jnp.where(causal, frow, 0.0), axis=1, keepdims=True)
        b_row = jnp.sum(jnp.where(row <= col, fcol, 0.0), axis=0, keepdims=True)
        dmat = jnp.where(causal, b_col - b_row + irow, NEG)
        m_prev = m_ref[0, h:h + 1, 0:1]
        inter = b_col + m_prev
        m_t = jnp.maximum(inter, jnp.max(dmat, axis=1, keepdims=True))
        w_intra = jnp.exp(dmat - m_t)
        w_inter = jnp.exp(inter - m_t)
        qf = q_ref[:, h * dk:(h + 1) * dk] * qscale
        kf = k_ref[:, h * dk:(h + 1) * dk]
        vf = v_ref[:, h * dv:(h + 1) * dv]
        s = _dot3(qf, kf, _dot_nt) * w_intra
        c_prev = c_ref[0, h]
        n_prev = n_ref[0, h:h + 1, :]
        num = _dot3(s, vf) + w_inter * _dot3(qf, c_prev, _dot_nt)
        den = jnp.sum(s, axis=1, keepdims=True) + w_inter * jnp.sum(qf * n_prev, axis=1, keepdims=True)
        hh = num / jnp.maximum(jnp.abs(den), jnp.exp(-m_t))
        m_new = m_t[L - 1:L, :]
        b_last = b_col[L - 1:L, :]
        w_s = jnp.exp(b_last - b_col + icol - m_new)
        decay = jnp.exp(b_last + m_prev - m_new)
        vw_t = (vf * w_s).T
        c_ref[0, h] = decay * c_prev + _dot3(vw_t, kf)
        n_ref[0, h:h + 1, :] = decay * n_prev + jnp.sum(kf * w_s, axis=0, keepdims=True)
        m_ref[0, h:h + 1, :] = jnp.broadcast_to(m_new, (1, LANES))
        hn = hh * lax.rsqrt(jnp.mean(hh * hh, -1, keepdims=True) + EPS) * ng_ref[:, h * dv:(h + 1) * dv]
        hn = hn * jax.nn.sigmoid(o_ref[:, h * dv:(h + 1) * dv])
        hh_ref[:, h * dv:(h + 1) * dv] = hn.astype(BF16)


def mlstm_prompt(y, gates, b_if, norm_g, bn, t):
    heads = A_HEADS
    dv = norm_g.shape[-1]
    dk = dv // 2
    qk = heads * dk
    vw = heads * dv
    L = A_CHUNK
    nc = t // L
    bias = jnp.zeros((1, LANES), F32).at[0, :2 * heads].set(b_if)
    row = lambda b, c: b * nc + c
    outs = pl.pallas_call(
        functools.partial(_mlstm_kernel, heads=heads, dk=dk, dv=dv, chunk=L),
        grid=(bn, nc),
        in_specs=[pl.BlockSpec((L, qk), lambda b, c: (row(b, c), 0)),
                  pl.BlockSpec((L, qk), lambda b, c: (row(b, c), 1)),
                  pl.BlockSpec((L, vw), lambda b, c: (row(b, c), 1)),
                  pl.BlockSpec((L, vw), lambda b, c: (row(b, c), 2)),
                  pl.BlockSpec((L, LANES), lambda b, c: (row(b, c), 0)),
                  pl.BlockSpec((1, LANES), lambda b, c: (0, 0)),
                  pl.BlockSpec((1, vw), lambda b, c: (0, 0))],
        out_specs=[pl.BlockSpec((L, vw), lambda b, c: (row(b, c), 0)),
                   pl.BlockSpec((1, heads, dv, dk), lambda b, c: (b, 0, 0, 0)),
                   pl.BlockSpec((1, heads, dk), lambda b, c: (b, 0, 0)),
                   pl.BlockSpec((1, heads, LANES), lambda b, c: (b, 0, 0))],
        out_shape=[jax.ShapeDtypeStruct((bn * t, vw), BF16),
                   jax.ShapeDtypeStruct((bn, heads, dv, dk), F32),
                   jax.ShapeDtypeStruct((bn, heads, dk), F32),
                   jax.ShapeDtypeStruct((bn, heads, LANES), F32)],
        compiler_params=_cparams(("arbitrary", "arbitrary")),
        name="mlstm_scan",
    )(y, y, y, y, gates, bias, norm_g.reshape(1, vw))
    hh, c_fin, n_fin, m_fin = outs
    return hh, c_fin, n_fin, m_fin[:, :, 0]


def _mlstm_step_kernel(q_ref, k_ref, v_ref, o_ref, ig_ref, fg_ref, c0_ref, n0_ref, m0_ref, ng_ref,
                       h_ref, c_ref, n_ref, m_ref, *, dk):
    cst = c0_ref[0]
    n0 = n0_ref[0]
    m0 = m0_ref[0]
    q = q_ref[0] * dk ** -0.5
    k = k_ref[0]
    v = v_ref[0]
    ig = ig_ref[0]
    fc = _log_sigmoid(fg_ref[0])
    inter = fc + m0
    m_t = jnp.maximum(inter, ig)
    w_intra = jnp.exp(ig - m_t)
    w_inter = jnp.exp(inter - m_t)
    s = jnp.sum(q * k, axis=-1, keepdims=True) * w_intra
    cq = jnp.sum(cst * q, axis=-1, keepdims=True)
    num = s * v + w_inter * cq
    den = s + w_inter * jnp.sum(n0 * q, axis=-1, keepdims=True)
    hh = num / jnp.maximum(jnp.abs(den), jnp.exp(-m_t))
    w_s = w_intra
    decay = jnp.exp(inter - m_t)
    c_ref[0] = decay * cst + w_s * (v * k)
    n_ref[0] = decay * n0 + w_s * k
    m_ref[0] = m_t
    hn = hh * lax.rsqrt(jnp.mean(hh * hh, axis=1, keepdims=True) + EPS) * ng_ref[...]
    h_ref[0] = hn * jax.nn.sigmoid(o_ref[0])


def mlstm_sample(y, gates, b_if, norm_g, c0, n0, m0, bs):
    heads = A_HEADS
    dv = norm_g.shape[-1]
    dk = dv // 2
    qk = heads * dk
    vw = heads * dv
    y = y[:bs]
    q = y[:, :qk].reshape(bs, heads, 1, dk)
    k = y[:, qk:2 * qk].reshape(bs, heads, 1, dk)
    v = y[:, 2 * qk:2 * qk + vw].reshape(bs, heads, dv, 1)
    o = y[:, 2 * qk + vw:].reshape(bs, heads, dv, 1)
    gb = gates[:bs, :2 * heads] + b_if
    ig = gb[:, :heads].reshape(bs, heads, 1, 1)
    fg = gb[:, heads:].reshape(bs, heads, 1, 1)
    spec = lambda *shape: pl.BlockSpec((1,) + shape, lambda b: (b,) + (0,) * len(shape))
    outs = pl.pallas_call(
        functools.partial(_mlstm_step_kernel, dk=dk),
        grid=(bs,),
        in_specs=[spec(heads, 1, dk), spec(heads, 1, dk), spec(heads, dv, 1), spec(heads, dv, 1),
                  spec(heads, 1, 1), spec(heads, 1, 1),
                  spec(heads, dv, dk), spec(heads, 1, dk), spec(heads, 1, 1),
                  pl.BlockSpec((heads, dv, 1), lambda b: (0, 0, 0))],
        out_specs=[spec(heads, dv, 1), spec(heads, dv, dk), spec(heads, 1, dk), spec(heads, 1, 1)],
        out_shape=[jax.ShapeDtypeStruct((bs, heads, dv, 1), F32),
                   jax.ShapeDtypeStruct((bs, heads, dv, dk), F32),
                   jax.ShapeDtypeStruct((bs, heads, 1, dk), F32),
                   jax.ShapeDtypeStruct((bs, heads, 1, 1), F32)],
        compiler_params=_cparams(("arbitrary",)),
        name="mlstm_step",
    )(q, k, v, o, ig, fg, c0, n0.reshape(bs, heads, 1, dk), m0.reshape(bs, heads, 1, 1),
      norm_g.reshape(heads, dv, 1))
    hh, c_new, n_new, m_new = outs
    return hh.reshape(bs, vw), c_new, n_new.reshape(bs, heads, dk), m_new.reshape(bs, heads)


def _diff_lambda(lam_ref, lam_init):
    l = lam_ref[...]
    a = jnp.sum(l[0:1, :] * l[1:2, :], axis=1, keepdims=True)
    b = jnp.sum(l[2:3, :] * l[3:4, :], axis=1, keepdims=True)
    return jnp.exp(a) - jnp.exp(b) + lam_init


def _flash_step(s, vs, m_scr, l_scr, acc_scr, c):
    m_prev = m_scr[c]
    m_new = jnp.maximum(m_prev, jnp.max(s, axis=1, keepdims=True))
    alpha = jnp.exp(m_prev - m_new)
    p = jnp.exp(s - m_new)
    l_scr[c] = alpha * l_scr[c] + jnp.sum(p, axis=1, keepdims=True)
    acc_scr[c] = alpha * acc_scr[c] + _dot(p.astype(BF16), vs)
    m_scr[c] = m_new


def _diff_prompt_kernel(q_ref, k_ref, v_ref, lam_ref, ng_ref, o_ref, m_scr, l_scr, acc_scr,
                        *, tq, tk, dk, lam_init):
    i = pl.program_id(2)
    scale = dk ** -0.5
    m_scr[...] = jnp.full_like(m_scr, NEG)
    l_scr[...] = jnp.zeros_like(l_scr)
    acc_scr[...] = jnp.zeros_like(acc_scr)

    def chunk(j, masked):
        start = pl.multiple_of(j * tk, tk)
        vs = v_ref[pl.ds(start, tk), :]
        for c in range(2):
            s = _dot_nt(q_ref[:, c * dk:(c + 1) * dk], k_ref[pl.ds(start, tk), c * dk:(c + 1) * dk]) * scale
            if masked:
                row = i * tq + lax.broadcasted_iota(I32, (tq, tk), 0)
                col = start + lax.broadcasted_iota(I32, (tq, tk), 1)
                s = jnp.where(col <= row, s, NEG)
            _flash_step(s, vs, m_scr, l_scr, acc_scr, c)

    def body(j, carry):
        chunk(j, False)
        return carry

    n_full = (i * tq) // tk
    lax.fori_loop(0, n_full, body, 0)
    chunk(n_full, True)
    lam = _diff_lambda(lam_ref, lam_init)
    o = acc_scr[0] / l_scr[0] - lam * (acc_scr[1] / l_scr[1])
    o = o * lax.rsqrt(jnp.mean(o * o, -1, keepdims=True) + EPS) * ng_ref[...] * (1.0 - lam_init)
    o_ref[...] = o.astype(o_ref.dtype)


def diff_prompt(qb, kb, vb, lam, norm_g, lam_init, bn, t):
    heads = B_HEADS
    dv = norm_g.shape[-1]
    dk = dv // 2
    tq = min(256, t)
    tk = min(1024, t)
    nq = t // tq
    return pl.pallas_call(
        functools.partial(_diff_prompt_kernel, tq=tq, tk=tk, dk=dk, lam_init=lam_init),
        grid=(bn, heads, nq),
        in_specs=[pl.BlockSpec((tq, 2 * dk), lambda b, h, i: (b * nq + i, h)),
                  pl.BlockSpec((t, 2 * dk), lambda b, h, i: (b, h)),
                  pl.BlockSpec((t, dv), lambda b, h, i: (b, h)),
                  pl.BlockSpec(lam.shape, lambda b, h, i: (0, 0)),
                  pl.BlockSpec((1, dv), lambda b, h, i: (0, h))],
        out_specs=pl.BlockSpec((tq, dv), lambda b, h, i: (b * nq + i, h)),
        out_shape=jax.ShapeDtypeStruct((bn * t, heads * dv), BF16),
        scratch_shapes=[pltpu.VMEM((2, tq, 1), F32), pltpu.VMEM((2, tq, 1), F32),
                        pltpu.VMEM((2, tq, dv), F32)],
        compiler_params=_cparams(("arbitrary", "arbitrary", "arbitrary")),
        name="diff_prompt",
    )(qb, kb, vb, lam, norm_g.reshape(1, heads * dv))


def _paged_kernel(pt_ref, q_ref, kn_ref, vn_ref, *rest, mode, group, n_steps, pps, lam_init):
    k_refs, v_refs, rest = rest[:pps], rest[pps:2 * pps], rest[2 * pps:]
    if mode == "diff":
        lam_ref, ng_ref, o_ref, m_scr, l_scr, acc_scr = rest
    else:
        bias_ref, biasn_ref, o_ref, exp_scr, m_scr, l_scr, acc_scr = rest
    p = pl.program_id(1)
    rows, dk = q_ref.shape[1:]
    ncol = PAGE_SIZE * group
    scale = dk ** -0.5

    @pl.when(p == 0)
    def _():
        m_scr[...] = jnp.full_like(m_scr, NEG)
        l_scr[...] = jnp.zeros_like(l_scr)
        acc_scr[...] = jnp.zeros_like(acc_scr)
        if mode == "dsa":
            t = lax.broadcasted_iota(I32, (PAGE_SIZE, ncol), 0)
            c = lax.broadcasted_iota(I32, (PAGE_SIZE, ncol), 1)
            exp_scr[...] = jnp.where(c // group == t, 1.0, 0.0).astype(BF16)

    q = q_ref[0]
    qb = q.astype(BF16)

    def page_scores(k_ref):
        if mode == "diff":
            half = rows // 2
            return jnp.concatenate(
                [_dot_nt(qb[c * half:(c + 1) * half], k_ref[pl.ds(c, ncol, stride=2), :].astype(BF16))
                 for c in range(2)], axis=0)
        return _dot_nt(qb, k_ref[...].astype(BF16))

    s = jnp.concatenate([page_scores(k_ref) for k_ref in k_refs], axis=1)
    r_i = lax.broadcasted_iota(I32, s.shape, 0)
    c_i = lax.broadcasted_iota(I32, s.shape, 1)
    keep = (c_i % group) == (r_i % group)
    if mode == "dsa":
        sel = jnp.where(bias_ref[0, 0] == 0.0, 1.0, 0.0).astype(BF16)
        sel = jnp.concatenate(
            [_dot(jnp.broadcast_to(sel[pp:pp + 1], (8, PAGE_SIZE)), exp_scr[...])[0:1] for pp in range(pps)],
            axis=1)
        keep = keep & (sel > 0.5)
    s = jnp.where(keep, s * scale, NEG)
    m_prev = m_scr[...]
    m_new = jnp.maximum(m_prev, jnp.max(s, axis=1, keepdims=True))
    alpha = jnp.exp(m_prev - m_new)
    pe = jnp.exp(s - m_new)
    l_scr[...] = alpha * l_scr[...] + jnp.sum(pe, axis=1, keepdims=True)
    pv = _dot(pe[:, :ncol].astype(BF16), v_refs[0][...].astype(BF16))
    for pp in range(1, pps):
        pv = pv + _dot(pe[:, pp * ncol:(pp + 1) * ncol].astype(BF16), v_refs[pp][...].astype(BF16))
    acc_scr[...] = alpha * acc_scr[...] + pv
    m_scr[...] = m_new

    @pl.when(p == n_steps - 1)
    def _():
        s_n = jnp.sum(q * kn_ref[0], axis=1, keepdims=True) * scale
        v_n = vn_ref[0]
        if mode == "dsa":
            s_n = s_n + biasn_ref[0][:, 0:1]
        else:
            v_n = jnp.concatenate([v_n, v_n], axis=0)
        m_prev = m_scr[...]
        m_new = jnp.maximum(m_prev, s_n)
        alpha = jnp.exp(m_prev - m_new)
        pn = jnp.exp(s_n - m_new)
        l_fin = alpha * l_scr[...] + pn
        acc = (alpha * acc_scr[...] + pn * v_n) / l_fin
        if mode == "diff":
            half = rows // 2
            o = acc[:half] - _diff_lambda(lam_ref, lam_init) * acc[half:]
            o = o * lax.rsqrt(jnp.mean(o * o, -1, keepdims=True) + EPS) * ng_ref[...]
            o_ref[0] = o * (1.0 - lam_init)
        else:
            o_ref[0] = acc


def paged_attention(mode, q, k_new, v_new, cache_k, cache_v, slot, page_table, extra, lam_init=0.0):
    bs, n_pages = page_table.shape
    rows, dk = q.shape[1:]
    hv, dv = v_new.shape[1:]
    group = hv
    ncol = PAGE_SIZE * group
    pps = max(c for c in (4, 2, 1) if n_pages % c == 0)
    n_steps = n_pages // pps
    seq_spec = lambda a: pl.BlockSpec((1,) + a.shape[1:], lambda b, p, pt: (b, 0, 0))

    def page_specs(a):
        return [pl.BlockSpec((None, None) + a.shape[2:],
                             lambda b, p, pt, pp=pp: (slot, pt[b, p * pps + pp], 0, 0)) for pp in range(pps)]

    scratch = [pltpu.VMEM((rows, 1), F32), pltpu.VMEM((rows, 1), F32), pltpu.VMEM((rows, dv), F32)]
    if mode == "diff":
        lam, norm_g = extra
        extra_specs = [pl.BlockSpec(lam.shape, lambda b, p, pt: (0, 0)),
                       pl.BlockSpec(norm_g.shape, lambda b, p, pt: (0, 0))]
        extra_args = [lam, norm_g]
    else:
        bias, bias_new = extra
        bias = bias.reshape(bs, n_steps, pps, PAGE_SIZE)
        extra_specs = [pl.BlockSpec((1, 1, pps, PAGE_SIZE), lambda b, p, pt: (b, p, 0, 0)),
                       pl.BlockSpec((1, 1, LANES), lambda b, p, pt: (b, 0, 0))]
        extra_args = [bias, bias_new]
        scratch = [pltpu.VMEM((PAGE_SIZE, ncol), BF16)] + scratch
    grid_spec = pltpu.PrefetchScalarGridSpec(
        num_scalar_prefetch=1,
        grid=(bs, n_steps),
        in_specs=[seq_spec(q), seq_spec(k_new), seq_spec(v_new)] + page_specs(cache_k)
        + page_specs(cache_v) + extra_specs,
        out_specs=pl.BlockSpec((1, hv, dv), lambda b, p, pt: (b, 0, 0)),
        scratch_shapes=scratch,
    )
    return pl.pallas_call(
        functools.partial(_paged_kernel, mode=mode, group=group, n_steps=n_steps, pps=pps,
                          lam_init=lam_init),
        grid_spec=grid_spec,
        out_shape=jax.ShapeDtypeStruct((bs, hv, dv), F32),
        compiler_params=_cparams(("arbitrary", "arbitrary")),
        name="paged_" + mode,
    )(page_table, q, k_new, v_new, *([cache_k] * pps), *([cache_v] * pps), *extra_args)


def _sortable_key(s):
    bits = lax.bitcast_convert_type(s, I32)
    return bits ^ ((bits >> 31) & 0x7FFFFFFF)


def _bisect_threshold(count_ge, shapes, n_top):
    def step(it, t_us):
        bit = jnp.left_shift(jnp.int32(1), 31 - it)
        cands = [t_u | bit for t_u in t_us]
        cnts = count_ge([c ^ INT_MIN for c in cands])
        return tuple(jnp.where(cnt >= n_top, c, t_u) for cnt, c, t_u in zip(cnts, cands, t_us))

    t_us = lax.fori_loop(0, 32, step, tuple(jnp.zeros(shape, I32) for shape in shapes))
    return [jnp.maximum(t_u ^ INT_MIN, INT_MIN + 1) for t_u in t_us]


def _dsa_prompt_kernel(qi_ref, wt_ref, ki_ref, q_ref, k_ref, v_ref, o_ref,
                       key_scr, bias_scr, wb_scr, m_scr, l_scr, acc_scr, *, tq, tk, dh, n_top, wcol):
    i = pl.program_id(1)
    h = pl.program_id(2)
    scale = dh ** -0.5
    idx_scale = IDX_DIM ** -0.5 * IDX_HEADS ** -0.5
    groups = tq // LANES
    n_att = (i * tq) // tk + 1

    @pl.when(h == 0)
    def _():
        w = wt_ref[...] * idx_scale
        for hh in range(IDX_HEADS):
            wb_scr[hh] = jnp.broadcast_to(w[:, wcol + hh:wcol + hh + 1], (tq, LANES))

        def score_chunk(j, masked):
            start = pl.multiple_of(j * tq, tq)
            kc = ki_ref[pl.ds(start, tq), :]
            acc = [jnp.zeros((tq, LANES), F32) for _ in range(groups)]
            for hh in range(IDX_HEADS):
                s = _dot_nt(qi_ref[:, hh * IDX_PACK:(hh + 1) * IDX_PACK], kc)
                wb = wb_scr[hh]
                for c in range(groups):
                    acc[c] = acc[c] + jnp.maximum(s[:, c * LANES:(c + 1) * LANES], 0.0) * wb
            for c in range(groups):
                key = _sortable_key(acc[c])
                if masked:
                    row = lax.broadcasted_iota(I32, (tq, LANES), 0)
                    col = lax.broadcasted_iota(I32, (tq, LANES), 1) + c * LANES
                    key = jnp.where(col <= row, key, INT_MIN)
                key_scr[:, pl.ds(pl.multiple_of(start + c * LANES, LANES), LANES)] = key

        def score_body(j, carry):
            score_chunk(j, False)
            return carry

        lax.fori_loop(0, i, score_body, 0)
        score_chunk(i, True)

        row_blocks = [pl.ds(rb * BISECT_ROWS, BISECT_ROWS) for rb in range(tq // BISECT_ROWS)]

        def count_ge(cands):
            parts = []
            for rows, cand in zip(row_blocks, cands):
                cand_b = jnp.broadcast_to(cand, (BISECT_ROWS, LANES))

                def cbody(j, part, rows=rows, cand_b=cand_b):
                    start = pl.multiple_of(j * tq, tq)
                    for c in range(groups):
                        kk = key_scr[rows, pl.ds(start + c * LANES, LANES)]
                        part = part + jnp.where(kk >= cand_b, 1.0, 0.0)
                    return part

                parts.append(lax.fori_loop(0, i + 1, cbody, jnp.zeros((BISECT_ROWS, LANES), F32)))
            return [jnp.sum(part, axis=1, keepdims=True) for part in parts]

        thrs = _bisect_threshold(count_ge, [(BISECT_ROWS, 1)] * len(row_blocks), n_top)
        for rows, thr in zip(row_blocks, thrs):
            thr = jnp.broadcast_to(thr, (BISECT_ROWS, LANES))

            def bias_body(j, carry, rows=rows, thr=thr):
                start = pl.multiple_of(j * LANES, LANES)
                kk = key_scr[rows, pl.ds(start, LANES)]
                bias_scr[rows, pl.ds(start, LANES)] = jnp.where(kk >= thr, 0.0, NEG)
                return carry

            lax.fori_loop(0, (i + 1) * groups, bias_body, 0)

        def fill_body(j, carry):
            start = pl.multiple_of(j * LANES, LANES)
            bias_scr[:, pl.ds(start, LANES)] = jnp.full((tq, LANES), NEG, F32)
            return carry

        lax.fori_loop((i + 1) * groups, n_att * (tk // LANES), fill_body, 0)

    m_scr[...] = jnp.full_like(m_scr, NEG)
    l_scr[...] = jnp.zeros_like(l_scr)
    acc_scr[...] = jnp.zeros_like(acc_scr)
    qh = q_ref[...]

    def att_body(j, carry):
        start = pl.multiple_of(j * tk, tk)
        s = _dot_nt(qh, k_ref[pl.ds(start, tk), :]) * scale + bias_scr[:, pl.ds(start, tk)]
        _flash_step(s, v_ref[pl.ds(start, tk), :], m_scr, l_scr, acc_scr, 0)
        return carry

    lax.fori_loop(0, n_att, att_body, 0)
    o_ref[...] = (acc_scr[0] / l_scr[0]).astype(o_ref.dtype)


def dsa_prompt(qi3, tail, ki3, qb, kb, vb, bn, t, wcol):
    heads, _, dh = qb.shape
    tq = min(256, t)
    tk = min(1024, t)
    nq = t // tq
    n_top = min(TOPK_MAX, t // 4)
    return pl.pallas_call(
        functools.partial(_dsa_prompt_kernel, tq=tq, tk=tk, dh=dh, n_top=n_top, wcol=wcol),
        grid=(bn, nq, heads),
        in_specs=[pl.BlockSpec((tq, IDX_HEADS * IDX_PACK), lambda b, i, h: (b * nq + i, 0)),
                  pl.BlockSpec((tq, LANES), lambda b, i, h: (b * nq + i, 0)),
                  pl.BlockSpec((t, IDX_PACK), lambda b, i, h: (b, 0)),
                  pl.BlockSpec((None, tq, dh), lambda b, i, h: (h, b * nq + i, 0)),
                  pl.BlockSpec((None, t, dh), lambda b, i, h: (h, b, 0)),
                  pl.BlockSpec((None, t, dh), lambda b, i, h: (h, b, 0))],
        out_specs=pl.BlockSpec((tq, dh), lambda b, i, h: (b * nq + i, h)),
        out_shape=jax.ShapeDtypeStruct((bn * t, heads * dh), BF16),
        scratch_shapes=[pltpu.VMEM((tq, t), I32), pltpu.VMEM((tq, t), F32),
                        pltpu.VMEM((IDX_HEADS, tq, LANES), F32),
                        pltpu.VMEM((1, tq, 1), F32), pltpu.VMEM((1, tq, 1), F32),
                        pltpu.VMEM((1, tq, dh), F32)],
        compiler_params=_cparams(("arbitrary", "arbitrary", "arbitrary")),
        name="dsa_prompt",
    )(qi3, tail, ki3, qb, kb, vb)


def _dsa_index_kernel(pt_ref, qi_ref, w_ref, kin_ref, *rest, n_steps, pps, n_top):
    kp_refs, (bias_ref, biasn_ref, sc_scr) = rest[:pps], rest[pps:]
    p = pl.program_id(1)
    idx_scale = IDX_DIM ** -0.5 * IDX_HEADS ** -0.5
    qi = qi_ref[0]
    w = w_ref[0] * idx_scale
    for pp, kp_ref in enumerate(kp_refs):
        s = _dot3(qi, kp_ref[...])
        sc_scr[pl.ds(p * pps + pp, 1), :] = jnp.sum(jnp.maximum(s, 0.0) * w, axis=0, keepdims=True)

    @pl.when(p == n_steps - 1)
    def _():
        s_n = jnp.sum(qi * kin_ref[0], axis=1, keepdims=True)
        sc_new = jnp.sum(jnp.maximum(s_n, 0.0) * w, axis=0, keepdims=True)
        key = _sortable_key(sc_scr[...])
        key_new = _sortable_key(sc_new)

        def count_ge(cands):
            (cand,) = cands
            hit = jnp.where(key >= cand, 1.0, 0.0)
            cnt = jnp.sum(jnp.sum(hit, axis=1, keepdims=True), axis=0, keepdims=True)
            return [cnt + jnp.where(key_new >= cand, 1.0, 0.0)]

        (thr,) = _bisect_threshold(count_ge, [(1, 1)], n_top)
        bias_ref[0] = jnp.where(key >= thr, 0.0, NEG)
        biasn_ref[0] = jnp.broadcast_to(jnp.where(key_new >= thr, 0.0, NEG), (1, LANES))


def dsa_sample_index(qi, w, ki_new, cache_ki, slot, page_table):
    bs, n_pages = page_table.shape
    n_top = min(TOPK_MAX, (n_pages * PAGE_SIZE + 1) // 4)
    pps = max(c for c in (16, 8, 4, 2, 1) if n_pages % c == 0)
    n_steps = n_pages // pps
    grid_spec = pltpu.PrefetchScalarGridSpec(
        num_scalar_prefetch=1,
        grid=(bs, n_steps),
        in_specs=[pl.BlockSpec((1, IDX_HEADS, IDX_DIM), lambda b, p, pt: (b, 0, 0)),
                  pl.BlockSpec((1, IDX_HEADS, 1), lambda b, p, pt: (b, 0, 0)),
                  pl.BlockSpec((1, 1, IDX_DIM), lambda b, p, pt: (b, 0, 0))]
        + [pl.BlockSpec((None, None, IDX_DIM, PAGE_SIZE),
                        lambda b, p, pt, pp=pp: (slot, pt[b, p * pps + pp], 0, 0)) for pp in range(pps)],
        out_specs=[pl.BlockSpec((1, n_pages, PAGE_SIZE), lambda b, p, pt: (b, 0, 0)),
                   pl.BlockSpec((1, 1, LANES), lambda b, p, pt: (b, 0, 0))],
        scratch_shapes=[pltpu.VMEM((n_pages, PAGE_SIZE), F32)],
    )
    bias, bias_new = pl.pallas_call(
        functools.partial(_dsa_index_kernel, n_steps=n_steps, pps=pps, n_top=n_top),
        grid_spec=grid_spec,
        out_shape=[jax.ShapeDtypeStruct((bs, n_pages, PAGE_SIZE), F32),
                   jax.ShapeDtypeStruct((bs, 1, LANES), F32)],
        compiler_params=_cparams(("arbitrary", "arbitrary")),
        name="dsa_index",
    )(page_table, qi, w, ki_new, *([cache_ki] * pps))
    return bias, bias_new


def _rope_tables(pos, half):
    inv = ROPE_THETA ** (-jnp.arange(half, dtype=F32) / half)
    ang = pos.astype(F32)[:, None] * inv[None, :]
    return jnp.cos(ang), jnp.sin(ang)


def _tables_128(pos):
    c, s = _rope_tables(pos, 64)
    return jnp.concatenate([c, c], -1), jnp.concatenate([-s, s], -1)


def _tables_64(pos, tail):
    c, s = _rope_tables(pos, 32)
    if tail:
        one, zero = jnp.ones_like(c), jnp.zeros_like(s)
        return jnp.concatenate([c, c, one, one], -1), jnp.concatenate([-s, s, zero, zero], -1)
    return jnp.concatenate([c, c, c, c], -1), jnp.concatenate([-s, s, -s, s], -1)


def _pad_cols(w, n):
    return jnp.pad(w, ((0, 0), (0, 0), (0, n - w.shape[-1])))


def kernel(x_prompt, x_sample, c_prompt, c_sample, state_a_C, state_a_n, state_a_m, cache_b_k,
           cache_b_v, cache_c_k, cache_c_v, cache_c_kidx, page_table, ada_w, ada_b, norm_g, ffn_wi,
           ffn_wo, a_w_in, a_b_if, a_norm_g, a_w_out, b_w_in, b_lambda, b_norm_g, b_w_out, c_w_in,
           c_w_out, final_g):
    bp, t_p, d = x_prompt.shape
    bs = x_sample.shape[0]
    depth = ada_w.shape[0]
    n_pages = page_table.shape[1]
    past = n_pages * PAGE_SIZE
    n_pool = cache_b_k.shape[1]
    rs = SUBLANES_BF16 * ((bs + SUBLANES_BF16 - 1) // SUBLANES_BF16)
    mp = bp * t_p

    ff = ffn_wo.shape[2]
    fp = FFN_TF * ((ff + FFN_TF - 1) // FFN_TF)
    wi_p, wo_p = stage_ffn_weights(ffn_wi, ffn_wo, fp)

    c_rows = jnp.zeros((rs, d), F32).at[:bp].set(c_prompt).at[bp:bp + bs].set(c_sample)
    mod = ada_all(c_rows, ada_w, ada_b).reshape(depth, rs, N_ADA, d)

    xp = x_prompt.reshape(mp, d)
    xs = jnp.zeros((rs, d), F32).at[:bs].set(x_sample.reshape(bs, d))

    pos_p = jnp.arange(t_p, dtype=jnp.int32)
    pos_s = jnp.full((rs,), past, jnp.int32)
    tab128_p, tab128_s = _tables_128(pos_p), _tables_128(pos_s)
    tab64_p, tab64_s = _tables_64(pos_p, False), _tables_64(pos_s, False)
    tabt_p, tabt_s = _tables_64(pos_p, True), _tables_64(pos_s, True)

    a_main = 2 * A_HEADS * (a_norm_g.shape[-1] // 2) + 2 * A_HEADS * a_norm_g.shape[-1]
    a_tail_w = _pad_cols(a_w_in[:, :, a_main:], LANES)
    c_main = 3 * d + IDX_HEADS * IDX_DIM
    c_tail_w = _pad_cols(c_w_in[:, :, c_main:], LANES)
    a_w_in, b_w_in, c_w_in = (w.astype(BF16) for w in (a_w_in, b_w_in, c_w_in))
    a_w_out, b_w_out, c_w_out = (w.astype(BF16) for w in (a_w_out, b_w_out, c_w_out))

    kind_of = [i % 3 for i in range(depth)]
    outs = {k: [] for k in ("aCp", "anp", "amp", "aCs", "ans", "ams", "bkp", "bvp", "bks", "bvs",
                            "ckp", "cvp", "cip", "cks", "cvs", "cis")}
    for l in range(depth):
        kind = kind_of[l]
        j = kind_of[:l].count(kind)
        mp_l = [mod[l, :bp, k][:, None, :] for k in range(N_ADA)]
        ms_l = [mod[l, bp:bp + bs, k] for k in range(N_ADA)]
        ms_l = [jnp.zeros((1, rs, d), F32).at[0, :bs].set(m) for m in ms_l]

        xp = ffn_sublayer(xp, mp_l[0], mp_l[1], mp_l[2], norm_g[l, 0], wi_p, wo_p, l, 0, t_p)
        xs = ffn_sublayer(xs, ms_l[0], ms_l[1], ms_l[2], norm_g[l, 0], wi_p, wo_p, l, 0, rs)
        pj_p = functools.partial(proj, xp, mp_l[3], mp_l[4], norm_g[l, 1], rows_per_batch=t_p)
        pj_s = functools.partial(proj, xs, ms_l[3], ms_l[4], norm_g[l, 1], rows_per_batch=rs)

        if kind == 0:
            (y_p,) = pj_p(a_w_in, j, 0, a_main, dtypes=(F32,))
            (g_p,) = pj_p(a_tail_w, j, 0, LANES, dtypes=(F32,), precise=True)
            hh_p, c_p, n_p, m_p = mlstm_prompt(y_p, g_p, a_b_if[j], a_norm_g[j], bp, t_p)
            (y_s,) = pj_s(a_w_in, j, 0, a_main, dtypes=(F32,))
            (g_s,) = pj_s(a_tail_w, j, 0, LANES, dtypes=(F32,), precise=True)
            hh_s, c_s, n_s, m_s = mlstm_sample(y_s, g_s, a_b_if[j], a_norm_g[j],
                                               state_a_C[j], state_a_n[j], state_a_m[j], bs)
            op_in = hh_p
            os_in = jnp.zeros((rs, hh_s.shape[1]), BF16).at[:bs].set(hh_s.astype(BF16))
            w_out = a_w_out
            outs["aCp"].append(c_p); outs["anp"].append(n_p); outs["amp"].append(m_p)
            outs["aCs"].append(c_s); outs["ans"].append(n_s); outs["ams"].append(m_s)
        elif kind == 1:
            lam_init = 0.8 - 0.6 * math.exp(-0.3 * l)
            bqk = b_w_in.shape[2] // 3
            (q_pb,) = pj_p(b_w_in, j, 0, bqk, dtypes=(BF16,), rope=128, tables=tab128_p)
            k_p, k_pb = pj_p(b_w_in, j, bqk, bqk, dtypes=(F32, BF16), rope=128, tables=tab128_p)
            v_p, v_pb = pj_p(b_w_in, j, 2 * bqk, bqk, dtypes=(F32, BF16))
            op_in = diff_prompt(q_pb, k_pb, v_pb, b_lambda[j], b_norm_g[j], lam_init, bp, t_p)
            (q_s,) = pj_s(b_w_in, j, 0, bqk, dtypes=(F32,), rope=128, tables=tab128_s)
            (k_s,) = pj_s(b_w_in, j, bqk, bqk, dtypes=(F32,), rope=128, tables=tab128_s)
            (v_s,) = pj_s(b_w_in, j, 2 * bqk, bqk, dtypes=(F32,))
            comp_major = lambda a: a[:bs].reshape(bs, B_HEADS, 2, -1).transpose(0, 2, 1, 3).reshape(
                bs, 2 * B_HEADS, -1)
            o_s = paged_attention(
                "diff", comp_major(q_s), comp_major(k_s), v_s[:bs].reshape(bs, B_HEADS, -1),
                cache_b_k.reshape(cache_b_k.shape[0], n_pool, PAGE_SIZE * B_HEADS * 2, -1),
                cache_b_v.reshape(cache_b_v.shape[0], n_pool, PAGE_SIZE * B_HEADS, -1),
                j, page_table, (b_lambda[j], b_norm_g[j]), lam_init)
            os_in = jnp.zeros((rs, bqk), BF16).at[:bs].set(o_s.reshape(bs, bqk).astype(BF16))
            w_out = b_w_out
            outs["bkp"].append(k_p.reshape(bp, t_p, B_HEADS, 2, -1))
            outs["bvp"].append(v_p.reshape(bp, t_p, B_HEADS, -1))
            outs["bks"].append(k_s[:bs].reshape(bs, 1, B_HEADS, 2, -1))
            outs["bvs"].append(v_s[:bs].reshape(bs, 1, B_HEADS, -1))
        else:
            cw = d
            ni = IDX_HEADS * IDX_DIM
            (q_pb,) = pj_p(c_w_in, j, 0, cw, dtypes=(HEADS,), rope=128, tables=tab128_p)
            k_p, k_pb = pj_p(c_w_in, j, cw, cw, dtypes=(F32, HEADS), rope=128, tables=tab128_p)
            v_p, v_pb = pj_p(c_w_in, j, 2 * cw, cw, dtypes=(F32, HEADS))
            (qi_p3,) = pj_p(c_w_in, j, 3 * cw, ni, dtypes=(Q3,), rope=64, tables=tab64_p)
            t_pf, ki_p3 = pj_p(c_tail_w, j, 0, LANES, dtypes=(F32, K3), rope=64, tables=tabt_p)
            op_in = dsa_prompt(qi_p3, t_pf, ki_p3, q_pb, k_pb, v_pb, bp, t_p, IDX_DIM)
            (q_s,) = pj_s(c_w_in, j, 0, cw, dtypes=(F32,), rope=128, tables=tab128_s)
            (k_s,) = pj_s(c_w_in, j, cw, cw, dtypes=(F32,), rope=128, tables=tab128_s)
            (v_s,) = pj_s(c_w_in, j, 2 * cw, cw, dtypes=(F32,))
            (qi_s,) = pj_s(c_w_in, j, 3 * cw, ni, dtypes=(F32,), rope=64, tables=tab64_s)
            (t_s,) = pj_s(c_tail_w, j, 0, LANES, dtypes=(F32,), rope=64, tables=tabt_s)
            ki_s = t_s[:bs, :IDX_DIM]
            bias, bias_new = dsa_sample_index(
                qi_s[:bs].reshape(bs, IDX_HEADS, IDX_DIM),
                t_s[:bs, IDX_DIM:IDX_DIM + IDX_HEADS].reshape(bs, IDX_HEADS, 1),
                ki_s[:, None], jnp.swapaxes(cache_c_kidx, 2, 3), j, page_table)
            o_s = paged_attention(
                "dsa", q_s[:bs].reshape(bs, C_HEADS, -1), k_s[:bs].reshape(bs, C_HEADS, -1),
                v_s[:bs].reshape(bs, C_HEADS, -1),
                cache_c_k.reshape(cache_c_k.shape[0], n_pool, PAGE_SIZE * C_HEADS, -1),
                cache_c_v.reshape(cache_c_v.shape[0], n_pool, PAGE_SIZE * C_HEADS, -1),
                j, page_table, (bias, bias_new))
            os_in = jnp.zeros((rs, cw), BF16).at[:bs].set(o_s.reshape(bs, cw).astype(BF16))
            w_out = c_w_out
            outs["ckp"].append(k_p.reshape(bp, t_p, C_HEADS, -1))
            outs["cvp"].append(v_p.reshape(bp, t_p, C_HEADS, -1))
            outs["cip"].append(t_pf[:, :IDX_DIM].reshape(bp, t_p, IDX_DIM))
            outs["cks"].append(k_s[:bs].reshape(bs, 1, C_HEADS, -1))
            outs["cvs"].append(v_s[:bs].reshape(bs, 1, C_HEADS, -1))
            outs["cis"].append(ki_s.reshape(bs, 1, IDX_DIM))

        xp = outproj(op_in, xp, mp_l[5], w_out, j, t_p)
        xs = outproj(os_in, xs, ms_l[5], w_out, j, rs)
        fin = final_g if l == depth - 1 else None
        xp = ffn_sublayer(xp, mp_l[6], mp_l[7], mp_l[8], norm_g[l, 2], wi_p, wo_p, l, 1, t_p, fin)
        xs = ffn_sublayer(xs, ms_l[6], ms_l[7], ms_l[8], norm_g[l, 2], wi_p, wo_p, l, 1, rs, fin)

    st = lambda k: jnp.stack(outs[k])
    return (xp.reshape(bp, t_p, d), xs[:bs].reshape(bs, 1, d),
            st("aCp"), st("anp"), st("amp"), st("aCs"), st("ans"), st("ams"),
            st("bkp"), st("bvp"), st("bks"), st("bvs"),
            st("ckp"), st("cvp"), st("cip"), st("cks"), st("cvs"), st("cis"))
```
